```python
import math
import jax
import jax.numpy as jnp
from jax import lax
import numpy as np

D_MODEL = 1024
BATCH = 4
SEQ = 4096
DEPTH = 4

CHUNK = 64
EPS = 1e-6
D_FF = 2816

SSD_HEADS = 8
SSD_HEAD_DIM = 64
SSD_INNER = SSD_HEADS * SSD_HEAD_DIM
SSD_GROUPS = 2
SSD_STATE = 128
SSD_CONV = 4
SSD_XBC = SSD_INNER + 2 * SSD_GROUPS * SSD_STATE
SSD_HPG = SSD_HEADS // SSD_GROUPS

S5_WIDTH = 512
S5_GROUP = 16
S5_GROUPS = S5_WIDTH // S5_GROUP
S5_STATE = 64

ATT_HEADS = 8
ATT_HEAD_DIM = 64
ATT_WIDTH = ATT_HEADS * ATT_HEAD_DIM
LEFT_CHUNKS = 8
BAND_CHUNKS = LEFT_CHUNKS + 1
BAND = BAND_CHUNKS * CHUNK
MAX_REL = 128

POOL_WINDOWS = (2, 4, 8, 16)
POOL_GROUPS = len(POOL_WINDOWS)
POOL_WIDTH = 512
POOL_GROUP = POOL_WIDTH // POOL_GROUPS
POOL_MAX = max(POOL_WINDOWS)

EVEN_IN = SSD_INNER + SSD_XBC + SSD_HEADS + S5_WIDTH
EVEN_OUT = SSD_INNER + S5_WIDTH
ODD_IN = 3 * ATT_WIDTH + POOL_WIDTH
ODD_OUT = ATT_WIDTH + POOL_WIDTH
N_EVEN = (DEPTH + 1) // 2
N_ODD = DEPTH // 2

kernel_name = 'hybrid_chunk_causal_encoder'


def _rmsnorm(x, gain):
    xf = x.astype(jnp.float32)
    y = xf * lax.rsqrt(jnp.mean(xf * xf, axis=-1, keepdims=True) + EPS)
    return (y * gain.astype(jnp.float32)).astype(x.dtype)


def _swiglu(h, w_gate, w_up, w_down):
    return (jax.nn.silu(h @ w_gate) * (h @ w_up)) @ w_down


def _causal_depthwise_conv(x, w, b):
    c = x.shape[-1]
    y = lax.conv_general_dilated(
        x.astype(jnp.float32), w.astype(jnp.float32)[:, None, :],
        window_strides=(1,), padding=[(SSD_CONV - 1, 0)],
        dimension_numbers=('NWC', 'WIO', 'NWC'), feature_group_count=c)
    return y + b.astype(jnp.float32)


def _ssd_mixer(zxbcdt, conv_w, conv_b, dt_bias, a_log, d_skip, norm_gain):
    f32 = jnp.float32
    bsz, s, _ = zxbcdt.shape
    nc = s // CHUNK
    z, xbc, dt = jnp.split(zxbcdt, [SSD_INNER, SSD_INNER + SSD_XBC], axis=-1)
    xbc = jax.nn.silu(_causal_depthwise_conv(xbc, conv_w, conv_b))
    xs, bm, cm = jnp.split(xbc, [SSD_INNER, SSD_INNER + SSD_GROUPS * SSD_STATE], axis=-1)
    xs = xs.reshape(bsz, nc, CHUNK, SSD_GROUPS, SSD_HPG, SSD_HEAD_DIM)
    bm = bm.reshape(bsz, nc, CHUNK, SSD_GROUPS, SSD_STATE)
    cm = cm.reshape(bsz, nc, CHUNK, SSD_GROUPS, SSD_STATE)
    dt = jax.nn.softplus(dt.astype(f32) + dt_bias.astype(f32))
    dt = dt.reshape(bsz, nc, CHUNK, SSD_GROUPS, SSD_HPG)
    a = -jnp.exp(a_log.astype(f32)).reshape(SSD_GROUPS, SSD_HPG)
    a_cs = jnp.cumsum(dt * a, axis=2)
    xdt = xs * dt[..., None]
    causal = jnp.tril(jnp.ones((CHUNK, CHUNK), dtype=bool))[:, :, None, None]
    seg = a_cs[:, :, :, None] - a_cs[:, :, None, :]
    decay = jnp.exp(jnp.where(causal, seg, -jnp.inf))
    cb = jnp.einsum('bclgn,bcsgn->bclsg', cm, bm)
    y_diag = jnp.einsum('bclsg,bclsgj,bcsgjp->bclgjp', cb, decay, xdt)
    to_end = jnp.exp(a_cs[:, :, -1:] - a_cs)
    states = jnp.einsum('bclgn,bclgj,bclgjp->bcgjpn', bm, to_end, xdt)
    chunk_decay = jnp.exp(a_cs[:, :, -1])

    def step(h, inp):
        st, dec = inp
        return dec[..., None, None] * h + st, h

    h0 = jnp.zeros((bsz, SSD_GROUPS, SSD_HPG, SSD_HEAD_DIM, SSD_STATE), f32)
    _, h_prev = lax.scan(step, h0, (jnp.moveaxis(states, 1, 0), jnp.moveaxis(chunk_decay, 1, 0)))
    h_prev = jnp.moveaxis(h_prev, 0, 1)
    y_off = jnp.einsum('bclgn,bcgjpn,bclgj->bclgjp', cm, h_prev, jnp.exp(a_cs))
    y = y_diag + y_off + d_skip.astype(f32).reshape(SSD_GROUPS, SSD_HPG, 1) * xs
    y = y.reshape(bsz, s, SSD_INNER) * jax.nn.silu(z.astype(f32))
    yg = y.reshape(bsz, s, SSD_GROUPS, SSD_INNER // SSD_GROUPS)
    yg = yg * lax.rsqrt(jnp.mean(yg * yg, axis=-1, keepdims=True) + EPS)
    return (yg.reshape(bsz, s, SSD_INNER) * norm_gain.astype(f32)).astype(zxbcdt.dtype)


def _complex_affine_combine(e1, e2):
    a1r, a1i, x1r, x1i = e1
    a2r, a2i, x2r, x2i = e2
    return (a2r * a1r - a2i * a1i,
            a2r * a1i + a2i * a1r,
            a2r * x1r - a2i * x1i + x2r,
            a2r * x1i + a2i * x1r + x2i)


def _s5_mixer(u, a_re, a_im, log_dt, b_re, b_im, c_re, c_im, d_skip, w_glu, b_glu):
    f32 = jnp.float32
    bsz, s, _ = u.shape
    uf = u.astype(f32).reshape(bsz, s, S5_GROUPS, S5_GROUP)
    ar = a_re.astype(f32)
    ai = a_im.astype(f32)
    dt = jnp.exp(log_dt.astype(f32))[:, None]
    mag = jnp.exp(dt * ar)
    abar_re = mag * jnp.cos(dt * ai)
    abar_im = mag * jnp.sin(dt * ai)
    den = ar * ar + ai * ai
    f_re = ((abar_re - 1.0) * ar + abar_im * ai) / den
    f_im = (abar_im * ar - (abar_re - 1.0) * ai) / den
    br = b_re.astype(f32)
    bi = b_im.astype(f32)
    bb_re = f_re[..., None] * br - f_im[..., None] * bi
    bb_im = f_re[..., None] * bi + f_im[..., None] * br
    drive_re = jnp.einsum('bsgi,gpi->sbgp', uf, bb_re)
    drive_im = jnp.einsum('bsgi,gpi->sbgp', uf, bb_im)
    shape_a = (s, 1, S5_GROUPS, S5_STATE)
    dec_re = jnp.broadcast_to(abar_re, shape_a)
    dec_im = jnp.broadcast_to(abar_im, shape_a)
    _, _, h_re, h_im = lax.associative_scan(
        _complex_affine_combine, (dec_re, dec_im, drive_re, drive_im), axis=0)
    y = (jnp.einsum('gip,sbgp->bsgi', c_re.astype(f32), h_re)
         - jnp.einsum('gip,sbgp->bsgi', c_im.astype(f32), h_im)
         + d_skip.astype(f32) * uf)
    y = y.reshape(bsz, s, S5_WIDTH)
    gate = jax.nn.gelu(y) @ w_glu.astype(f32) + b_glu.astype(f32)
    return (y * jax.nn.sigmoid(gate)).astype(u.dtype)


def _chunk_band_attention(q, k, v, q_gain, k_gain, rel_bias):
    f32 = jnp.float32
    bsz, s, _ = q.shape
    nc = s // CHUNK
    q = _rmsnorm(q.reshape(bsz, nc, CHUNK, ATT_HEADS, ATT_HEAD_DIM), q_gain)
    k = _rmsnorm(k.reshape(bsz, s, ATT_HEADS, ATT_HEAD_DIM), k_gain)
    v = v.reshape(bsz, s, ATT_HEADS, ATT_HEAD_DIM)
    front = ((0, 0), (LEFT_CHUNKS * CHUNK, 0), (0, 0), (0, 0))
    kp = jnp.pad(k, front).reshape(bsz, nc + LEFT_CHUNKS, CHUNK, ATT_HEADS, ATT_HEAD_DIM)
    vp = jnp.pad(v, front).reshape(bsz, nc + LEFT_CHUNKS, CHUNK, ATT_HEADS, ATT_HEAD_DIM)
    band_idx = jnp.arange(nc)[:, None] + jnp.arange(BAND_CHUNKS)[None, :]
    kb = kp[:, band_idx].reshape(bsz, nc, BAND, ATT_HEADS, ATT_HEAD_DIM)
    vb = vp[:, band_idx].reshape(bsz, nc, BAND, ATT_HEADS, ATT_HEAD_DIM)
    scores = jnp.einsum('bclhd,bckhd->bhclk', q.astype(f32), kb.astype(f32)) * (ATT_HEAD_DIM ** -0.5)
    q_off = LEFT_CHUNKS * CHUNK + jnp.arange(CHUNK)
    dist = q_off[:, None] - jnp.arange(BAND)[None, :]
    bias = rel_bias.astype(f32)[:, jnp.clip(dist, -MAX_REL, MAX_REL) + MAX_REL]
    valid = jnp.repeat(band_idx >= LEFT_CHUNKS, CHUNK, axis=1)
    scores = jnp.where(valid[None, None, :, None, :], scores + bias[None, :, None], -jnp.inf)
    probs = jax.nn.softmax(scores, axis=-1)
    out = jnp.einsum('bhclk,bckhd->bclhd', probs, vb.astype(f32))
    return out.reshape(bsz, s, ATT_WIDTH).astype(q.dtype)


def _multiscale_pool(u, w_pool, scale):
    f32 = jnp.float32
    bsz, s, _ = u.shape
    uf = u.astype(f32).reshape(bsz, s, POOL_GROUPS, POOL_GROUP)
    csum = jnp.pad(jnp.cumsum(uf, axis=1), ((0, 0), (POOL_MAX, 0), (0, 0), (0, 0)))
    pos = jnp.arange(s)
    means = []
    for g, w in enumerate(POOL_WINDOWS):
        win = csum[:, POOL_MAX:, g] - csum[:, POOL_MAX - w:POOL_MAX - w + s, g]
        count = jnp.minimum(pos + 1, w).astype(f32)[None, :, None]
        means.append(win / count)
    pooled = jnp.stack(means, axis=2) - uf
    y = jnp.einsum('bsgi,gio->bsgo', pooled, w_pool.astype(f32)).reshape(bsz, s, POOL_WIDTH)
    return (y * scale.astype(f32)).astype(u.dtype)


def setup_inputs(seed: int = 0) -> dict:
    key = jax.random.key(seed)
    keys = iter(jax.random.split(key, 48))
    f32 = jnp.float32

    def normal(shape, scale):
        return scale * jax.random.normal(next(keys), shape, f32)

    def gain(shape):
        return 1.0 + 0.05 * jax.random.normal(next(keys), shape, f32)

    def log_uniform(shape, lo, hi):
        return jax.random.uniform(next(keys), shape, f32, math.log(lo), math.log(hi))

    ssd_dt = jnp.exp(log_uniform((N_EVEN, SSD_HEADS), 1e-3, 1e-1))
    ssd_dt_bias = ssd_dt + jnp.log(-jnp.expm1(-ssd_dt))
    ssd_a_log = jnp.log(jax.random.uniform(next(keys), (N_EVEN, SSD_HEADS), f32, 1.0, 16.0))
    s5_shape = (N_EVEN, S5_GROUPS, S5_STATE)
    s5_a_re = -0.5 + normal(s5_shape, 0.01)
    s5_a_im = math.pi * jnp.arange(S5_STATE, dtype=f32) + normal(s5_shape, 0.01)
    s5_log_dt = log_uniform((N_EVEN, S5_GROUPS), 1e-3, 1e-1)
    return {
        'x': normal((BATCH, SEQ, D_MODEL), 1.0),
        'ffn1_norm': gain((DEPTH, D_MODEL)),
        'ffn1_w_gate': normal((DEPTH, D_MODEL, D_FF), D_MODEL ** -0.5),
        'ffn1_w_up': normal((DEPTH, D_MODEL, D_FF), D_MODEL ** -0.5),
        'ffn1_w_down': normal((DEPTH, D_FF, D_MODEL), D_FF ** -0.5),
        'mix_norm': gain((DEPTH, D_MODEL)),
        'even_w_in': normal((N_EVEN, D_MODEL, EVEN_IN), D_MODEL ** -0.5),
        'even_w_out': normal((N_EVEN, EVEN_OUT, D_MODEL), EVEN_OUT ** -0.5),
        'ssd_conv_w': normal((N_EVEN, SSD_CONV, SSD_XBC), SSD_CONV ** -0.5),
        'ssd_conv_b': normal((N_EVEN, SSD_XBC), 0.02),
        'ssd_dt_bias': ssd_dt_bias,
        'ssd_a_log': ssd_a_log,
        'ssd_d': gain((N_EVEN, SSD_HEADS)),
        'ssd_norm': gain((N_EVEN, SSD_INNER)),
        's5_a_re': s5_a_re,
        's5_a_im': s5_a_im,
        's5_log_dt': s5_log_dt,
        's5_b_re': normal((N_EVEN, S5_GROUPS, S5_STATE, S5_GROUP), (2 * S5_GROUP) ** -0.5),
        's5_b_im': normal((N_EVEN, S5_GROUPS, S5_STATE, S5_GROUP), (2 * S5_GROUP) ** -0.5),
        's5_c_re': normal((N_EVEN, S5_GROUPS, S5_GROUP, S5_STATE), S5_STATE ** -0.5),
        's5_c_im': normal((N_EVEN, S5_GROUPS, S5_GROUP, S5_STATE), S5_STATE ** -0.5),
        's5_d': normal((N_EVEN, S5_GROUPS, S5_GROUP), 1.0),
        's5_w_glu': normal((N_EVEN, S5_WIDTH, S5_WIDTH), S5_WIDTH ** -0.5),
        's5_b_glu': normal((N_EVEN, S5_WIDTH), 0.02),
        'odd_w_in': normal((N_ODD, D_MODEL, ODD_IN), D_MODEL ** -0.5),
        'odd_w_out': normal((N_ODD, ODD_OUT, D_MODEL), ODD_OUT ** -0.5),
        'attn_q_norm': gain((N_ODD, ATT_HEAD_DIM)),
        'attn_k_norm': gain((N_ODD, ATT_HEAD_DIM)),
        'attn_rel_bias': normal((N_ODD, ATT_HEADS, 2 * MAX_REL + 1), 0.1),
        'pool_w': normal((N_ODD, POOL_GROUPS, POOL_GROUP, POOL_GROUP), POOL_GROUP ** -0.5),
        'pool_scale': gain((N_ODD, POOL_WIDTH)),
        'ffn2_norm': gain((DEPTH, D_MODEL)),
        'ffn2_w_gate': normal((DEPTH, D_MODEL, D_FF), D_MODEL ** -0.5),
        'ffn2_w_up': normal((DEPTH, D_MODEL, D_FF), D_MODEL ** -0.5),
        'ffn2_w_down': normal((DEPTH, D_FF, D_MODEL), D_FF ** -0.5),
    }


def reference(x, ffn1_norm, ffn1_w_gate, ffn1_w_up, ffn1_w_down, mix_norm,
              even_w_in, even_w_out, ssd_conv_w, ssd_conv_b, ssd_dt_bias, ssd_a_log, ssd_d, ssd_norm,
              s5_a_re, s5_a_im, s5_log_dt, s5_b_re, s5_b_im, s5_c_re, s5_c_im, s5_d, s5_w_glu, s5_b_glu,
              odd_w_in, odd_w_out, attn_q_norm, attn_k_norm, attn_rel_bias, pool_w, pool_scale,
              ffn2_norm, ffn2_w_gate, ffn2_w_up, ffn2_w_down):
    for layer in range(DEPTH):
        i = layer // 2
        h = _rmsnorm(x, ffn1_norm[layer])
        x = x + 0.5 * _swiglu(h, ffn1_w_gate[layer], ffn1_w_up[layer], ffn1_w_down[layer])
        h = _rmsnorm(x, mix_norm[layer])
        if layer % 2 == 0:
            proj = h @ even_w_in[i]
            zxbcdt, u = jnp.split(proj, [SSD_INNER + SSD_XBC + SSD_HEADS], axis=-1)
            y_a = _ssd_mixer(zxbcdt, ssd_conv_w[i], ssd_conv_b[i], ssd_dt_bias[i],
                             ssd_a_log[i], ssd_d[i], ssd_norm[i])
            y_b = _s5_mixer(u, s5_a_re[i], s5_a_im[i], s5_log_dt[i], s5_b_re[i], s5_b_im[i],
                            s5_c_re[i], s5_c_im[i], s5_d[i], s5_w_glu[i], s5_b_glu[i])
            mixed = jnp.concatenate([y_a, y_b], axis=-1).astype(x.dtype) @ even_w_out[i]
        else:
            proj = h @ odd_w_in[i]
            q, k, v, u = jnp.split(proj, [ATT_WIDTH, 2 * ATT_WIDTH, 3 * ATT_WIDTH], axis=-1)
            y_c = _chunk_band_attention(q, k, v, attn_q_norm[i], attn_k_norm[i], attn_rel_bias[i])
            y_d = _multiscale_pool(u, pool_w[i], pool_scale[i])
            mixed = jnp.concatenate([y_c, y_d], axis=-1).astype(x.dtype) @ odd_w_out[i]
        x = x + mixed.astype(x.dtype)
        h = _rmsnorm(x, ffn2_norm[layer])
        x = x + 0.5 * _swiglu(h, ffn2_w_gate[layer], ffn2_w_up[layer], ffn2_w_down[layer])
    return x
```

```python
import functools

import jax
import jax.numpy as jnp
from jax import lax
from jax.experimental import pallas as pl
from jax.experimental.pallas import tpu as pltpu

F32 = jnp.float32
BF16 = jnp.bfloat16
HIGHEST = lax.Precision.HIGHEST

EPS = 1e-6
LANES = 128
VMEM_LIMIT = 56 * 1024 * 1024

SSD_HEADS = 8
SSD_HEAD_DIM = 64
SSD_INNER = 512
SSD_GROUPS = 2
SSD_STATE = 128
SSD_CONV = 4
SSD_XBC = 1024
S5_WIDTH = 512
S5_GROUP = 16
S5_GROUPS = 32
S5_STATE = 64
ATT_HEADS = 8
ATT_HEAD_DIM = 64
ATT_WIDTH = 512
ATT_CHUNK = 64
LEFT_CHUNKS = 8
MAX_REL = 128
POOL_WINDOWS = (2, 4, 8, 16)
POOL_GROUP = 128
POOL_MAX = 16

FFN_TM = 1024
FFN_TF = 512
PROJ_TM = 512
SSD_Q = 128
S5_Q = 128
ATT_T = 256
POOL_T = 512


def _params(*sem):
    return pltpu.CompilerParams(dimension_semantics=sem, vmem_limit_bytes=VMEM_LIMIT)


def _dot(a, b):
    return jnp.dot(a, b, preferred_element_type=F32)


def _dot_exact(a, b):
    return jnp.dot(a, b, preferred_element_type=F32, precision=HIGHEST)


def _rmsnorm_bf16(x, gain):
    ms = jnp.mean(x * x, axis=-1, keepdims=True)
    return (x * lax.rsqrt(ms + EPS) * gain).astype(BF16)


def _silu(x):
    return x * jax.nn.sigmoid(x)


def _ffn_kernel(x_ref, g_ref, wg_ref, wu_ref, wd_ref, o_ref, h_scr, acc_scr):
    j = pl.program_id(1)

    @pl.when(j == 0)
    def _():
        h_scr[...] = _rmsnorm_bf16(x_ref[...], g_ref[...])
        acc_scr[...] = jnp.zeros_like(acc_scr)

    h = h_scr[...]
    gate = _dot(h, wg_ref[...])
    up = _dot(h, wu_ref[...])
    act = (_silu(gate) * up).astype(BF16)
    acc_scr[...] += _dot(act, wd_ref[...])

    @pl.when(j == pl.num_programs(1) - 1)
    def _():
        o_ref[...] = x_ref[...] + 0.5 * acc_scr[...]


def _ffn(x2d, gain, wg, wu, wd, layer):
    m, d = x2d.shape
    dff = wg.shape[-1]
    tm = min(FFN_TM, m)
    tf = FFN_TF if dff % FFN_TF == 0 else 256
    return pl.pallas_call(
        _ffn_kernel,
        out_shape=jax.ShapeDtypeStruct((m, d), F32),
        grid=(m // tm, dff // tf),
        in_specs=[
            pl.BlockSpec((tm, d), lambda i, j: (i, 0)),
            pl.BlockSpec((None, 1, d), lambda i, j: (layer, 0, 0)),
            pl.BlockSpec((None, d, tf), lambda i, j: (layer, 0, j)),
            pl.BlockSpec((None, d, tf), lambda i, j: (layer, 0, j)),
            pl.BlockSpec((None, tf, d), lambda i, j: (layer, j, 0)),
        ],
        out_specs=pl.BlockSpec((tm, d), lambda i, j: (i, 0)),
        scratch_shapes=[pltpu.VMEM((tm, d), BF16), pltpu.VMEM((tm, d), F32)],
        compiler_params=_params("parallel", "arbitrary"),
        name="ffn",
    )(x2d, gain, wg, wu, wd)


def _inproj_even_kernel(x_ref, g_ref, wz_ref, wx_ref, wdt_ref, wut_ref,
                        z_ref, xbc_ref, dt_ref, ut_ref):
    h = _rmsnorm_bf16(x_ref[...], g_ref[...])
    z_ref[...] = _dot(h, wz_ref[...])
    xbc_ref[...] = _dot(h, wx_ref[...])
    dt_ref[...] = _dot(h, wdt_ref[...])
    ut_ref[0] = lax.dot_general(wut_ref[...], h, (((1,), (1,)), ((), ())),
                                preferred_element_type=F32)


def _inproj_even(x2d, gain, wz, wx, wdt, wut, layer, bsz, seq):
    m, d = x2d.shape
    tm = min(PROJ_TM, seq)
    per_b = seq // tm
    const = lambda i: (0, 0)
    return pl.pallas_call(
        _inproj_even_kernel,
        out_shape=(
            jax.ShapeDtypeStruct((m, SSD_INNER), F32),
            jax.ShapeDtypeStruct((m, SSD_XBC), F32),
            jax.ShapeDtypeStruct((m, LANES), F32),
            jax.ShapeDtypeStruct((bsz, S5_WIDTH, seq), F32),
        ),
        grid=(m // tm,),
        in_specs=[
            pl.BlockSpec((tm, d), lambda i: (i, 0)),
            pl.BlockSpec((None, 1, d), lambda i: (layer, 0, 0)),
            pl.BlockSpec(wz.shape, const),
            pl.BlockSpec(wx.shape, const),
            pl.BlockSpec(wdt.shape, const),
            pl.BlockSpec(wut.shape, const),
        ],
        out_specs=(
            pl.BlockSpec((tm, SSD_INNER), lambda i: (i, 0)),
            pl.BlockSpec((tm, SSD_XBC), lambda i: (i, 0)),
            pl.BlockSpec((tm, LANES), lambda i: (i, 0)),
            pl.BlockSpec((1, S5_WIDTH, tm), lambda i: (i // per_b, 0, i % per_b)),
        ),
        compiler_params=_params("parallel"),
        name="inproj_even",
    )(x2d, gain, wz, wx, wdt, wut)


def _shift_rows(cur, prev8, j):
    rolled = pltpu.roll(cur, j, 0)
    prev_rolled = pltpu.roll(prev8, j, 0)
    row = lax.broadcasted_iota(jnp.int32, prev8.shape, 0)
    top = jnp.where(row < j, prev_rolled, rolled[:8])
    return jnp.concatenate([top, rolled[8:]], axis=0)


def _ssd_kernel(z_ref, xbc_ref, xprev_ref, dt_ref, cw_ref, cb_ref, dtb_ref, alog_ref,
                dx_ref, ng_ref, e_ref, o_ref, h_scr):
    t = pl.program_id(1)
    q = z_ref.shape[0]
    gw = SSD_INNER // SSD_GROUPS
    hpg = SSD_HEADS // SSD_GROUPS

    @pl.when(t == 0)
    def _():
        h_scr[...] = jnp.zeros_like(h_scr)

    xbc = xbc_ref[...]
    prev = jnp.where(t > 0, xprev_ref[...], 0.0)
    conv = xbc * cw_ref[SSD_CONV - 1:SSD_CONV, :] + cb_ref[...]
    for j in range(1, SSD_CONV):
        conv = conv + _shift_rows(xbc, prev, j) * cw_ref[SSD_CONV - 1 - j:SSD_CONV - j, :]
    xc = _silu(conv)
    xs = xc[:, :SSD_INNER]

    dtr = dt_ref[...] + dtb_ref[...]
    dt = jnp.maximum(dtr, 0.0) + jnp.log1p(jnp.exp(-jnp.abs(dtr)))
    a = -jnp.exp(alog_ref[...])
    row = lax.broadcasted_iota(jnp.int32, (q, q), 0)
    col = lax.broadcasted_iota(jnp.int32, (q, q), 1)
    causal = row >= col
    tril = causal.astype(F32)
    acs = _dot_exact(tril, dt * a)
    acs_t = acs.T
    e = e_ref[...]
    acs_x = _dot_exact(acs, e)
    dt_x = _dot_exact(dt, e)
    xdt = xs * dt_x
    decay_in = jnp.exp(acs_x)
    acs_end = acs_x[q - 1:q, :]
    to_end = jnp.exp(acs_end - acs_x)
    chunk_decay = jnp.exp(acs_end)
    xdt_end = xdt * to_end

    lane_head = lax.broadcasted_iota(jnp.int32, (1, gw), 1) // SSD_HEAD_DIM
    ys = []
    for g in range(SSD_GROUPS):
        bm = xc[:, SSD_INNER + g * SSD_STATE:SSD_INNER + (g + 1) * SSD_STATE].astype(BF16)
        cm = xc[:, SSD_INNER + (SSD_GROUPS + g) * SSD_STATE:
                SSD_INNER + (SSD_GROUPS + g + 1) * SSD_STATE].astype(BF16)
        cb = lax.dot_general(cm, bm, (((1,), (1,)), ((), ())), preferred_element_type=F32)
        xdt_g = xdt[:, g * gw:(g + 1) * gw]
        gmats, rhs = [], []
        for j in range(hpg):
            hh = g * hpg + j
            seg = acs[:, hh:hh + 1] - acs_t[hh:hh + 1, :]
            decay = jnp.exp(jnp.where(causal, seg, -jnp.inf))
            gmats.append((cb * decay).astype(BF16))
            rhs.append(jnp.where(lane_head == j, xdt_g, 0.0).astype(BF16))
        y_diag = _dot(jnp.concatenate(gmats, axis=1), jnp.concatenate(rhs, axis=0))
        h_prev = h_scr[g]
        y_off = _dot(cm, h_prev.astype(BF16)) * decay_in[:, g * gw:(g + 1) * gw]
        upd = lax.dot_general(bm, xdt_end[:, g * gw:(g + 1) * gw].astype(BF16),
                              (((0,), (0,)), ((), ())), preferred_element_type=F32)
        h_scr[g] = chunk_decay[:, g * gw:(g + 1) * gw] * h_prev + upd
        ys.append(y_diag + y_off)
    y = jnp.concatenate(ys, axis=1) + dx_ref[...] * xs
    y = y * _silu(z_ref[...])
    outs = []
    for g in range(SSD_GROUPS):
        yg = y[:, g * gw:(g + 1) * gw]
        outs.append(yg * lax.rsqrt(jnp.mean(yg * yg, axis=-1, keepdims=True) + EPS))
    o_ref[...] = jnp.concatenate(outs, axis=1) * ng_ref[...]


def _ssd(z, xbc, dt, conv_w, conv_b, dt_bias, a_log, d_x, norm_gain, expand, bsz, seq):
    m = z.shape[0]
    q = min(SSD_Q, seq)
    nq = seq // q
    const = lambda b, t: (0, 0)
    return pl.pallas_call(
        _ssd_kernel,
        out_shape=jax.ShapeDtypeStruct((m, SSD_INNER), F32),
        grid=(bsz, nq),
        in_specs=[
            pl.BlockSpec((q, SSD_INNER), lambda b, t: (b * nq + t, 0)),
            pl.BlockSpec((q, SSD_XBC), lambda b, t: (b * nq + t, 0)),
            pl.BlockSpec((8, SSD_XBC), lambda b, t: (jnp.maximum((b * nq + t) * (q // 8) - 1, 0), 0)),
            pl.BlockSpec((q, LANES), lambda b, t: (b * nq + t, 0)),
            pl.BlockSpec(conv_w.shape, const),
            pl.BlockSpec(conv_b.shape, const),
            pl.BlockSpec(dt_bias.shape, const),
            pl.BlockSpec(a_log.shape, const),
            pl.BlockSpec(d_x.shape, const),
            pl.BlockSpec(norm_gain.shape, const),
            pl.BlockSpec(expand.shape, const),
        ],
        out_specs=pl.BlockSpec((q, SSD_INNER), lambda b, t: (b * nq + t, 0)),
        scratch_shapes=[pltpu.VMEM((SSD_GROUPS, SSD_STATE, SSD_INNER // SSD_GROUPS), F32)],
        compiler_params=_params("parallel", "arbitrary"),
        name="ssd",
    )(z, xbc, xbc, dt, conv_w, conv_b, dt_bias, a_log, d_x, norm_gain, expand)


def _cpow(base_re, base_im, exponent, nbits):
    shape = jnp.broadcast_shapes(base_re.shape, exponent.shape)
    pr = jnp.ones(shape, F32)
    pi = jnp.zeros(shape, F32)
    sr, si = base_re, base_im
    for k in range(nbits):
        bit = ((exponent >> k) & 1) == 1
        nr = pr * sr - pi * si
        ni = pr * si + pi * sr
        pr = jnp.where(bit, nr, pr)
        pi = jnp.where(bit, ni, pi)
        if k + 1 < nbits:
            sr, si = sr * sr - si * si, 2.0 * sr * si
    return pr, pi


def _s5_discretize(ar, ai, log_dt):
    dt = jnp.exp(log_dt)
    mag = jnp.exp(dt * ar)
    lam_re = mag * jnp.cos(dt * ai)
    lam_im = mag * jnp.sin(dt * ai)
    den = ar * ar + ai * ai
    f_re = ((lam_re - 1.0) * ar + lam_im * ai) / den
    f_im = (lam_im * ar - (lam_re - 1.0) * ai) / den
    return lam_re, lam_im, f_re, f_im


def _s5_kernel(u_ref, acol_ref, arow_ref, ldt_ref, bd_ref, cd_re_ref, cd_im_ref,
               ct_re_ref, ct_im_ref, d_ref, y_ref, kmat_scr, t_scr, f_scr):
    bsz, _, gsz, nc, q = u_ref.shape
    m = bsz * nc
    p2 = 2 * S5_STATE
    nbits = q.bit_length()
    log_dt = ldt_ref[...]

    lane = lax.broadcasted_iota(jnp.int32, (1, p2), 1)
    first_half_l = lane < S5_STATE
    rowp = lax.broadcasted_iota(jnp.int32, (p2, 1), 0)
    first_half_r = rowp < S5_STATE

    ar_r = arow_ref[0:1, :]
    ai_r = arow_ref[1:2, :]
    lr, li, fr, fi = _s5_discretize(ar_r, ai_r, log_dt)
    sign_b = jnp.where(first_half_l, -1.0, 1.0)
    bd1 = bd_ref[...]
    bd2 = pltpu.roll(bd1, S5_STATE, 1) * sign_b
    bb_a = fr * bd1 + fi * bd2
    bb_b = fr * bd2 - fi * bd1

    srow = lax.broadcasted_iota(jnp.int32, (q, 1), 0)
    pr_re, pr_im = _cpow(lr, li, (q - 1) - srow, nbits)
    for i in range(gsz):
        es = pr_re * bb_a[i:i + 1, :] + pr_im * bb_b[i:i + 1, :]
        t_scr[i * q:(i + 1) * q, gsz * q:gsz * q + p2] = es.astype(BF16)

    ar_c = acol_ref[:, 0:1]
    ai_c = acol_ref[:, 1:2]
    lcr, lci, _, _ = _s5_discretize(ar_c, ai_c, log_dt)
    dlane = lax.broadcasted_iota(jnp.int32, (1, q), 1)
    pw_re, pw_im = _cpow(lcr, lci, dlane, nbits)
    p1_re, p1_im = _cpow(lcr, lci, dlane + 1, nbits)
    pow_stack = jnp.where(first_half_r, pw_re, pw_im)

    v1 = bb_a * jnp.where(first_half_l, 1.0, -1.0)
    v2 = -pltpu.roll(bb_a, S5_STATE, 1)
    cre = cd_re_ref[...]
    cim = cd_im_ref[...]
    coef = cre[:, None, :] * v1[None, :, :] + cim[:, None, :] * v2[None, :, :]
    kmat_scr[...] = _dot_exact(coef.reshape(gsz * gsz, p2), pow_stack)

    for o in range(gsz):
        c_r = ct_re_ref[:, o:o + 1]
        c_i = ct_im_ref[:, o:o + 1]
        f_o = jnp.where(first_half_r, c_r * p1_re - c_i * p1_im, -(c_r * p1_im + c_i * p1_re))
        f_scr[:, o * q:(o + 1) * q] = f_o.astype(BF16)

    trow = lax.broadcasted_iota(jnp.int32, (q, q), 0)
    tcol = lax.broadcasted_iota(jnp.int32, (q, q), 1)
    lower = tcol >= trow

    def build(i, carry):
        for o in range(gsz):
            k_row = kmat_scr[pl.ds(o * gsz + i, 1), :]
            blk = pltpu.roll(jnp.broadcast_to(k_row, (q, q)), 0, 1, stride=1, stride_axis=0)
            blk = jnp.where(lower, blk, 0.0).astype(BF16)
            t_scr[pl.ds(pl.multiple_of(i * q, q), q), o * q:(o + 1) * q] = blk
        return carry

    lax.fori_loop(0, gsz, build, 0)

    u_f32 = [u_ref[:, 0, i].reshape(m, q) for i in range(gsz)]
    u_cat = jnp.concatenate([v.astype(BF16) for v in u_f32], axis=1)
    acc = _dot(u_cat, t_scr[...])
    x = acc[:, gsz * q:]

    sq_re, sq_im = lr, li
    for _ in range(q.bit_length() - 1):
        sq_re, sq_im = sq_re * sq_re - sq_im * sq_im, 2.0 * sq_re * sq_im
    crow = lax.broadcasted_iota(jnp.int32, (m, 1), 0) % nc
    k = 1
    while k < nc:
        sh = jnp.where(crow >= k, pltpu.roll(x, k, 0), 0.0)
        x = x + sq_re * sh + (sq_im * sign_b) * pltpu.roll(sh, S5_STATE, 1)
        sq_re, sq_im = sq_re * sq_re - sq_im * sq_im, 2.0 * sq_re * sq_im
        k *= 2
    h_prev = jnp.where(crow >= 1, pltpu.roll(x, 1, 0), 0.0)
    y = acc[:, :gsz * q] + _dot(h_prev.astype(BF16), f_scr[...])
    for o in range(gsz):
        y_o = y[:, o * q:(o + 1) * q] + d_ref[o:o + 1, :] * u_f32[o]
        y_ref[:, 0, o] = y_o.reshape(bsz, nc, q)


def _s5(ut, acol, arow, ldt, bd, cd_re, cd_im, ct_re, ct_im, d_rows):
    bsz, width, seq = ut.shape
    q = min(S5_Q, seq)
    nc = seq // q
    u5 = ut.reshape(bsz, S5_GROUPS, S5_GROUP, nc, q)
    p2 = 2 * S5_STATE
    per_g = lambda g: (g, 0, 0)
    y5 = pl.pallas_call(
        _s5_kernel,
        out_shape=jax.ShapeDtypeStruct(u5.shape, F32),
        grid=(S5_GROUPS,),
        in_specs=[
            pl.BlockSpec((bsz, 1, S5_GROUP, nc, q), lambda g: (0, g, 0, 0, 0)),
            pl.BlockSpec((None, p2, 2), per_g),
            pl.BlockSpec((None, 2, p2), per_g),
            pl.BlockSpec((None, 1, 1), per_g),
            pl.BlockSpec((None, S5_GROUP, p2), per_g),
            pl.BlockSpec((None, S5_GROUP, p2), per_g),
            pl.BlockSpec((None, S5_GROUP, p2), per_g),
            pl.BlockSpec((None, p2, S5_GROUP), per_g),
            pl.BlockSpec((None, p2, S5_GROUP), per_g),
            pl.BlockSpec((None, S5_GROUP, q), per_g),
        ],
        out_specs=pl.BlockSpec((bsz, 1, S5_GROUP, nc, q), lambda g: (0, g, 0, 0, 0)),
        scratch_shapes=[
            pltpu.VMEM((S5_GROUP * S5_GROUP, q), F32),
            pltpu.VMEM((S5_GROUP * q, S5_GROUP * q + p2), BF16),
            pltpu.VMEM((p2, S5_GROUP * q), BF16),
        ],
        compiler_params=_params("parallel"),
        name="s5",
    )(u5, acol, arow, ldt, bd, cd_re, cd_im, ct_re, ct_im, d_rows)
    return y5.reshape(bsz, width, seq)


def _gelu_tanh(x):
    return 0.5 * x * (1.0 + jnp.tanh(0.7978845608028654 * (x + 0.044715 * (x * x * x))))


def _outproj_even_kernel(x_ref, ya_ref, ybt_ref, wglut_ref, bglu_ref, wa_ref, wb_ref, o_ref):
    ybt = ybt_ref[0]
    gate = _dot(wglut_ref[...], _gelu_tanh(ybt).astype(BF16)) + bglu_ref[...]
    yb = (ybt * jax.nn.sigmoid(gate)).T.astype(BF16)
    o_ref[...] = (x_ref[...] + _dot(ya_ref[...].astype(BF16), wa_ref[...])
                  + _dot(yb, wb_ref[...]))


def _outproj_even(x2d, ya, ybt, wglut, bglu, wa, wb, bsz, seq):
    m, d = x2d.shape
    tm = min(PROJ_TM, seq)
    per_b = seq // tm
    const = lambda i: (0, 0)
    return pl.pallas_call(
        _outproj_even_kernel,
        out_shape=jax.ShapeDtypeStruct((m, d), F32),
        grid=(m // tm,),
        in_specs=[
            pl.BlockSpec((tm, d), lambda i: (i, 0)),
            pl.BlockSpec((tm, SSD_INNER), lambda i: (i, 0)),
            pl.BlockSpec((1, S5_WIDTH, tm), lambda i: (i // per_b, 0, i % per_b)),
            pl.BlockSpec(wglut.shape, const),
            pl.BlockSpec(bglu.shape, const),
            pl.BlockSpec(wa.shape, const),
            pl.BlockSpec(wb.shape, const),
        ],
        out_specs=pl.BlockSpec((tm, d), lambda i: (i, 0)),
        compiler_params=_params("parallel"),
        name="outproj_even",
    )(x2d, ya, ybt, wglut, bglu, wa, wb)


def _head_rmsnorm(x, gain, bd):
    sq = x * x
    hi = sq.astype(BF16)
    lo = (sq - hi.astype(F32)).astype(BF16)
    ss = _dot(hi, bd) + _dot(lo, bd)
    return x * lax.rsqrt(ss * (1.0 / ATT_HEAD_DIM) + EPS) * gain


def _inproj_odd_kernel(x_ref, g_ref, wq_ref, wk_ref, wv_ref, wu_ref, qg_ref, kg_ref, bd_ref,
                       q_ref, k_ref, v_ref, u_ref):
    h = _rmsnorm_bf16(x_ref[...], g_ref[...])
    bd = bd_ref[...]
    q = _head_rmsnorm(_dot(h, wq_ref[...]), qg_ref[...], bd)
    q_ref[...] = q * (ATT_HEAD_DIM ** -0.5)
    k_ref[...] = _head_rmsnorm(_dot(h, wk_ref[...]), kg_ref[...], bd)
    v_ref[...] = _dot(h, wv_ref[...])
    u_ref[...] = _dot(h, wu_ref[...])


def _inproj_odd(x2d, gain, wq, wk, wv, wu, qg, kg, bd, layer):
    m, d = x2d.shape
    tm = min(PROJ_TM, m)
    const = lambda i: (0, 0)
    w = ATT_WIDTH
    out = jax.ShapeDtypeStruct((m, w), F32)
    tile = pl.BlockSpec((tm, w), lambda i: (i, 0))
    return pl.pallas_call(
        _inproj_odd_kernel,
        out_shape=(out, out, out, out),
        grid=(m // tm,),
        in_specs=[
            pl.BlockSpec((tm, d), lambda i: (i, 0)),
            pl.BlockSpec((None, 1, d), lambda i: (layer, 0, 0)),
            pl.BlockSpec(wq.shape, const),
            pl.BlockSpec(wk.shape, const),
            pl.BlockSpec(wv.shape, const),
            pl.BlockSpec(wu.shape, const),
            pl.BlockSpec(qg.shape, const),
            pl.BlockSpec(kg.shape, const),
            pl.BlockSpec(bd.shape, const),
        ],
        out_specs=(tile, tile, tile, tile),
        compiler_params=_params("parallel"),
        name="inproj_odd",
    )(x2d, gain, wq, wk, wv, wu, qg, kg, bd)


def _attn_kernel(q_ref, k0_ref, k1_ref, k2_ref, v0_ref, v1_ref, v2_ref, bias_ref, o_ref):
    t = pl.program_id(1)
    tq = q_ref.shape[0]
    q = q_ref[...]
    kcat = jnp.concatenate([k0_ref[...], k1_ref[...], k2_ref[...]], axis=0).astype(BF16)
    vcat = jnp.concatenate([v0_ref[...], v1_ref[...], v2_ref[...]], axis=0)
    colk = lax.broadcasted_iota(jnp.int32, (1, 3 * tq), 1)
    valid = colk >= (2 - t) * tq
    lane = lax.broadcasted_iota(jnp.int32, (1, LANES), 1)
    heads_per_slab = LANES // ATT_HEAD_DIM
    for p in range(ATT_WIDTH // LANES):
        sl = slice(p * LANES, (p + 1) * LANES)
        qp, kp, vp = q[:, sl], kcat[:, sl], vcat[:, sl]
        out = jnp.zeros((tq, LANES), F32)
        for hh in range(heads_per_slab):
            in_head = (lane // ATT_HEAD_DIM) == hh
            qh = jnp.where(in_head, qp, 0.0).astype(BF16)
            s = lax.dot_general(qh, kp, (((1,), (1,)), ((), ())), preferred_element_type=F32)
            s = jnp.where(valid, s + bias_ref[p * heads_per_slab + hh], -jnp.inf)
            e = jnp.exp(s - jnp.max(s, axis=-1, keepdims=True))
            den = jnp.sum(e, axis=-1, keepdims=True)
            vh = jnp.where(in_head, vp, 0.0).astype(BF16)
            out = out + _dot(e.astype(BF16), vh) / den
        o_ref[:, sl] = out


def _attention(q, k, v, bias, bsz, seq):
    m, w = q.shape
    tq = ATT_T
    nt = seq // tq
    cur = lambda b, t: (b * nt + t, 0)
    back1 = lambda b, t: (b * nt + jnp.maximum(t - 1, 0), 0)
    back2 = lambda b, t: (b * nt + jnp.maximum(t - 2, 0), 0)
    blk = lambda f: pl.BlockSpec((tq, w), f)
    return pl.pallas_call(
        _attn_kernel,
        out_shape=jax.ShapeDtypeStruct((m, w), F32),
        grid=(bsz, nt),
        in_specs=[blk(cur), blk(back2), blk(back1), blk(cur), blk(back2), blk(back1), blk(cur),
                  pl.BlockSpec(bias.shape, lambda b, t: (0, 0, 0))],
        out_specs=blk(cur),
        compiler_params=_params("parallel", "arbitrary"),
        name="band_attention",
    )(q, k, k, k, v, v, v, bias)


def _band_bias(rel_bias, tq):
    l = jnp.arange(tq)[:, None]
    j = jnp.arange(3 * tq)[None, :]
    dist = 2 * tq + l - j
    qc = l // ATT_CHUNK
    kc = j // ATT_CHUNK - (2 * tq // ATT_CHUNK - LEFT_CHUNKS)
    in_band = (kc >= qc) & (kc <= qc + LEFT_CHUNKS)
    bias = rel_bias.astype(F32)[:, jnp.clip(dist, -MAX_REL, MAX_REL) + MAX_REL]
    return jnp.where(in_band[None], bias, -jnp.inf)


def _pool_kernel(u_ref, uprev_ref, w_ref, sc_ref, o_ref):
    t = pl.program_id(1)
    tp = u_ref.shape[0]
    u = u_ref[...]
    prev = jnp.where(t > 0, uprev_ref[...], 0.0)
    ext = jnp.concatenate([prev, u], axis=0)
    sums = {}
    s = ext
    w = 1
    while w < POOL_MAX:
        s = s + pltpu.roll(s, w, 0)
        w *= 2
        sums[w] = s[POOL_MAX:]
    pos = t * tp + lax.broadcasted_iota(jnp.int32, (tp, 1), 0)
    outs = []
    for g, win in enumerate(POOL_WINDOWS):
        sl = slice(g * POOL_GROUP, (g + 1) * POOL_GROUP)
        count = jnp.minimum(pos + 1, win).astype(F32)
        pooled = sums[win][:, sl] / count - u[:, sl]
        outs.append(_dot(pooled.astype(BF16), w_ref[g]))
    o_ref[...] = jnp.concatenate(outs, axis=1) * sc_ref[...]


def _pool(u, w_pool, scale, bsz, seq):
    m, w = u.shape
    tp = min(POOL_T, seq)
    nt = seq // tp
    return pl.pallas_call(
        _pool_kernel,
        out_shape=jax.ShapeDtypeStruct((m, w), F32),
        grid=(bsz, nt),
        in_specs=[
            pl.BlockSpec((tp, w), lambda b, t: (b * nt + t, 0)),
            pl.BlockSpec((POOL_MAX, w),
                         lambda b, t: (jnp.maximum((b * nt + t) * (tp // POOL_MAX) - 1, 0), 0)),
            pl.BlockSpec(w_pool.shape, lambda b, t: (0, 0, 0)),
            pl.BlockSpec(scale.shape, lambda b, t: (0, 0)),
        ],
        out_specs=pl.BlockSpec((tp, w), lambda b, t: (b * nt + t, 0)),
        compiler_params=_params("parallel", "arbitrary"),
        name="pool",
    )(u, u, w_pool, scale)


def _outproj_odd_kernel(x_ref, yc_ref, yd_ref, wa_ref, wb_ref, o_ref):
    o_ref[...] = (x_ref[...] + _dot(yc_ref[...].astype(BF16), wa_ref[...])
                  + _dot(yd_ref[...].astype(BF16), wb_ref[...]))


def _outproj_odd(x2d, yc, yd, wa, wb):
    m, d = x2d.shape
    tm = min(PROJ_TM, m)
    const = lambda i: (0, 0)
    return pl.pallas_call(
        _outproj_odd_kernel,
        out_shape=jax.ShapeDtypeStruct((m, d), F32),
        grid=(m // tm,),
        in_specs=[
            pl.BlockSpec((tm, d), lambda i: (i, 0)),
            pl.BlockSpec((tm, ATT_WIDTH), lambda i: (i, 0)),
            pl.BlockSpec((tm, ATT_WIDTH), lambda i: (i, 0)),
            pl.BlockSpec(wa.shape, const),
            pl.BlockSpec(wb.shape, const),
        ],
        out_specs=pl.BlockSpec((tm, d), lambda i: (i, 0)),
        compiler_params=_params("parallel"),
        name="outproj_odd",
    )(x2d, yc, yd, wa, wb)


def _even_mixer(x2d, gain, i, bsz, seq, even_w_in, even_w_out, ssd_conv_w, ssd_conv_b, ssd_dt_bias,
                ssd_a_log, ssd_d, ssd_norm, s5_a_re, s5_a_im, s5_log_dt, s5_b_re, s5_b_im,
                s5_c_re, s5_c_im, s5_d, s5_w_glu, s5_b_glu):
    w_in = even_w_in[i]
    o1 = SSD_INNER
    o2 = o1 + SSD_XBC
    o3 = o2 + SSD_HEADS
    wz = w_in[:, :o1].astype(BF16)
    wx = w_in[:, o1:o2].astype(BF16)
    wdt = jnp.pad(w_in[:, o2:o3], ((0, 0), (0, LANES - SSD_HEADS))).astype(BF16)
    wut = w_in[:, o3:].T.astype(BF16)
    z, xbc, dt, ut = _inproj_even(x2d, gain, wz, wx, wdt, wut, i * 2, bsz, seq)

    pad_h = lambda v: jnp.pad(v.astype(F32), (0, LANES - SSD_HEADS))[None, :]
    expand = (jnp.arange(LANES)[:, None] == (jnp.arange(SSD_INNER)[None, :] // SSD_HEAD_DIM)).astype(F32)
    ya = _ssd(z, xbc, dt, ssd_conv_w[i], ssd_conv_b[i][None, :], pad_h(ssd_dt_bias[i]),
              pad_h(ssd_a_log[i]), jnp.repeat(ssd_d[i], SSD_HEAD_DIM)[None, :], ssd_norm[i][None, :],
              expand, bsz, seq)

    q = min(S5_Q, seq)
    are, aim = s5_a_re[i], s5_a_im[i]
    dup = lambda v: jnp.concatenate([v, v], axis=-1)
    arow = jnp.stack([dup(are), dup(aim)], axis=1)
    acol = jnp.swapaxes(arow, 1, 2)
    ldt = s5_log_dt[i][:, None, None]
    bd = jnp.concatenate([jnp.swapaxes(s5_b_re[i], 1, 2), jnp.swapaxes(s5_b_im[i], 1, 2)], axis=-1)
    cd_re, cd_im = dup(s5_c_re[i]), dup(s5_c_im[i])
    ct_re, ct_im = jnp.swapaxes(cd_re, 1, 2), jnp.swapaxes(cd_im, 1, 2)
    d_rows = jnp.broadcast_to(s5_d[i][:, :, None], (S5_GROUPS, S5_GROUP, q))
    ybt = _s5(ut, acol, arow, ldt, bd, cd_re, cd_im, ct_re, ct_im, d_rows)

    w_out = even_w_out[i]
    return _outproj_even(x2d, ya, ybt, s5_w_glu[i].T.astype(BF16), s5_b_glu[i][:, None],
                         w_out[:SSD_INNER].astype(BF16), w_out[SSD_INNER:].astype(BF16), bsz, seq)


def _odd_mixer(x2d, gain, i, bsz, seq, odd_w_in, odd_w_out, attn_q_norm, attn_k_norm, attn_rel_bias,
               pool_w, pool_scale):
    w_in = odd_w_in[i].astype(BF16)
    w = ATT_WIDTH
    tile_gain = lambda v: jnp.tile(v.astype(F32), ATT_HEADS)[None, :]
    head = jnp.arange(w) // ATT_HEAD_DIM
    bd = (head[:, None] == head[None, :]).astype(BF16)
    q, k, v, u = _inproj_odd(x2d, gain, w_in[:, :w], w_in[:, w:2 * w], w_in[:, 2 * w:3 * w], w_in[:, 3 * w:],
                             tile_gain(attn_q_norm[i]), tile_gain(attn_k_norm[i]), bd, i * 2 + 1)
    yc = _attention(q, k, v, _band_bias(attn_rel_bias[i], ATT_T), bsz, seq)
    yd = _pool(u, pool_w[i].astype(BF16), pool_scale[i][None, :], bsz, seq)
    w_out = odd_w_out[i]
    return _outproj_odd(x2d, yc, yd, w_out[:w].astype(BF16), w_out[w:].astype(BF16))


def kernel(x, ffn1_norm, ffn1_w_gate, ffn1_w_up, ffn1_w_down, mix_norm, even_w_in, even_w_out, ssd_conv_w, ssd_conv_b, ssd_dt_bias, ssd_a_log, ssd_d, ssd_norm, s5_a_re, s5_a_im, s5_log_dt, s5_b_re, s5_b_im, s5_c_re, s5_c_im, s5_d, s5_w_glu, s5_b_glu, odd_w_in, odd_w_out, attn_q_norm, attn_k_norm, attn_rel_bias, pool_w, pool_scale, ffn2_norm, ffn2_w_gate, ffn2_w_up, ffn2_w_down):
    bsz, seq, d = x.shape
    depth = ffn1_norm.shape[0]
    x2d = x.reshape(bsz * seq, d)
    g3 = lambda v: v.astype(F32)[:, None, :]
    f1n, mxn, f2n = g3(ffn1_norm), g3(mix_norm), g3(ffn2_norm)
    f1g, f1u, f1d = ffn1_w_gate.astype(BF16), ffn1_w_up.astype(BF16), ffn1_w_down.astype(BF16)
    f2g, f2u, f2d = ffn2_w_gate.astype(BF16), ffn2_w_up.astype(BF16), ffn2_w_down.astype(BF16)
    for layer in range(depth):
        i = layer // 2
        x2d = _ffn(x2d, f1n, f1g, f1u, f1d, layer)
        if layer % 2 == 0:
            x2d = _even_mixer(x2d, mxn, i, bsz, seq, even_w_in, even_w_out, ssd_conv_w, ssd_conv_b,
                              ssd_dt_bias, ssd_a_log, ssd_d, ssd_norm, s5_a_re, s5_a_im, s5_log_dt,
                              s5_b_re, s5_b_im, s5_c_re, s5_c_im, s5_d, s5_w_glu, s5_b_glu)
        else:
            x2d = _odd_mixer(x2d, mxn, i, bsz, seq, odd_w_in, odd_w_out, attn_q_norm, attn_k_norm,
                             attn_rel_bias, pool_w, pool_scale)
        x2d = _ffn(x2d, f2n, f2g, f2u, f2d, layer)
    return x2d.reshape(bsz, seq, d)
```

```python
import functools

import jax
import jax.numpy as jnp
from jax import lax
from jax.experimental import pallas as pl
from jax.experimental.pallas import tpu as pltpu

F32 = jnp.float32
BF16 = jnp.bfloat16
HIGHEST = lax.Precision.HIGHEST

EPS = 1e-6
LANES = 128
VMEM_LIMIT = 56 * 1024 * 1024

SSD_HEADS = 8
SSD_HEAD_DIM = 64
SSD_INNER = 512
SSD_GROUPS = 2
SSD_STATE = 128
SSD_CONV = 4
SSD_XBC = 1024
S5_WIDTH = 512
S5_GROUP = 16
S5_GROUPS = 32
S5_STATE = 64
ATT_HEADS = 8
ATT_HEAD_DIM = 64
ATT_WIDTH = 512
ATT_CHUNK = 64
LEFT_CHUNKS = 8
MAX_REL = 128
POOL_WINDOWS = (2, 4, 8, 16)
POOL_GROUP = 128
POOL_MAX = 16

FFN_TM = 512
FFN_TF = 256
PROJ_TM = 512
SSD_Q = 128
S5_Q = 128
ATT_T = 256
POOL_T = 512


def _params(*sem):
    return pltpu.CompilerParams(dimension_semantics=sem, vmem_limit_bytes=VMEM_LIMIT)


def _dot(a, b):
    return jnp.dot(a, b, preferred_element_type=F32)


def _dot_exact(a, b):
    return jnp.dot(a, b, preferred_element_type=F32, precision=HIGHEST)


def _rmsnorm_bf16(x, gain):
    ms = jnp.mean(x * x, axis=-1, keepdims=True)
    return (x * lax.rsqrt(ms + EPS) * gain).astype(BF16)


def _silu(x):
    return x * jax.nn.sigmoid(x)


def _ffn_kernel(x_ref, g_ref, wg_ref, wu_ref, wd_ref, o_ref, act_scr):
    x = x_ref[...]
    h = _rmsnorm_bf16(x, g_ref[...])
    dff = wg_ref.shape[1]
    for c in range(0, dff, FFN_TF):
        gate = _dot(h, wg_ref[:, c:c + FFN_TF])
        up = _dot(h, wu_ref[:, c:c + FFN_TF])
        act_scr[:, c:c + FFN_TF] = (_silu(gate) * up).astype(BF16)
    o_ref[...] = x + 0.5 * _dot(act_scr[...], wd_ref[...])


def _ffn(x2d, gain, wg, wu, wd, layer):
    m, d = x2d.shape
    dff = wg.shape[-1]
    tm = min(FFN_TM, m)
    assert dff % FFN_TF == 0 and m % tm == 0
    resident = pl.Buffered(1)
    return pl.pallas_call(
        _ffn_kernel,
        out_shape=jax.ShapeDtypeStruct((m, d), F32),
        grid=(m // tm,),
        in_specs=[
            pl.BlockSpec((tm, d), lambda i: (i, 0)),
            pl.BlockSpec((None, 1, d), lambda i: (layer, 0, 0)),
            pl.BlockSpec((None, d, dff), lambda i: (layer, 0, 0), pipeline_mode=resident),
            pl.BlockSpec((None, d, dff), lambda i: (layer, 0, 0), pipeline_mode=resident),
            pl.BlockSpec((None, dff, d), lambda i: (layer, 0, 0), pipeline_mode=resident),
        ],
        out_specs=pl.BlockSpec((tm, d), lambda i: (i, 0)),
        scratch_shapes=[pltpu.VMEM((tm, dff), BF16)],
        compiler_params=_params("parallel"),
        name="ffn",
    )(x2d, gain, wg, wu, wd)


def _inproj_even_kernel(x_ref, g_ref, wz_ref, wx_ref, wdt_ref, wut_ref,
                        z_ref, xbc_ref, dt_ref, ut_ref):
    h = _rmsnorm_bf16(x_ref[...], g_ref[...])
    z_ref[...] = _dot(h, wz_ref[...])
    xbc_ref[...] = _dot(h, wx_ref[...])
    dt_ref[...] = _dot(h, wdt_ref[...])
    ut_ref[0] = lax.dot_general(wut_ref[...], h, (((1,), (1,)), ((), ())),
                                preferred_element_type=F32)


def _inproj_even(x2d, gain, wz, wx, wdt, wut, layer, bsz, seq):
    m, d = x2d.shape
    tm = min(PROJ_TM, seq)
    per_b = seq // tm
    const = lambda i: (0, 0)
    return pl.pallas_call(
        _inproj_even_kernel,
        out_shape=(
            jax.ShapeDtypeStruct((m, SSD_INNER), F32),
            jax.ShapeDtypeStruct((m, SSD_XBC), F32),
            jax.ShapeDtypeStruct((m, LANES), F32),
            jax.ShapeDtypeStruct((bsz, S5_WIDTH, seq), F32),
        ),
        grid=(m // tm,),
        in_specs=[
            pl.BlockSpec((tm, d), lambda i: (i, 0)),
            pl.BlockSpec((None, 1, d), lambda i: (layer, 0, 0)),
            pl.BlockSpec(wz.shape, const),
            pl.BlockSpec(wx.shape, const),
            pl.BlockSpec(wdt.shape, const),
            pl.BlockSpec(wut.shape, const),
        ],
        out_specs=(
            pl.BlockSpec((tm, SSD_INNER), lambda i: (i, 0)),
            pl.BlockSpec((tm, SSD_XBC), lambda i: (i, 0)),
            pl.BlockSpec((tm, LANES), lambda i: (i, 0)),
            pl.BlockSpec((1, S5_WIDTH, tm), lambda i: (i // per_b, 0, i % per_b)),
        ),
        compiler_params=_params("parallel"),
        name="inproj_even",
    )(x2d, gain, wz, wx, wdt, wut)


def _shift_rows(cur, prev8, j):
    rolled = pltpu.roll(cur, j, 0)
    prev_rolled = pltpu.roll(prev8, j, 0)
    row = lax.broadcasted_iota(jnp.int32, prev8.shape, 0)
    top = jnp.where(row < j, prev_rolled, rolled[:8])
    return jnp.concatenate([top, rolled[8:]], axis=0)


def _ssd_kernel(z_ref, xbc_ref, xprev_ref, dt_ref, cw_ref, cb_ref, dtb_ref, alog_ref,
                dx_ref, ng_ref, e_ref, o_ref, h_scr):
    t = pl.program_id(1)
    q = z_ref.shape[0]
    gw = SSD_INNER // SSD_GROUPS
    hpg = SSD_HEADS // SSD_GROUPS

    @pl.when(t == 0)
    def _():
        h_scr[...] = jnp.zeros_like(h_scr)

    xbc = xbc_ref[...]
    prev = jnp.where(t > 0, xprev_ref[...], 0.0)
    conv = xbc * cw_ref[SSD_CONV - 1:SSD_CONV, :] + cb_ref[...]
    for j in range(1, SSD_CONV):
        conv = conv + _shift_rows(xbc, prev, j) * cw_ref[SSD_CONV - 1 - j:SSD_CONV - j, :]
    xc = _silu(conv)
    xs = xc[:, :SSD_INNER]

    dtr = dt_ref[...] + dtb_ref[...]
    dt = jnp.maximum(dtr, 0.0) + jnp.log1p(jnp.exp(-jnp.abs(dtr)))
    a = -jnp.exp(alog_ref[...])
    row = lax.broadcasted_iota(jnp.int32, (q, q), 0)
    col = lax.broadcasted_iota(jnp.int32, (q, q), 1)
    causal = row >= col
    tril = causal.astype(F32)
    acs = _dot_exact(tril, dt * a)
    acs_t = acs.T
    e = e_ref[...]
    acs_x = _dot_exact(acs, e)
    dt_x = _dot_exact(dt, e)
    xdt = xs * dt_x
    decay_in = jnp.exp(acs_x)
    acs_end = acs_x[q - 1:q, :]
    to_end = jnp.exp(acs_end - acs_x)
    chunk_decay = jnp.exp(acs_end)
    xdt_end = xdt * to_end

    lane_head = lax.broadcasted_iota(jnp.int32, (1, gw), 1) // SSD_HEAD_DIM
    ys = []
    for g in range(SSD_GROUPS):
        bm = xc[:, SSD_INNER + g * SSD_STATE:SSD_INNER + (g + 1) * SSD_STATE].astype(BF16)
        cm = xc[:, SSD_INNER + (SSD_GROUPS + g) * SSD_STATE:
                SSD_INNER + (SSD_GROUPS + g + 1) * SSD_STATE].astype(BF16)
        cb = lax.dot_general(cm, bm, (((1,), (1,)), ((), ())), preferred_element_type=F32)
        xdt_g = xdt[:, g * gw:(g + 1) * gw]
        gmats, rhs = [], []
        for j in range(hpg):
            hh = g * hpg + j
            seg = acs[:, hh:hh + 1] - acs_t[hh:hh + 1, :]
            decay = jnp.exp(jnp.where(causal, seg, -jnp.inf))
            gmats.append((cb * decay).astype(BF16))
            rhs.append(jnp.where(lane_head == j, xdt_g, 0.0).astype(BF16))
        y_diag = _dot(jnp.concatenate(gmats, axis=1), jnp.concatenate(rhs, axis=0))
        h_prev = h_scr[g]
        y_off = _dot(cm, h_prev.astype(BF16)) * decay_in[:, g * gw:(g + 1) * gw]
        upd = lax.dot_general(bm, xdt_end[:, g * gw:(g + 1) * gw].astype(BF16),
                              (((0,), (0,)), ((), ())), preferred_element_type=F32)
        h_scr[g] = chunk_decay[:, g * gw:(g + 1) * gw] * h_prev + upd
        ys.append(y_diag + y_off)
    y = jnp.concatenate(ys, axis=1) + dx_ref[...] * xs
    y = y * _silu(z_ref[...])
    outs = []
    for g in range(SSD_GROUPS):
        yg = y[:, g * gw:(g + 1) * gw]
        outs.append(yg * lax.rsqrt(jnp.mean(yg * yg, axis=-1, keepdims=True) + EPS))
    o_ref[...] = jnp.concatenate(outs, axis=1) * ng_ref[...]


def _ssd(z, xbc, dt, conv_w, conv_b, dt_bias, a_log, d_x, norm_gain, expand, bsz, seq):
    m = z.shape[0]
    q = min(SSD_Q, seq)
    nq = seq // q
    const = lambda b, t: (0, 0)
    return pl.pallas_call(
        _ssd_kernel,
        out_shape=jax.ShapeDtypeStruct((m, SSD_INNER), F32),
        grid=(bsz, nq),
        in_specs=[
            pl.BlockSpec((q, SSD_INNER), lambda b, t: (b * nq + t, 0)),
            pl.BlockSpec((q, SSD_XBC), lambda b, t: (b * nq + t, 0)),
            pl.BlockSpec((8, SSD_XBC), lambda b, t: (jnp.maximum((b * nq + t) * (q // 8) - 1, 0), 0)),
            pl.BlockSpec((q, LANES), lambda b, t: (b * nq + t, 0)),
            pl.BlockSpec(conv_w.shape, const),
            pl.BlockSpec(conv_b.shape, const),
            pl.BlockSpec(dt_bias.shape, const),
            pl.BlockSpec(a_log.shape, const),
            pl.BlockSpec(d_x.shape, const),
            pl.BlockSpec(norm_gain.shape, const),
            pl.BlockSpec(expand.shape, const),
        ],
        out_specs=pl.BlockSpec((q, SSD_INNER), lambda b, t: (b * nq + t, 0)),
        scratch_shapes=[pltpu.VMEM((SSD_GROUPS, SSD_STATE, SSD_INNER // SSD_GROUPS), F32)],
        compiler_params=_params("parallel", "arbitrary"),
        name="ssd",
    )(z, xbc, xbc, dt, conv_w, conv_b, dt_bias, a_log, d_x, norm_gain, expand)


def _cpow(base_re, base_im, exponent, nbits):
    shape = jnp.broadcast_shapes(base_re.shape, exponent.shape)
    pr = jnp.ones(shape, F32)
    pi = jnp.zeros(shape, F32)
    sr, si = base_re, base_im
    for k in range(nbits):
        bit = ((exponent >> k) & 1) == 1
        nr = pr * sr - pi * si
        ni = pr * si + pi * sr
        pr = jnp.where(bit, nr, pr)
        pi = jnp.where(bit, ni, pi)
        if k + 1 < nbits:
            sr, si = sr * sr - si * si, 2.0 * sr * si
    return pr, pi


def _s5_discretize(ar, ai, log_dt):
    dt = jnp.exp(log_dt)
    mag = jnp.exp(dt * ar)
    lam_re = mag * jnp.cos(dt * ai)
    lam_im = mag * jnp.sin(dt * ai)
    den = ar * ar + ai * ai
    f_re = ((lam_re - 1.0) * ar + lam_im * ai) / den
    f_im = (lam_im * ar - (lam_re - 1.0) * ai) / den
    return lam_re, lam_im, f_re, f_im


def _s5_kernel(u_ref, acol_ref, arow_ref, ldt_ref, bd_ref, cd_re_ref, cd_im_ref,
               ct_re_ref, ct_im_ref, d_ref, y_ref, kmat_scr, t_scr, f_scr):
    bsz, _, gsz, nc, q = u_ref.shape
    m = bsz * nc
    p2 = 2 * S5_STATE
    nbits = q.bit_length()
    log_dt = ldt_ref[...]

    lane = lax.broadcasted_iota(jnp.int32, (1, p2), 1)
    first_half_l = lane < S5_STATE
    rowp = lax.broadcasted_iota(jnp.int32, (p2, 1), 0)
    first_half_r = rowp < S5_STATE

    ar_r = arow_ref[0:1, :]
    ai_r = arow_ref[1:2, :]
    lr, li, fr, fi = _s5_discretize(ar_r, ai_r, log_dt)
    sign_b = jnp.where(first_half_l, -1.0, 1.0)
    bd1 = bd_ref[...]
    bd2 = pltpu.roll(bd1, S5_STATE, 1) * sign_b
    bb_a = fr * bd1 + fi * bd2
    bb_b = fr * bd2 - fi * bd1

    srow = lax.broadcasted_iota(jnp.int32, (q, 1), 0)
    pr_re, pr_im = _cpow(lr, li, (q - 1) - srow, nbits)
    for i in range(gsz):
        es = pr_re * bb_a[i:i + 1, :] + pr_im * bb_b[i:i + 1, :]
        t_scr[i * q:(i + 1) * q, gsz * q:gsz * q + p2] = es.astype(BF16)

    ar_c = acol_ref[:, 0:1]
    ai_c = acol_ref[:, 1:2]
    lcr, lci, _, _ = _s5_discretize(ar_c, ai_c, log_dt)
    dlane = lax.broadcasted_iota(jnp.int32, (1, q), 1)
    pw_re, pw_im = _cpow(lcr, lci, dlane, nbits)
    p1_re, p1_im = _cpow(lcr, lci, dlane + 1, nbits)
    pow_stack = jnp.where(first_half_r, pw_re, pw_im)

    v1 = bb_a * jnp.where(first_half_l, 1.0, -1.0)
    v2 = -pltpu.roll(bb_a, S5_STATE, 1)
    cre = cd_re_ref[...]
    cim = cd_im_ref[...]
    coef = cre[:, None, :] * v1[None, :, :] + cim[:, None, :] * v2[None, :, :]
    kmat_scr[...] = _dot_exact(coef.reshape(gsz * gsz, p2), pow_stack)

    for o in range(gsz):
        c_r = ct_re_ref[:, o:o + 1]
        c_i = ct_im_ref[:, o:o + 1]
        f_o = jnp.where(first_half_r, c_r * p1_re - c_i * p1_im, -(c_r * p1_im + c_i * p1_re))
        f_scr[:, o * q:(o + 1) * q] = f_o.astype(BF16)

    trow = lax.broadcasted_iota(jnp.int32, (q, q), 0)
    tcol = lax.broadcasted_iota(jnp.int32, (q, q), 1)
    lower = tcol >= trow

    def build(i, carry):
        for o in range(gsz):
            k_row = kmat_scr[pl.ds(o * gsz + i, 1), :]
            blk = pltpu.roll(jnp.broadcast_to(k_row, (q, q)), 0, 1, stride=1, stride_axis=0)
            blk = jnp.where(lower, blk, 0.0).astype(BF16)
            t_scr[pl.ds(pl.multiple_of(i * q, q), q), o * q:(o + 1) * q] = blk
        return carry

    lax.fori_loop(0, gsz, build, 0)

    u_f32 = [u_ref[:, 0, i].reshape(m, q) for i in range(gsz)]
    u_cat = jnp.concatenate([v.astype(BF16) for v in u_f32], axis=1)
    acc = _dot(u_cat, t_scr[...])
    x = acc[:, gsz * q:]

    sq_re, sq_im = lr, li
    for _ in range(q.bit_length() - 1):
        sq_re, sq_im = sq_re * sq_re - sq_im * sq_im, 2.0 * sq_re * sq_im
    crow = lax.broadcasted_iota(jnp.int32, (m, 1), 0) % nc
    k = 1
    while k < nc:
        sh = jnp.where(crow >= k, pltpu.roll(x, k, 0), 0.0)
        x = x + sq_re * sh + (sq_im * sign_b) * pltpu.roll(sh, S5_STATE, 1)
        sq_re, sq_im = sq_re * sq_re - sq_im * sq_im, 2.0 * sq_re * sq_im
        k *= 2
    h_prev = jnp.where(crow >= 1, pltpu.roll(x, 1, 0), 0.0)
    y = acc[:, :gsz * q] + _dot(h_prev.astype(BF16), f_scr[...])
    for o in range(gsz):
        y_o = y[:, o * q:(o + 1) * q] + d_ref[o:o + 1, :] * u_f32[o]
        y_ref[:, 0, o] = y_o.reshape(bsz, nc, q)


def _s5(ut, acol, arow, ldt, bd, cd_re, cd_im, ct_re, ct_im, d_rows):
    bsz, width, seq = ut.shape
    q = min(S5_Q, seq)
    nc = seq // q
    u5 = ut.reshape(bsz, S5_GROUPS, S5_GROUP, nc, q)
    p2 = 2 * S5_STATE
    per_g = lambda g: (g, 0, 0)
    y5 = pl.pallas_call(
        _s5_kernel,
        out_shape=jax.ShapeDtypeStruct(u5.shape, F32),
        grid=(S5_GROUPS,),
        in_specs=[
            pl.BlockSpec((bsz, 1, S5_GROUP, nc, q), lambda g: (0, g, 0, 0, 0)),
            pl.BlockSpec((None, p2, 2), per_g),
            pl.BlockSpec((None, 2, p2), per_g),
            pl.BlockSpec((None, 1, 1), per_g),
            pl.BlockSpec((None, S5_GROUP, p2), per_g),
            pl.BlockSpec((None, S5_GROUP, p2), per_g),
            pl.BlockSpec((None, S5_GROUP, p2), per_g),
            pl.BlockSpec((None, p2, S5_GROUP), per_g),
            pl.BlockSpec((None, p2, S5_GROUP), per_g),
            pl.BlockSpec((None, S5_GROUP, q), per_g),
        ],
        out_specs=pl.BlockSpec((bsz, 1, S5_GROUP, nc, q), lambda g: (0, g, 0, 0, 0)),
        scratch_shapes=[
            pltpu.VMEM((S5_GROUP * S5_GROUP, q), F32),
            pltpu.VMEM((S5_GROUP * q, S5_GROUP * q + p2), BF16),
            pltpu.VMEM((p2, S5_GROUP * q), BF16),
        ],
        compiler_params=_params("parallel"),
        name="s5",
    )(u5, acol, arow, ldt, bd, cd_re, cd_im, ct_re, ct_im, d_rows)
    return y5.reshape(bsz, width, seq)


def _gelu_tanh(x):
    return 0.5 * x * (1.0 + jnp.tanh(0.7978845608028654 * (x + 0.044715 * (x * x * x))))


def _outproj_even_kernel(x_ref, ya_ref, ybt_ref, wglut_ref, bglu_ref, wa_ref, wb_ref, o_ref):
    ybt = ybt_ref[0]
    gate = _dot(wglut_ref[...], _gelu_tanh(ybt).astype(BF16)) + bglu_ref[...]
    yb = (ybt * jax.nn.sigmoid(gate)).T.astype(BF16)
    o_ref[...] = (x_ref[...] + _dot(ya_ref[...].astype(BF16), wa_ref[...])
                  + _dot(yb, wb_ref[...]))


def _outproj_even(x2d, ya, ybt, wglut, bglu, wa, wb, bsz, seq):
    m, d = x2d.shape
    tm = min(PROJ_TM, seq)
    per_b = seq // tm
    const = lambda i: (0, 0)
    return pl.pallas_call(
        _outproj_even_kernel,
        out_shape=jax.ShapeDtypeStruct((m, d), F32),
        grid=(m // tm,),
        in_specs=[
            pl.BlockSpec((tm, d), lambda i: (i, 0)),
            pl.BlockSpec((tm, SSD_INNER), lambda i: (i, 0)),
            pl.BlockSpec((1, S5_WIDTH, tm), lambda i: (i // per_b, 0, i % per_b)),
            pl.BlockSpec(wglut.shape, const),
            pl.BlockSpec(bglu.shape, const),
            pl.BlockSpec(wa.shape, const),
            pl.BlockSpec(wb.shape, const),
        ],
        out_specs=pl.BlockSpec((tm, d), lambda i: (i, 0)),
        compiler_params=_params("parallel"),
        name="outproj_even",
    )(x2d, ya, ybt, wglut, bglu, wa, wb)


def _head_rmsnorm(x, gain, bd):
    sq = x * x
    hi = sq.astype(BF16)
    lo = (sq - hi.astype(F32)).astype(BF16)
    ss = _dot(hi, bd) + _dot(lo, bd)
    return x * lax.rsqrt(ss * (1.0 / ATT_HEAD_DIM) + EPS) * gain


def _inproj_odd_kernel(x_ref, g_ref, wq_ref, wk_ref, wv_ref, wu_ref, qg_ref, kg_ref, bd_ref,
                       q_ref, k_ref, v_ref, u_ref):
    h = _rmsnorm_bf16(x_ref[...], g_ref[...])
    bd = bd_ref[...]
    q = _head_rmsnorm(_dot(h, wq_ref[...]), qg_ref[...], bd)
    q_ref[...] = q * (ATT_HEAD_DIM ** -0.5)
    k_ref[...] = _head_rmsnorm(_dot(h, wk_ref[...]), kg_ref[...], bd)
    v_ref[...] = _dot(h, wv_ref[...])
    u_ref[...] = _dot(h, wu_ref[...])


def _inproj_odd(x2d, gain, wq, wk, wv, wu, qg, kg, bd, layer):
    m, d = x2d.shape
    tm = min(PROJ_TM, m)
    const = lambda i: (0, 0)
    w = ATT_WIDTH
    out = jax.ShapeDtypeStruct((m, w), F32)
    tile = pl.BlockSpec((tm, w), lambda i: (i, 0))
    return pl.pallas_call(
        _inproj_odd_kernel,
        out_shape=(out, out, out, out),
        grid=(m // tm,),
        in_specs=[
            pl.BlockSpec((tm, d), lambda i: (i, 0)),
            pl.BlockSpec((None, 1, d), lambda i: (layer, 0, 0)),
            pl.BlockSpec(wq.shape, const),
            pl.BlockSpec(wk.shape, const),
            pl.BlockSpec(wv.shape, const),
            pl.BlockSpec(wu.shape, const),
            pl.BlockSpec(qg.shape, const),
            pl.BlockSpec(kg.shape, const),
            pl.BlockSpec(bd.shape, const),
        ],
        out_specs=(tile, tile, tile, tile),
        compiler_params=_params("parallel"),
        name="inproj_odd",
    )(x2d, gain, wq, wk, wv, wu, qg, kg, bd)


def _attn_kernel(q_ref, k0_ref, k1_ref, k2_ref, v0_ref, v1_ref, v2_ref, brow_ref, o_ref, bias_ref):
    t = pl.program_id(1)
    tq = q_ref.shape[0]

    @pl.when((pl.program_id(0) == 0) & (t == 0))
    def _():
        rowi = lax.broadcasted_iota(jnp.int32, (tq, 3 * tq), 0)
        coli = lax.broadcasted_iota(jnp.int32, (tq, 3 * tq), 1)
        rel_chunk = coli // ATT_CHUNK - (2 * tq // ATT_CHUNK - LEFT_CHUNKS) - rowi // ATT_CHUNK
        in_band = jnp.where(rel_chunk >= 0, rel_chunk, LEFT_CHUNKS + 1) <= LEFT_CHUNKS
        for hh in range(ATT_HEADS):
            base = jnp.broadcast_to(brow_ref[hh:hh + 1, :], (tq, 4 * tq))
            shifted = pltpu.roll(base, 0, 1, stride=1, stride_axis=0)
            bias_ref[hh] = jnp.where(in_band, shifted[:, tq:], -jnp.inf)

    q = q_ref[...]
    kcat = jnp.concatenate([k0_ref[...], k1_ref[...], k2_ref[...]], axis=0).astype(BF16)
    vcat = jnp.concatenate([v0_ref[...], v1_ref[...], v2_ref[...]], axis=0)
    colk = lax.broadcasted_iota(jnp.int32, (1, 3 * tq), 1)
    valid = colk >= (2 - t) * tq
    lane = lax.broadcasted_iota(jnp.int32, (1, LANES), 1)
    heads_per_slab = LANES // ATT_HEAD_DIM
    for p in range(ATT_WIDTH // LANES):
        sl = slice(p * LANES, (p + 1) * LANES)
        qp, kp, vp = q[:, sl], kcat[:, sl], vcat[:, sl]
        out = jnp.zeros((tq, LANES), F32)
        for hh in range(heads_per_slab):
            in_head = (lane // ATT_HEAD_DIM) == hh
            qh = jnp.where(in_head, qp, 0.0).astype(BF16)
            s = lax.dot_general(qh, kp, (((1,), (1,)), ((), ())), preferred_element_type=F32)
            s = jnp.where(valid, s + bias_ref[p * heads_per_slab + hh], -jnp.inf)
            e = jnp.exp(s - jnp.max(s, axis=-1, keepdims=True))
            den = jnp.sum(e, axis=-1, keepdims=True)
            vh = jnp.where(in_head, vp, 0.0).astype(BF16)
            out = out + _dot(e.astype(BF16), vh) / den
        o_ref[:, sl] = out


def _attention(q, k, v, brow, bsz, seq):
    m, w = q.shape
    tq = ATT_T
    assert 2 * tq >= LEFT_CHUNKS * ATT_CHUNK and tq > MAX_REL and seq % tq == 0
    nt = seq // tq
    cur = lambda b, t: (b * nt + t, 0)
    back1 = lambda b, t: (b * nt + jnp.maximum(t - 1, 0), 0)
    back2 = lambda b, t: (b * nt + jnp.maximum(t - 2, 0), 0)
    blk = lambda f: pl.BlockSpec((tq, w), f)
    return pl.pallas_call(
        _attn_kernel,
        out_shape=jax.ShapeDtypeStruct((m, w), F32),
        grid=(bsz, nt),
        in_specs=[blk(cur), blk(back2), blk(back1), blk(cur), blk(back2), blk(back1), blk(cur),
                  pl.BlockSpec(brow.shape, lambda b, t: (0, 0))],
        out_specs=blk(cur),
        scratch_shapes=[pltpu.VMEM((ATT_HEADS, tq, 3 * tq), F32)],
        compiler_params=_params("arbitrary", "arbitrary"),
        name="band_attention",
    )(q, k, k, k, v, v, v, brow)


def _bias_rows(rel_bias, tq):
    rb = rel_bias.astype(F32)
    nh = rb.shape[0]
    far = jnp.broadcast_to(rb[:, -1:], (nh, 3 * tq - MAX_REL))
    near = jnp.broadcast_to(rb[:, :1], (nh, tq - MAX_REL - 1))
    return jnp.concatenate([far, rb[:, ::-1], near], axis=1)


def _pool_kernel(u_ref, uprev_ref, w_ref, sc_ref, o_ref):
    t = pl.program_id(1)
    tp = u_ref.shape[0]
    u = u_ref[...]
    prev = jnp.where(t > 0, uprev_ref[...], 0.0)
    ext = jnp.concatenate([prev, u], axis=0)
    sums = {}
    s = ext
    w = 1
    while w < POOL_MAX:
        s = s + pltpu.roll(s, w, 0)
        w *= 2
        sums[w] = s[POOL_MAX:]
    pos = t * tp + lax.broadcasted_iota(jnp.int32, (tp, 1), 0)
    outs = []
    for g, win in enumerate(POOL_WINDOWS):
        sl = slice(g * POOL_GROUP, (g + 1) * POOL_GROUP)
        count = jnp.minimum(pos + 1, win).astype(F32)
        pooled = sums[win][:, sl] / count - u[:, sl]
        outs.append(_dot(pooled.astype(BF16), w_ref[g]))
    o_ref[...] = jnp.concatenate(outs, axis=1) * sc_ref[...]


def _pool(u, w_pool, scale, bsz, seq):
    m, w = u.shape
    tp = min(POOL_T, seq)
    nt = seq // tp
    return pl.pallas_call(
        _pool_kernel,
        out_shape=jax.ShapeDtypeStruct((m, w), F32),
        grid=(bsz, nt),
        in_specs=[
            pl.BlockSpec((tp, w), lambda b, t: (b * nt + t, 0)),
            pl.BlockSpec((POOL_MAX, w),
                         lambda b, t: (jnp.maximum((b * nt + t) * (tp // POOL_MAX) - 1, 0), 0)),
            pl.BlockSpec(w_pool.shape, lambda b, t: (0, 0, 0)),
            pl.BlockSpec(scale.shape, lambda b, t: (0, 0)),
        ],
        out_specs=pl.BlockSpec((tp, w), lambda b, t: (b * nt + t, 0)),
        compiler_params=_params("parallel", "arbitrary"),
        name="pool",
    )(u, u, w_pool, scale)


def _outproj_odd_kernel(x_ref, yc_ref, yd_ref, wa_ref, wb_ref, o_ref):
    o_ref[...] = (x_ref[...] + _dot(yc_ref[...].astype(BF16), wa_ref[...])
                  + _dot(yd_ref[...].astype(BF16), wb_ref[...]))


def _outproj_odd(x2d, yc, yd, wa, wb):
    m, d = x2d.shape
    tm = min(PROJ_TM, m)
    const = lambda i: (0, 0)
    return pl.pallas_call(
        _outproj_odd_kernel,
        out_shape=jax.ShapeDtypeStruct((m, d), F32),
        grid=(m // tm,),
        in_specs=[
            pl.BlockSpec((tm, d), lambda i: (i, 0)),
            pl.BlockSpec((tm, ATT_WIDTH), lambda i: (i, 0)),
            pl.BlockSpec((tm, ATT_WIDTH), lambda i: (i, 0)),
            pl.BlockSpec(wa.shape, const),
            pl.BlockSpec(wb.shape, const),
        ],
        out_specs=pl.BlockSpec((tm, d), lambda i: (i, 0)),
        compiler_params=_params("parallel"),
        name="outproj_odd",
    )(x2d, yc, yd, wa, wb)


def _even_mixer(x2d, gain, i, bsz, seq, even_w_in, even_w_out, ssd_conv_w, ssd_conv_b, ssd_dt_bias,
                ssd_a_log, ssd_d, ssd_norm, s5_a_re, s5_a_im, s5_log_dt, s5_b_re, s5_b_im,
                s5_c_re, s5_c_im, s5_d, s5_w_glu, s5_b_glu):
    w_in = even_w_in[i]
    o1 = SSD_INNER
    o2 = o1 + SSD_XBC
    o3 = o2 + SSD_HEADS
    wz = w_in[:, :o1].astype(BF16)
    wx = w_in[:, o1:o2].astype(BF16)
    wdt = jnp.pad(w_in[:, o2:o3], ((0, 0), (0, LANES - SSD_HEADS))).astype(BF16)
    wut = w_in[:, o3:].T.astype(BF16)
    z, xbc, dt, ut = _inproj_even(x2d, gain, wz, wx, wdt, wut, i * 2, bsz, seq)

    pad_h = lambda v: jnp.pad(v.astype(F32), (0, LANES - SSD_HEADS))[None, :]
    expand = (jnp.arange(LANES)[:, None] == (jnp.arange(SSD_INNER)[None, :] // SSD_HEAD_DIM)).astype(F32)
    ya = _ssd(z, xbc, dt, ssd_conv_w[i], ssd_conv_b[i][None, :], pad_h(ssd_dt_bias[i]),
              pad_h(ssd_a_log[i]), jnp.repeat(ssd_d[i], SSD_HEAD_DIM)[None, :], ssd_norm[i][None, :],
              expand, bsz, seq)

    q = min(S5_Q, seq)
    are, aim = s5_a_re[i], s5_a_im[i]
    dup = lambda v: jnp.concatenate([v, v], axis=-1)
    arow = jnp.stack([dup(are), dup(aim)], axis=1)
    acol = jnp.swapaxes(arow, 1, 2)
    ldt = s5_log_dt[i][:, None, None]
    bd = jnp.concatenate([jnp.swapaxes(s5_b_re[i], 1, 2), jnp.swapaxes(s5_b_im[i], 1, 2)], axis=-1)
    cd_re, cd_im = dup(s5_c_re[i]), dup(s5_c_im[i])
    ct_re, ct_im = jnp.swapaxes(cd_re, 1, 2), jnp.swapaxes(cd_im, 1, 2)
    d_rows = jnp.broadcast_to(s5_d[i][:, :, None], (S5_GROUPS, S5_GROUP, q))
    ybt = _s5(ut, acol, arow, ldt, bd, cd_re, cd_im, ct_re, ct_im, d_rows)

    w_out = even_w_out[i]
    return _outproj_even(x2d, ya, ybt, s5_w_glu[i].T.astype(BF16), s5_b_glu[i][:, None],
                         w_out[:SSD_INNER].astype(BF16), w_out[SSD_INNER:].astype(BF16), bsz, seq)


def _odd_mixer(x2d, gain, i, bsz, seq, odd_w_in, odd_w_out, attn_q_norm, attn_k_norm, attn_rel_bias,
               pool_w, pool_scale):
    w_in = odd_w_in[i].astype(BF16)
    w = ATT_WIDTH
    tile_gain = lambda v: jnp.tile(v.astype(F32), ATT_HEADS)[None, :]
    head = jnp.arange(w) // ATT_HEAD_DIM
    bd = (head[:, None] == head[None, :]).astype(BF16)
    q, k, v, u = _inproj_odd(x2d, gain, w_in[:, :w], w_in[:, w:2 * w], w_in[:, 2 * w:3 * w], w_in[:, 3 * w:],
                             tile_gain(attn_q_norm[i]), tile_gain(attn_k_norm[i]), bd, i * 2 + 1)
    yc = _attention(q, k, v, _bias_rows(attn_rel_bias[i], ATT_T), bsz, seq)
    yd = _pool(u, pool_w[i].astype(BF16), pool_scale[i][None, :], bsz, seq)
    w_out = odd_w_out[i]
    return _outproj_odd(x2d, yc, yd, w_out[:w].astype(BF16), w_out[w:].astype(BF16))


def kernel(x, ffn1_norm, ffn1_w_gate, ffn1_w_up, ffn1_w_down, mix_norm, even_w_in, even_w_out, ssd_conv_w, ssd_conv_b, ssd_dt_bias, ssd_a_log, ssd_d, ssd_norm, s5_a_re, s5_a_im, s5_log_dt, s5_b_re, s5_b_im, s5_c_re, s5_c_im, s5_d, s5_w_glu, s5_b_glu, odd_w_in, odd_w_out, attn_q_norm, attn_k_norm, attn_rel_bias, pool_w, pool_scale, ffn2_norm, ffn2_w_gate, ffn2_w_up, ffn2_w_down):
    bsz, seq, d = x.shape
    depth = ffn1_norm.shape[0]
    x2d = x.reshape(bsz * seq, d)
    g3 = lambda v: v.astype(F32)[:, None, :]
    f1n, mxn, f2n = g3(ffn1_norm), g3(mix_norm), g3(ffn2_norm)
    f1g, f1u, f1d = ffn1_w_gate.astype(BF16), ffn1_w_up.astype(BF16), ffn1_w_down.astype(BF16)
    f2g, f2u, f2d = ffn2_w_gate.astype(BF16), ffn2_w_up.astype(BF16), ffn2_w_down.astype(BF16)
    for layer in range(depth):
        i = layer // 2
        x2d = _ffn(x2d, f1n, f1g, f1u, f1d, layer)
        if layer % 2 == 0:
            x2d = _even_mixer(x2d, mxn, i, bsz, seq, even_w_in, even_w_out, ssd_conv_w, ssd_conv_b,
                              ssd_dt_bias, ssd_a_log, ssd_d, ssd_norm, s5_a_re, s5_a_im, s5_log_dt,
                              s5_b_re, s5_b_im, s5_c_re, s5_c_im, s5_d, s5_w_glu, s5_b_glu)
        else:
            x2d = _odd_mixer(x2d, mxn, i, bsz, seq, odd_w_in, odd_w_out, attn_q_norm, attn_k_norm,
                             attn_rel_bias, pool_w, pool_scale)
        x2d = _ffn(x2d, f2n, f2g, f2u, f2d, layer)
    return x2d.reshape(bsz, seq, d)
```

```python
import functools

import jax
import jax.numpy as jnp
from jax import lax
from jax.experimental import pallas as pl
from jax.experimental.pallas import tpu as pltpu

F32 = jnp.float32
BF16 = jnp.bfloat16
HIGHEST = lax.Precision.HIGHEST

EPS = 1e-6
LOG2E = 1.4426950408889634
LANES = 128
VMEM_LIMIT = 56 * 1024 * 1024

SSD_HEADS = 8
SSD_HEAD_DIM = 64
SSD_INNER = 512
SSD_GROUPS = 2
SSD_STATE = 128
SSD_CONV = 4
SSD_XBC = 1024
S5_WIDTH = 512
S5_GROUP = 16
S5_GROUPS = 32
S5_STATE = 64
ATT_HEADS = 8
ATT_HEAD_DIM = 64
ATT_WIDTH = 512
ATT_CHUNK = 64
LEFT_CHUNKS = 8
MAX_REL = 128
POOL_WINDOWS = (2, 4, 8, 16)
POOL_GROUP = 128
POOL_MAX = 16

FFN_TM = 512
FFN_TF = 256
PROJ_TM = 512
SSD_Q = 128
S5_Q = 128
ATT_T = 256
POOL_T = 512


def _params(*sem):
    return pltpu.CompilerParams(dimension_semantics=sem, vmem_limit_bytes=VMEM_LIMIT)


def _dot(a, b):
    return jnp.dot(a, b, preferred_element_type=F32)


def _dot_exact(a, b):
    return jnp.dot(a, b, preferred_element_type=F32, precision=HIGHEST)


def _split3(x):
    hi = x.astype(BF16)
    rest = x - hi.astype(F32)
    mid = rest.astype(BF16)
    lo = (rest - mid.astype(F32)).astype(BF16)
    return hi, mid, lo


def _select_left(sel, x):
    hi, mid, lo = _split3(x)
    return _dot(sel, hi) + _dot(sel, mid) + _dot(sel, lo)


def _select_right(x, sel):
    hi, mid, lo = _split3(x)
    return _dot(hi, sel) + _dot(mid, sel) + _dot(lo, sel)


def _rmsnorm_bf16(x, gain):
    ms = jnp.mean(x * x, axis=-1, keepdims=True)
    return (x * lax.rsqrt(ms + EPS) * gain).astype(BF16)


def _silu(x):
    return x * jax.nn.sigmoid(x)


def _ffn_kernel(x_ref, g_ref, wg_ref, wu_ref, wd_ref, o_ref, act_scr):
    x = x_ref[...]
    h = _rmsnorm_bf16(x, g_ref[...])
    dff = wg_ref.shape[1]
    for c in range(0, dff, FFN_TF):
        gate = _dot(h, wg_ref[:, c:c + FFN_TF])
        up = _dot(h, wu_ref[:, c:c + FFN_TF])
        act_scr[:, c:c + FFN_TF] = (_silu(gate) * up).astype(BF16)
    o_ref[...] = x + 0.5 * _dot(act_scr[...], wd_ref[...])


def _ffn(x2d, gain, wg, wu, wd, layer):
    m, d = x2d.shape
    dff = wg.shape[-1]
    tm = min(FFN_TM, m)
    assert dff % FFN_TF == 0 and m % tm == 0
    resident = pl.Buffered(1)
    return pl.pallas_call(
        _ffn_kernel,
        out_shape=jax.ShapeDtypeStruct((m, d), F32),
        grid=(m // tm,),
        in_specs=[
            pl.BlockSpec((tm, d), lambda i: (i, 0)),
            pl.BlockSpec((None, 1, d), lambda i: (layer, 0, 0)),
            pl.BlockSpec((None, d, dff), lambda i: (layer, 0, 0), pipeline_mode=resident),
            pl.BlockSpec((None, d, dff), lambda i: (layer, 0, 0), pipeline_mode=resident),
            pl.BlockSpec((None, dff, d), lambda i: (layer, 0, 0), pipeline_mode=resident),
        ],
        out_specs=pl.BlockSpec((tm, d), lambda i: (i, 0)),
        scratch_shapes=[pltpu.VMEM((tm, dff), BF16)],
        compiler_params=_params("parallel"),
        name="ffn",
    )(x2d, gain, wg, wu, wd)


def _inproj_even_kernel(x_ref, g_ref, wz_ref, wx_ref, wdt_ref, wut_ref,
                        z_ref, xbc_ref, dt_ref, ut_ref):
    h = _rmsnorm_bf16(x_ref[...], g_ref[...])
    z_ref[...] = _dot(h, wz_ref[...])
    xbc_ref[...] = _dot(h, wx_ref[...])
    dt_ref[...] = _dot(h, wdt_ref[...])
    ut = lax.dot_general(wut_ref[...], h, (((1,), (1,)), ((), ())), preferred_element_type=F32)
    q = ut_ref.shape[-1]
    for cc in range(ut_ref.shape[2]):
        ut_ref[0, :, cc] = ut[:, cc * q:(cc + 1) * q].reshape(S5_GROUPS, S5_GROUP, q)


def _inproj_even(x2d, gain, wz, wx, wdt, wut, layer, bsz, seq):
    m, d = x2d.shape
    tm = min(PROJ_TM, seq)
    per_b = seq // tm
    q = min(S5_Q, seq)
    const = lambda i: (0, 0)
    return pl.pallas_call(
        _inproj_even_kernel,
        out_shape=(
            jax.ShapeDtypeStruct((m, SSD_INNER), F32),
            jax.ShapeDtypeStruct((m, SSD_XBC), F32),
            jax.ShapeDtypeStruct((m, LANES), F32),
            jax.ShapeDtypeStruct((bsz, S5_GROUPS, seq // q, S5_GROUP, q), F32),
        ),
        grid=(m // tm,),
        in_specs=[
            pl.BlockSpec((tm, d), lambda i: (i, 0)),
            pl.BlockSpec((None, 1, d), lambda i: (layer, 0, 0)),
            pl.BlockSpec(wz.shape, const),
            pl.BlockSpec(wx.shape, const),
            pl.BlockSpec(wdt.shape, const),
            pl.BlockSpec(wut.shape, const),
        ],
        out_specs=(
            pl.BlockSpec((tm, SSD_INNER), lambda i: (i, 0)),
            pl.BlockSpec((tm, SSD_XBC), lambda i: (i, 0)),
            pl.BlockSpec((tm, LANES), lambda i: (i, 0)),
            pl.BlockSpec((1, S5_GROUPS, tm // q, S5_GROUP, q),
                         lambda i: (i // per_b, 0, i % per_b, 0, 0)),
        ),
        compiler_params=_params("parallel"),
        name="inproj_even",
    )(x2d, gain, wz, wx, wdt, wut)


def _shift_rows(cur, prev8, j):
    rolled = pltpu.roll(cur, j, 0)
    prev_rolled = pltpu.roll(prev8, j, 0)
    row = lax.broadcasted_iota(jnp.int32, prev8.shape, 0)
    top = jnp.where(row < j, prev_rolled, rolled[:8])
    return jnp.concatenate([top, rolled[8:]], axis=0)


def _ssd_kernel(z_ref, xbc_ref, xprev_ref, dt_ref, cw_ref, cb_ref, dtb_ref, alog_ref,
                dx_ref, ng_ref, e_ref, o_ref, h_scr):
    t = pl.program_id(1)
    q = z_ref.shape[0]
    gw = SSD_INNER // SSD_GROUPS
    hpg = SSD_HEADS // SSD_GROUPS

    @pl.when(t == 0)
    def _():
        h_scr[...] = jnp.zeros_like(h_scr)

    xbc = xbc_ref[...]
    prev = jnp.where(t > 0, xprev_ref[...], 0.0)
    conv = xbc * cw_ref[SSD_CONV - 1:SSD_CONV, :] + cb_ref[...]
    for j in range(1, SSD_CONV):
        conv = conv + _shift_rows(xbc, prev, j) * cw_ref[SSD_CONV - 1 - j:SSD_CONV - j, :]
    xc = _silu(conv)
    xs = xc[:, :SSD_INNER]

    dtr = dt_ref[...] + dtb_ref[...]
    dt = jnp.maximum(dtr, 0.0) + jnp.log1p(jnp.exp(-jnp.abs(dtr)))
    a = -jnp.exp(alog_ref[...])
    row = lax.broadcasted_iota(jnp.int32, (q, q), 0)
    col = lax.broadcasted_iota(jnp.int32, (q, q), 1)
    causal = row >= col
    tril = jnp.where(causal, 1.0, 0.0).astype(BF16)
    acs = _select_left(tril, dt * a)
    acs_t = acs.T
    expanded = _select_right(jnp.concatenate([acs, dt], axis=0), e_ref[...])
    acs_x = expanded[:q]
    dt_x = expanded[q:]
    xdt = xs * dt_x
    decay_in = jnp.exp(acs_x)
    acs_end = acs_x[q - 1:q, :]
    to_end = jnp.exp(acs_end - acs_x)
    chunk_decay = jnp.exp(acs_end)
    xdt_end = xdt * to_end

    lane_head = lax.broadcasted_iota(jnp.int32, (1, gw), 1) // SSD_HEAD_DIM
    ys = []
    for g in range(SSD_GROUPS):
        bm = xc[:, SSD_INNER + g * SSD_STATE:SSD_INNER + (g + 1) * SSD_STATE].astype(BF16)
        cm = xc[:, SSD_INNER + (SSD_GROUPS + g) * SSD_STATE:
                SSD_INNER + (SSD_GROUPS + g + 1) * SSD_STATE].astype(BF16)
        cb = lax.dot_general(cm, bm, (((1,), (1,)), ((), ())), preferred_element_type=F32)
        xdt_g = xdt[:, g * gw:(g + 1) * gw]
        gmats, rhs = [], []
        for j in range(hpg):
            hh = g * hpg + j
            seg = acs[:, hh:hh + 1] - acs_t[hh:hh + 1, :]
            decay = jnp.exp(jnp.where(causal, seg, -jnp.inf))
            gmats.append((cb * decay).astype(BF16))
            rhs.append(jnp.where(lane_head == j, xdt_g, 0.0).astype(BF16))
        y_diag = _dot(jnp.concatenate(gmats, axis=1), jnp.concatenate(rhs, axis=0))
        h_prev = h_scr[g]
        y_off = _dot(cm, h_prev.astype(BF16)) * decay_in[:, g * gw:(g + 1) * gw]
        upd = lax.dot_general(bm, xdt_end[:, g * gw:(g + 1) * gw].astype(BF16),
                              (((0,), (0,)), ((), ())), preferred_element_type=F32)
        h_scr[g] = chunk_decay[:, g * gw:(g + 1) * gw] * h_prev + upd
        ys.append(y_diag + y_off)
    y = jnp.concatenate(ys, axis=1) + dx_ref[...] * xs
    y = y * _silu(z_ref[...])
    outs = []
    for g in range(SSD_GROUPS):
        yg = y[:, g * gw:(g + 1) * gw]
        outs.append(yg * lax.rsqrt(jnp.mean(yg * yg, axis=-1, keepdims=True) + EPS))
    o_ref[...] = (jnp.concatenate(outs, axis=1) * ng_ref[...]).astype(o_ref.dtype)


def _ssd(z, xbc, dt, conv_w, conv_b, dt_bias, a_log, d_x, norm_gain, expand, bsz, seq):
    m = z.shape[0]
    q = min(SSD_Q, seq)
    nq = seq // q
    const = lambda b, t: (0, 0)
    return pl.pallas_call(
        _ssd_kernel,
        out_shape=jax.ShapeDtypeStruct((m, SSD_INNER), BF16),
        grid=(bsz, nq),
        in_specs=[
            pl.BlockSpec((q, SSD_INNER), lambda b, t: (b * nq + t, 0)),
            pl.BlockSpec((q, SSD_XBC), lambda b, t: (b * nq + t, 0)),
            pl.BlockSpec((8, SSD_XBC), lambda b, t: (jnp.maximum((b * nq + t) * (q // 8) - 1, 0), 0)),
            pl.BlockSpec((q, LANES), lambda b, t: (b * nq + t, 0)),
            pl.BlockSpec(conv_w.shape, const),
            pl.BlockSpec(conv_b.shape, const),
            pl.BlockSpec(dt_bias.shape, const),
            pl.BlockSpec(a_log.shape, const),
            pl.BlockSpec(d_x.shape, const),
            pl.BlockSpec(norm_gain.shape, const),
            pl.BlockSpec(expand.shape, const),
        ],
        out_specs=pl.BlockSpec((q, SSD_INNER), lambda b, t: (b * nq + t, 0)),
        scratch_shapes=[pltpu.VMEM((SSD_GROUPS, SSD_STATE, SSD_INNER // SSD_GROUPS), F32)],
        compiler_params=_params("parallel", "arbitrary"),
        name="ssd",
    )(z, xbc, xbc, dt, conv_w, conv_b, dt_bias, a_log, d_x, norm_gain, expand)


def _cpow(base_re, base_im, exponent, nbits):
    shape = jnp.broadcast_shapes(base_re.shape, exponent.shape)
    pr = jnp.ones(shape, F32)
    pi = jnp.zeros(shape, F32)
    sr, si = base_re, base_im
    for k in range(nbits):
        bit = ((exponent >> k) & 1) == 1
        nr = pr * sr - pi * si
        ni = pr * si + pi * sr
        pr = jnp.where(bit, nr, pr)
        pi = jnp.where(bit, ni, pi)
        if k + 1 < nbits:
            sr, si = sr * sr - si * si, 2.0 * sr * si
    return pr, pi


def _s5_discretize(ar, ai, log_dt):
    dt = jnp.exp(log_dt)
    mag = jnp.exp(dt * ar)
    lam_re = mag * jnp.cos(dt * ai)
    lam_im = mag * jnp.sin(dt * ai)
    den = ar * ar + ai * ai
    f_re = ((lam_re - 1.0) * ar + lam_im * ai) / den
    f_im = (lam_im * ar - (lam_re - 1.0) * ai) / den
    return lam_re, lam_im, f_re, f_im


def _s5_kernel(u_ref, arow_ref, ldt_ref, bd_ref, cd_re_ref, cd_im_ref,
               ct_re_ref, ct_im_ref, d_ref, y_ref, kmat_scr, t_scr, f_scr):
    bsz, rows, q = u_ref.shape
    gsz = S5_GROUP
    nc = rows // gsz
    m = bsz * nc
    p2 = 2 * S5_STATE
    nbits = (q - 1).bit_length()
    log_dt = ldt_ref[...]

    lane = lax.broadcasted_iota(jnp.int32, (1, p2), 1)
    first_half_l = lane < S5_STATE
    rowp = lax.broadcasted_iota(jnp.int32, (p2, 1), 0)
    first_half_r = rowp < S5_STATE

    ar_r = arow_ref[0:1, :]
    ai_r = arow_ref[1:2, :]
    lr, li, fr, fi = _s5_discretize(ar_r, ai_r, log_dt)
    sign_b = jnp.where(first_half_l, -1.0, 1.0)
    bd1 = bd_ref[...]
    bd2 = pltpu.roll(bd1, S5_STATE, 1) * sign_b
    bb_a = fr * bd1 + fi * bd2
    bb_b = fr * bd2 - fi * bd1

    srow = lax.broadcasted_iota(jnp.int32, (q, 1), 0)
    pr_re, pr_im = _cpow(lr, li, (q - 1) - srow, nbits)
    for i in range(gsz):
        es = pr_re * bb_a[i:i + 1, :] + pr_im * bb_b[i:i + 1, :]
        t_scr[i * q:(i + 1) * q, gsz * q:gsz * q + p2] = es.astype(BF16)

    eye = (lax.broadcasted_iota(jnp.int32, (p2, p2), 0)
           == lax.broadcasted_iota(jnp.int32, (p2, p2), 1))
    lcr = jnp.sum(jnp.where(eye, lr, 0.0), axis=1, keepdims=True)
    lci = jnp.sum(jnp.where(eye, li, 0.0), axis=1, keepdims=True)
    dlane = lax.broadcasted_iota(jnp.int32, (1, q), 1)
    pw_re, pw_im = _cpow(lcr, lci, dlane, nbits)
    p1_re = pw_re * lcr - pw_im * lci
    p1_im = pw_re * lci + pw_im * lcr
    pow_stack = jnp.where(first_half_r, pw_re, pw_im)

    v1 = bb_a * jnp.where(first_half_l, 1.0, -1.0)
    v2 = -pltpu.roll(bb_a, S5_STATE, 1)
    cre = cd_re_ref[...]
    cim = cd_im_ref[...]
    coef = cre[:, None, :] * v1[None, :, :] + cim[:, None, :] * v2[None, :, :]
    kmat_scr[...] = _dot_exact(coef.reshape(gsz * gsz, p2), pow_stack)

    for o in range(gsz):
        c_r = ct_re_ref[:, o:o + 1]
        c_i = ct_im_ref[:, o:o + 1]
        f_o = jnp.where(first_half_r, c_r * p1_re - c_i * p1_im, -(c_r * p1_im + c_i * p1_re))
        f_scr[:, o * q:(o + 1) * q] = f_o.astype(BF16)

    trow = lax.broadcasted_iota(jnp.int32, (q, q), 0)
    tcol = lax.broadcasted_iota(jnp.int32, (q, q), 1)
    lower = tcol >= trow

    def build(i, carry):
        for o in range(gsz):
            k_row = kmat_scr[pl.ds(o * gsz + i, 1), :]
            blk = pltpu.roll(jnp.broadcast_to(k_row, (q, q)), 0, 1, stride=1, stride_axis=0)
            blk = jnp.where(lower, blk, 0.0).astype(BF16)
            t_scr[pl.ds(pl.multiple_of(i * q, q), q), o * q:(o + 1) * q] = blk
        return carry

    lax.fori_loop(0, gsz, build, 0)

    u_f32 = [jnp.concatenate([u_ref[b, pl.ds(i, nc, stride=gsz), :] for b in range(bsz)], axis=0)
             for i in range(gsz)]
    u_cat = jnp.concatenate([v.astype(BF16) for v in u_f32], axis=1)
    acc = _dot(u_cat, t_scr[...])
    x = acc[:, gsz * q:]

    sq_re, sq_im = lr, li
    for _ in range(q.bit_length() - 1):
        sq_re, sq_im = sq_re * sq_re - sq_im * sq_im, 2.0 * sq_re * sq_im
    crow = lax.broadcasted_iota(jnp.int32, (m, 1), 0) % nc
    k = 1
    while k < nc:
        sh = jnp.where(crow >= k, pltpu.roll(x, k, 0), 0.0)
        x = x + sq_re * sh + (sq_im * sign_b) * pltpu.roll(sh, S5_STATE, 1)
        sq_re, sq_im = sq_re * sq_re - sq_im * sq_im, 2.0 * sq_re * sq_im
        k *= 2
    h_prev = jnp.where(crow >= 1, pltpu.roll(x, 1, 0), 0.0)
    y = acc[:, :gsz * q] + _dot(h_prev.astype(BF16), f_scr[...])
    for o in range(gsz):
        y_o = y[:, o * q:(o + 1) * q] + d_ref[o:o + 1, :] * u_f32[o]
        for b in range(bsz):
            y_ref[b, pl.ds(o, nc, stride=gsz), :] = y_o[b * nc:(b + 1) * nc]


def _s5(u5, arow, ldt, bd, cd_re, cd_im, ct_re, ct_im, d_rows):
    bsz, _, nc, _, q = u5.shape
    rows = nc * S5_GROUP
    p2 = 2 * S5_STATE
    per_g = lambda g: (g, 0, 0)
    y4 = pl.pallas_call(
        _s5_kernel,
        out_shape=jax.ShapeDtypeStruct((bsz, S5_GROUPS, rows, q), F32),
        grid=(S5_GROUPS,),
        in_specs=[
            pl.BlockSpec((bsz, None, rows, q), lambda g: (0, g, 0, 0)),
            pl.BlockSpec((None, 2, p2), per_g),
            pl.BlockSpec((None, 1, 1), per_g),
            pl.BlockSpec((None, S5_GROUP, p2), per_g),
            pl.BlockSpec((None, S5_GROUP, p2), per_g),
            pl.BlockSpec((None, S5_GROUP, p2), per_g),
            pl.BlockSpec((None, p2, S5_GROUP), per_g),
            pl.BlockSpec((None, p2, S5_GROUP), per_g),
            pl.BlockSpec((None, S5_GROUP, q), per_g),
        ],
        out_specs=pl.BlockSpec((bsz, None, rows, q), lambda g: (0, g, 0, 0)),
        scratch_shapes=[
            pltpu.VMEM((S5_GROUP * S5_GROUP, q), F32),
            pltpu.VMEM((S5_GROUP * q, S5_GROUP * q + p2), BF16),
            pltpu.VMEM((p2, S5_GROUP * q), BF16),
        ],
        compiler_params=_params("parallel"),
        name="s5",
    )(u5.reshape(bsz, S5_GROUPS, rows, q), arow, ldt, bd, cd_re, cd_im, ct_re, ct_im, d_rows)
    return y4.reshape(u5.shape)


def _gelu_tanh(x):
    return 0.5 * x * (1.0 + jnp.tanh(0.7978845608028654 * (x + 0.044715 * (x * x * x))))


def _outproj_even_kernel(x_ref, ya_ref, ybt_ref, wglut_ref, bglu_ref, wa_ref, wb_ref, o_ref):
    q = ybt_ref.shape[-1]
    ybt = jnp.concatenate([ybt_ref[0, :, cc].reshape(S5_WIDTH, q) for cc in range(ybt_ref.shape[2])],
                          axis=1)
    gate = _dot(wglut_ref[...], _gelu_tanh(ybt).astype(BF16)) + bglu_ref[...]
    yb = (ybt * jax.nn.sigmoid(gate)).T.astype(BF16)
    o_ref[...] = (x_ref[...] + _dot(ya_ref[...], wa_ref[...]) + _dot(yb, wb_ref[...]))


def _outproj_even(x2d, ya, yb5, wglut, bglu, wa, wb, bsz, seq):
    m, d = x2d.shape
    tm = min(PROJ_TM, seq)
    per_b = seq // tm
    q = yb5.shape[-1]
    const = lambda i: (0, 0)
    return pl.pallas_call(
        _outproj_even_kernel,
        out_shape=jax.ShapeDtypeStruct((m, d), F32),
        grid=(m // tm,),
        in_specs=[
            pl.BlockSpec((tm, d), lambda i: (i, 0)),
            pl.BlockSpec((tm, SSD_INNER), lambda i: (i, 0)),
            pl.BlockSpec((1, S5_GROUPS, tm // q, S5_GROUP, q),
                         lambda i: (i // per_b, 0, i % per_b, 0, 0)),
            pl.BlockSpec(wglut.shape, const),
            pl.BlockSpec(bglu.shape, const),
            pl.BlockSpec(wa.shape, const),
            pl.BlockSpec(wb.shape, const),
        ],
        out_specs=pl.BlockSpec((tm, d), lambda i: (i, 0)),
        compiler_params=_params("parallel"),
        name="outproj_even",
    )(x2d, ya, yb5, wglut, bglu, wa, wb)


def _head_rmsnorm(x, gain, bd):
    sq = x * x
    hi = sq.astype(BF16)
    lo = (sq - hi.astype(F32)).astype(BF16)
    ss = _dot(hi, bd) + _dot(lo, bd)
    return x * lax.rsqrt(ss * (1.0 / ATT_HEAD_DIM) + EPS) * gain


def _inproj_odd_kernel(x_ref, g_ref, wq_ref, wk_ref, wv_ref, wu_ref, qg_ref, kg_ref, bd_ref,
                       q_ref, k_ref, v_ref, u_ref):
    h = _rmsnorm_bf16(x_ref[...], g_ref[...])
    bd = bd_ref[...]
    q = _head_rmsnorm(_dot(h, wq_ref[...]), qg_ref[...], bd)
    q_ref[...] = (q * (ATT_HEAD_DIM ** -0.5 * LOG2E)).astype(BF16)
    k_ref[...] = _head_rmsnorm(_dot(h, wk_ref[...]), kg_ref[...], bd).astype(BF16)
    v_ref[...] = _dot(h, wv_ref[...]).astype(BF16)
    u_ref[...] = _dot(h, wu_ref[...])


def _inproj_odd(x2d, gain, wq, wk, wv, wu, qg, kg, bd, layer):
    m, d = x2d.shape
    tm = min(PROJ_TM, m)
    const = lambda i: (0, 0)
    w = ATT_WIDTH
    out = jax.ShapeDtypeStruct((m, w), BF16)
    tile = pl.BlockSpec((tm, w), lambda i: (i, 0))
    return pl.pallas_call(
        _inproj_odd_kernel,
        out_shape=(out, out, out, jax.ShapeDtypeStruct((m, w), F32)),
        grid=(m // tm,),
        in_specs=[
            pl.BlockSpec((tm, d), lambda i: (i, 0)),
            pl.BlockSpec((None, 1, d), lambda i: (layer, 0, 0)),
            pl.BlockSpec(wq.shape, const),
            pl.BlockSpec(wk.shape, const),
            pl.BlockSpec(wv.shape, const),
            pl.BlockSpec(wu.shape, const),
            pl.BlockSpec(qg.shape, const),
            pl.BlockSpec(kg.shape, const),
            pl.BlockSpec(bd.shape, const),
        ],
        out_specs=(tile, tile, tile, tile),
        compiler_params=_params("parallel"),
        name="inproj_odd",
    )(x2d, gain, wq, wk, wv, wu, qg, kg, bd)


def _attn_kernel(q_ref, k0_ref, k1_ref, k2_ref, v0_ref, v1_ref, v2_ref, brow_ref, o_ref, bias_ref):
    t = pl.program_id(1)
    tq = q_ref.shape[0]

    @pl.when((pl.program_id(0) == 0) & (t == 0))
    def _():
        rowi = lax.broadcasted_iota(jnp.int32, (tq, 3 * tq), 0)
        coli = lax.broadcasted_iota(jnp.int32, (tq, 3 * tq), 1)
        rel_chunk = coli // ATT_CHUNK - (2 * tq // ATT_CHUNK - LEFT_CHUNKS) - rowi // ATT_CHUNK
        rel_chunk = jnp.where(rel_chunk >= 0, rel_chunk, LEFT_CHUNKS + 1)
        for hh in range(ATT_HEADS):
            base = jnp.broadcast_to(brow_ref[hh:hh + 1, :], (tq, 4 * tq))
            shifted = pltpu.roll(base, 0, 1, stride=1, stride_axis=0)
            bias = shifted[:, tq:] * LOG2E
            for var in range(3):
                first_key = (2 - var) * tq
                keep = jnp.where(coli >= first_key, rel_chunk, LEFT_CHUNKS + 1) <= LEFT_CHUNKS
                bias_ref[var, hh] = jnp.where(keep, bias, -jnp.inf)

    var = jnp.minimum(t, 2)
    q = q_ref[...]
    kcat = jnp.concatenate([k0_ref[...], k1_ref[...], k2_ref[...]], axis=0)
    vcat = jnp.concatenate([v0_ref[...], v1_ref[...], v2_ref[...]], axis=0)
    lane = lax.broadcasted_iota(jnp.int32, (1, LANES), 1)
    heads_per_slab = LANES // ATT_HEAD_DIM
    zero = jnp.zeros((), q.dtype)
    for p in range(ATT_WIDTH // LANES):
        sl = slice(p * LANES, (p + 1) * LANES)
        qp, kp, vp = q[:, sl], kcat[:, sl], vcat[:, sl]
        out = jnp.zeros((tq, LANES), F32)
        for hh in range(heads_per_slab):
            in_head = (lane // ATT_HEAD_DIM) == hh
            qh = jnp.where(in_head, qp, zero)
            s = lax.dot_general(qh, kp, (((1,), (1,)), ((), ())), preferred_element_type=F32)
            s = s + bias_ref[var, p * heads_per_slab + hh]
            e = jnp.exp2(s - jnp.max(s, axis=-1, keepdims=True))
            den = jnp.sum(e, axis=-1, keepdims=True)
            vh = jnp.where(in_head, vp, zero)
            out = out + _dot(e.astype(BF16), vh) / den
        o_ref[:, sl] = out.astype(o_ref.dtype)


def _attention(q, k, v, brow, bsz, seq):
    m, w = q.shape
    tq = ATT_T
    assert 2 * tq >= LEFT_CHUNKS * ATT_CHUNK and tq > MAX_REL and seq % tq == 0
    nt = seq // tq
    cur = lambda b, t: (b * nt + t, 0)
    back1 = lambda b, t: (b * nt + jnp.maximum(t - 1, 0), 0)
    back2 = lambda b, t: (b * nt + jnp.maximum(t - 2, 0), 0)
    blk = lambda f: pl.BlockSpec((tq, w), f)
    return pl.pallas_call(
        _attn_kernel,
        out_shape=jax.ShapeDtypeStruct((m, w), BF16),
        grid=(bsz, nt),
        in_specs=[blk(cur), blk(back2), blk(back1), blk(cur), blk(back2), blk(back1), blk(cur),
                  pl.BlockSpec(brow.shape, lambda b, t: (0, 0))],
        out_specs=blk(cur),
        scratch_shapes=[pltpu.VMEM((3, ATT_HEADS, tq, 3 * tq), F32)],
        compiler_params=_params("arbitrary", "arbitrary"),
        name="band_attention",
    )(q, k, k, k, v, v, v, brow)


def _bias_rows(rel_bias, tq):
    rb = rel_bias.astype(F32)
    nh = rb.shape[0]
    far = jnp.broadcast_to(rb[:, -1:], (nh, 3 * tq - MAX_REL))
    near = jnp.broadcast_to(rb[:, :1], (nh, tq - MAX_REL - 1))
    return jnp.concatenate([far, rb[:, ::-1], near], axis=1)


def _pool_kernel(u_ref, uprev_ref, w_ref, sc_ref, o_ref):
    t = pl.program_id(1)
    tp = u_ref.shape[0]
    u = u_ref[...]
    prev = jnp.where(t > 0, uprev_ref[...], 0.0)
    ext = jnp.concatenate([prev, u], axis=0)
    sums = {}
    s = ext
    w = 1
    while w < POOL_MAX:
        s = s + pltpu.roll(s, w, 0)
        w *= 2
        sums[w] = s[POOL_MAX:]
    pos = t * tp + lax.broadcasted_iota(jnp.int32, (tp, 1), 0)
    outs = []
    for g, win in enumerate(POOL_WINDOWS):
        sl = slice(g * POOL_GROUP, (g + 1) * POOL_GROUP)
        count = jnp.minimum(pos + 1, win).astype(F32)
        pooled = sums[win][:, sl] / count - u[:, sl]
        outs.append(_dot(pooled.astype(BF16), w_ref[g]))
    o_ref[...] = (jnp.concatenate(outs, axis=1) * sc_ref[...]).astype(o_ref.dtype)


def _pool(u, w_pool, scale, bsz, seq):
    m, w = u.shape
    tp = min(POOL_T, seq)
    nt = seq // tp
    return pl.pallas_call(
        _pool_kernel,
        out_shape=jax.ShapeDtypeStruct((m, w), BF16),
        grid=(bsz, nt),
        in_specs=[
            pl.BlockSpec((tp, w), lambda b, t: (b * nt + t, 0)),
            pl.BlockSpec((POOL_MAX, w),
                         lambda b, t: (jnp.maximum((b * nt + t) * (tp // POOL_MAX) - 1, 0), 0)),
            pl.BlockSpec(w_pool.shape, lambda b, t: (0, 0, 0)),
            pl.BlockSpec(scale.shape, lambda b, t: (0, 0)),
        ],
        out_specs=pl.BlockSpec((tp, w), lambda b, t: (b * nt + t, 0)),
        compiler_params=_params("parallel", "arbitrary"),
        name="pool",
    )(u, u, w_pool, scale)


def _outproj_odd_kernel(x_ref, yc_ref, yd_ref, wa_ref, wb_ref, o_ref):
    o_ref[...] = x_ref[...] + _dot(yc_ref[...], wa_ref[...]) + _dot(yd_ref[...], wb_ref[...])


def _outproj_odd(x2d, yc, yd, wa, wb):
    m, d = x2d.shape
    tm = min(PROJ_TM, m)
    const = lambda i: (0, 0)
    return pl.pallas_call(
        _outproj_odd_kernel,
        out_shape=jax.ShapeDtypeStruct((m, d), F32),
        grid=(m // tm,),
        in_specs=[
            pl.BlockSpec((tm, d), lambda i: (i, 0)),
            pl.BlockSpec((tm, ATT_WIDTH), lambda i: (i, 0)),
            pl.BlockSpec((tm, ATT_WIDTH), lambda i: (i, 0)),
            pl.BlockSpec(wa.shape, const),
            pl.BlockSpec(wb.shape, const),
        ],
        out_specs=pl.BlockSpec((tm, d), lambda i: (i, 0)),
        compiler_params=_params("parallel"),
        name="outproj_odd",
    )(x2d, yc, yd, wa, wb)


def _even_mixer(x2d, gain, i, bsz, seq, even_w_in, even_w_out, ssd_conv_w, ssd_conv_b, ssd_dt_bias,
                ssd_a_log, ssd_d, ssd_norm, s5_a_re, s5_a_im, s5_log_dt, s5_b_re, s5_b_im,
                s5_c_re, s5_c_im, s5_d, s5_w_glu, s5_b_glu):
    w_in = even_w_in[i]
    o1 = SSD_INNER
    o2 = o1 + SSD_XBC
    o3 = o2 + SSD_HEADS
    wz = w_in[:, :o1].astype(BF16)
    wx = w_in[:, o1:o2].astype(BF16)
    wdt = jnp.pad(w_in[:, o2:o3], ((0, 0), (0, LANES - SSD_HEADS))).astype(BF16)
    wut = w_in[:, o3:].T.astype(BF16)
    z, xbc, dt, ut = _inproj_even(x2d, gain, wz, wx, wdt, wut, i * 2, bsz, seq)

    pad_h = lambda v: jnp.pad(v.astype(F32), (0, LANES - SSD_HEADS))[None, :]
    expand = (jnp.arange(LANES)[:, None] == (jnp.arange(SSD_INNER)[None, :] // SSD_HEAD_DIM)).astype(BF16)
    ya = _ssd(z, xbc, dt, ssd_conv_w[i], ssd_conv_b[i][None, :], pad_h(ssd_dt_bias[i]),
              pad_h(ssd_a_log[i]), jnp.repeat(ssd_d[i], SSD_HEAD_DIM)[None, :], ssd_norm[i][None, :],
              expand, bsz, seq)

    q = min(S5_Q, seq)
    are, aim = s5_a_re[i], s5_a_im[i]
    dup = lambda v: jnp.concatenate([v, v], axis=-1)
    arow = jnp.stack([dup(are), dup(aim)], axis=1)
    ldt = s5_log_dt[i][:, None, None]
    bd = jnp.concatenate([jnp.swapaxes(s5_b_re[i], 1, 2), jnp.swapaxes(s5_b_im[i], 1, 2)], axis=-1)
    cd_re, cd_im = dup(s5_c_re[i]), dup(s5_c_im[i])
    ct_re, ct_im = jnp.swapaxes(cd_re, 1, 2), jnp.swapaxes(cd_im, 1, 2)
    d_rows = jnp.broadcast_to(s5_d[i][:, :, None], (S5_GROUPS, S5_GROUP, q))
    yb5 = _s5(ut, arow, ldt, bd, cd_re, cd_im, ct_re, ct_im, d_rows)

    w_out = even_w_out[i]
    return _outproj_even(x2d, ya, yb5, s5_w_glu[i].T.astype(BF16), s5_b_glu[i][:, None],
                         w_out[:SSD_INNER].astype(BF16), w_out[SSD_INNER:].astype(BF16), bsz, seq)


def _odd_mixer(x2d, gain, i, bsz, seq, odd_w_in, odd_w_out, attn_q_norm, attn_k_norm, attn_rel_bias,
               pool_w, pool_scale):
    w_in = odd_w_in[i].astype(BF16)
    w = ATT_WIDTH
    tile_gain = lambda v: jnp.tile(v.astype(F32), ATT_HEADS)[None, :]
    head = jnp.arange(w) // ATT_HEAD_DIM
    bd = (head[:, None] == head[None, :]).astype(BF16)
    q, k, v, u = _inproj_odd(x2d, gain, w_in[:, :w], w_in[:, w:2 * w], w_in[:, 2 * w:3 * w], w_in[:, 3 * w:],
                             tile_gain(attn_q_norm[i]), tile_gain(attn_k_norm[i]), bd, i * 2 + 1)
    yc = _attention(q, k, v, _bias_rows(attn_rel_bias[i], ATT_T), bsz, seq)
    yd = _pool(u, pool_w[i].astype(BF16), pool_scale[i][None, :], bsz, seq)
    w_out = odd_w_out[i]
    return _outproj_odd(x2d, yc, yd, w_out[:w].astype(BF16), w_out[w:].astype(BF16))


def kernel(x, ffn1_norm, ffn1_w_gate, ffn1_w_up, ffn1_w_down, mix_norm, even_w_in, even_w_out, ssd_conv_w, ssd_conv_b, ssd_dt_bias, ssd_a_log, ssd_d, ssd_norm, s5_a_re, s5_a_im, s5_log_dt, s5_b_re, s5_b_im, s5_c_re, s5_c_im, s5_d, s5_w_glu, s5_b_glu, odd_w_in, odd_w_out, attn_q_norm, attn_k_norm, attn_rel_bias, pool_w, pool_scale, ffn2_norm, ffn2_w_gate, ffn2_w_up, ffn2_w_down):
    bsz, seq, d = x.shape
    depth = ffn1_norm.shape[0]
    x2d = x.reshape(bsz * seq, d)
    g3 = lambda v: v.astype(F32)[:, None, :]
    f1n, mxn, f2n = g3(ffn1_norm), g3(mix_norm), g3(ffn2_norm)
    f1g, f1u, f1d = ffn1_w_gate.astype(BF16), ffn1_w_up.astype(BF16), ffn1_w_down.astype(BF16)
    f2g, f2u, f2d = ffn2_w_gate.astype(BF16), ffn2_w_up.astype(BF16), ffn2_w_down.astype(BF16)
    for layer in range(depth):
        i = layer // 2
        x2d = _ffn(x2d, f1n, f1g, f1u, f1d, layer)
        if layer % 2 == 0:
            x2d = _even_mixer(x2d, mxn, i, bsz, seq, even_w_in, even_w_out, ssd_conv_w, ssd_conv_b,
                              ssd_dt_bias, ssd_a_log, ssd_d, ssd_norm, s5_a_re, s5_a_im, s5_log_dt,
                              s5_b_re, s5_b_im, s5_c_re, s5_c_im, s5_d, s5_w_glu, s5_b_glu)
        else:
            x2d = _odd_mixer(x2d, mxn, i, bsz, seq, odd_w_in, odd_w_out, attn_q_norm, attn_k_norm,
                             attn_rel_bias, pool_w, pool_scale)
        x2d = _ffn(x2d, f2n, f2g, f2u, f2d, layer)
    return x2d.reshape(bsz, seq, d)
```

```python
import functools

import jax
import jax.numpy as jnp
from jax import lax
from jax.experimental import pallas as pl
from jax.experimental.pallas import tpu as pltpu

F32 = jnp.float32
BF16 = jnp.bfloat16
HIGHEST = lax.Precision.HIGHEST

EPS = 1e-6
LOG2E = 1.4426950408889634
LANES = 128
VMEM_LIMIT = 56 * 1024 * 1024

SSD_HEADS = 8
SSD_HEAD_DIM = 64
SSD_INNER = 512
SSD_GROUPS = 2
SSD_STATE = 128
SSD_CONV = 4
SSD_XBC = 1024
S5_WIDTH = 512
S5_GROUP = 16
S5_GROUPS = 32
S5_STATE = 64
ATT_HEADS = 8
ATT_HEAD_DIM = 64
ATT_WIDTH = 512
ATT_CHUNK = 64
LEFT_CHUNKS = 8
MAX_REL = 128
POOL_WINDOWS = (2, 4, 8, 16)
POOL_GROUP = 128
POOL_MAX = 16

FFN_TM = 512
FFN_TF = 256
PROJ_TM = 512
SSD_Q = 128
S5_Q = 128
ATT_T = 256
POOL_T = 512


def _params(*sem):
    return pltpu.CompilerParams(dimension_semantics=sem, vmem_limit_bytes=VMEM_LIMIT)


def _dot(a, b):
    return jnp.dot(a, b, preferred_element_type=F32)


def _dot_exact(a, b):
    return jnp.dot(a, b, preferred_element_type=F32, precision=HIGHEST)


def _split3(x):
    hi = x.astype(BF16)
    rest = x - hi.astype(F32)
    mid = rest.astype(BF16)
    lo = (rest - mid.astype(F32)).astype(BF16)
    return hi, mid, lo


def _select_left(sel, x):
    hi, mid, lo = _split3(x)
    return _dot(sel, hi) + _dot(sel, mid) + _dot(sel, lo)


def _select_right(x, sel):
    hi, mid, lo = _split3(x)
    return _dot(hi, sel) + _dot(mid, sel) + _dot(lo, sel)


def _rmsnorm_bf16(x, gain):
    ms = jnp.mean(x * x, axis=-1, keepdims=True)
    return (x * lax.rsqrt(ms + EPS) * gain).astype(BF16)


def _silu(x):
    return x * jax.nn.sigmoid(x)


def _ffn_apply(x, g_ref, wg_ref, wu_ref, wd_ref, act_scr):
    h = _rmsnorm_bf16(x, g_ref[...])
    dff = wg_ref.shape[1]
    for c in range(0, dff, FFN_TF):
        gate = _dot(h, wg_ref[:, c:c + FFN_TF])
        up = _dot(h, wu_ref[:, c:c + FFN_TF])
        act_scr[:, c:c + FFN_TF] = (_silu(gate) * up).astype(BF16)
    return x + 0.5 * _dot(act_scr[...], wd_ref[...])


def _resident(shape):
    return pl.BlockSpec(shape, lambda i: (0,) * len(shape), pipeline_mode=pl.Buffered(1))


def _ffn_specs(gain, wg, wu, wd, layer):
    d, dff = wg.shape[1:]
    assert dff % FFN_TF == 0
    whole = pl.Buffered(1)
    return [
        pl.BlockSpec((None, 1, d), lambda i: (layer, 0, 0)),
        pl.BlockSpec((None, d, dff), lambda i: (layer, 0, 0), pipeline_mode=whole),
        pl.BlockSpec((None, d, dff), lambda i: (layer, 0, 0), pipeline_mode=whole),
        pl.BlockSpec((None, dff, d), lambda i: (layer, 0, 0), pipeline_mode=whole),
    ]


def _ffn_inproj_even_kernel(x_ref, fg_ref, wg_ref, wu_ref, wd_ref, g_ref, wz_ref, wx_ref, wdt_ref,
                            wut_ref, x1_ref, z_ref, xbc_ref, dt_ref, ut_ref, act_scr):
    x1 = _ffn_apply(x_ref[...], fg_ref, wg_ref, wu_ref, wd_ref, act_scr)
    x1_ref[...] = x1
    h = _rmsnorm_bf16(x1, g_ref[...])
    z_ref[...] = _dot(h, wz_ref[...])
    xbc_ref[...] = _dot(h, wx_ref[...])
    dt_ref[...] = _dot(h, wdt_ref[...])
    ut = lax.dot_general(wut_ref[...], h, (((1,), (1,)), ((), ())), preferred_element_type=F32)
    q = ut_ref.shape[-1]
    for cc in range(ut_ref.shape[2]):
        ut_ref[0, :, cc] = ut[:, cc * q:(cc + 1) * q].reshape(S5_GROUPS, S5_GROUP, q)


def _ffn_inproj_even(x2d, ffn, gain, wz, wx, wdt, wut, layer, bsz, seq):
    m, d = x2d.shape
    tm = min(FFN_TM, seq)
    per_b = seq // tm
    q = min(S5_Q, seq)
    return pl.pallas_call(
        _ffn_inproj_even_kernel,
        out_shape=(
            jax.ShapeDtypeStruct((m, d), F32),
            jax.ShapeDtypeStruct((m, SSD_INNER), F32),
            jax.ShapeDtypeStruct((m, SSD_XBC), F32),
            jax.ShapeDtypeStruct((m, LANES), F32),
            jax.ShapeDtypeStruct((bsz, S5_GROUPS, seq // q, S5_GROUP, q), F32),
        ),
        grid=(m // tm,),
        in_specs=[pl.BlockSpec((tm, d), lambda i: (i, 0))] + _ffn_specs(*ffn, layer) + [
            pl.BlockSpec((None, 1, d), lambda i: (layer, 0, 0)),
            _resident(wz.shape), _resident(wx.shape), _resident(wdt.shape), _resident(wut.shape),
        ],
        out_specs=(
            pl.BlockSpec((tm, d), lambda i: (i, 0)),
            pl.BlockSpec((tm, SSD_INNER), lambda i: (i, 0)),
            pl.BlockSpec((tm, SSD_XBC), lambda i: (i, 0)),
            pl.BlockSpec((tm, LANES), lambda i: (i, 0)),
            pl.BlockSpec((1, S5_GROUPS, tm // q, S5_GROUP, q),
                         lambda i: (i // per_b, 0, i % per_b, 0, 0)),
        ),
        scratch_shapes=[pltpu.VMEM((tm, ffn[1].shape[-1]), BF16)],
        compiler_params=_params("parallel"),
        name="ffn_inproj_even",
    )(x2d, *ffn, gain, wz, wx, wdt, wut)


def _shift_rows(cur, prev8, j):
    rolled = pltpu.roll(cur, j, 0)
    prev_rolled = pltpu.roll(prev8, j, 0)
    row = lax.broadcasted_iota(jnp.int32, prev8.shape, 0)
    top = jnp.where(row < j, prev_rolled, rolled[:8])
    return jnp.concatenate([top, rolled[8:]], axis=0)


def _ssd_kernel(z_ref, xbc_ref, xprev_ref, dt_ref, cw_ref, cb_ref, dtb_ref, alog_ref,
                dx_ref, ng_ref, e_ref, o_ref, h_scr):
    t = pl.program_id(1)
    q = z_ref.shape[0]
    gw = SSD_INNER // SSD_GROUPS
    hpg = SSD_HEADS // SSD_GROUPS

    @pl.when(t == 0)
    def _():
        h_scr[...] = jnp.zeros_like(h_scr)

    xbc = xbc_ref[...]
    prev = jnp.where(t > 0, xprev_ref[...], 0.0)
    conv = xbc * cw_ref[SSD_CONV - 1:SSD_CONV, :] + cb_ref[...]
    for j in range(1, SSD_CONV):
        conv = conv + _shift_rows(xbc, prev, j) * cw_ref[SSD_CONV - 1 - j:SSD_CONV - j, :]
    xc = _silu(conv)
    xs = xc[:, :SSD_INNER]

    dtr = dt_ref[...] + dtb_ref[...]
    dt = jnp.maximum(dtr, 0.0) + jnp.log1p(jnp.exp(-jnp.abs(dtr)))
    a = -jnp.exp(alog_ref[...])
    row = lax.broadcasted_iota(jnp.int32, (q, q), 0)
    col = lax.broadcasted_iota(jnp.int32, (q, q), 1)
    causal = row >= col
    tril = jnp.where(causal, 1.0, 0.0).astype(BF16)
    acs = _select_left(tril, dt * a)
    acs_t = acs.T
    expanded = _select_right(jnp.concatenate([acs, dt], axis=0), e_ref[...])
    acs_x = expanded[:q]
    dt_x = expanded[q:]
    xdt = xs * dt_x
    decay_in = jnp.exp(acs_x)
    acs_end = acs_x[q - 1:q, :]
    to_end = jnp.exp(acs_end - acs_x)
    chunk_decay = jnp.exp(acs_end)
    xdt_end = xdt * to_end

    lane_head = lax.broadcasted_iota(jnp.int32, (1, gw), 1) // SSD_HEAD_DIM
    ys = []
    for g in range(SSD_GROUPS):
        bm = xc[:, SSD_INNER + g * SSD_STATE:SSD_INNER + (g + 1) * SSD_STATE].astype(BF16)
        cm = xc[:, SSD_INNER + (SSD_GROUPS + g) * SSD_STATE:
                SSD_INNER + (SSD_GROUPS + g + 1) * SSD_STATE].astype(BF16)
        cb = lax.dot_general(cm, bm, (((1,), (1,)), ((), ())), preferred_element_type=F32)
        xdt_g = xdt[:, g * gw:(g + 1) * gw]
        gmats, rhs = [], []
        for j in range(hpg):
            hh = g * hpg + j
            seg = acs[:, hh:hh + 1] - acs_t[hh:hh + 1, :]
            decay = jnp.exp(jnp.where(causal, seg, -jnp.inf))
            gmats.append((cb * decay).astype(BF16))
            rhs.append(jnp.where(lane_head == j, xdt_g, 0.0).astype(BF16))
        y_diag = _dot(jnp.concatenate(gmats, axis=1), jnp.concatenate(rhs, axis=0))
        h_prev = h_scr[g]
        y_off = _dot(cm, h_prev.astype(BF16)) * decay_in[:, g * gw:(g + 1) * gw]
        upd = lax.dot_general(bm, xdt_end[:, g * gw:(g + 1) * gw].astype(BF16),
                              (((0,), (0,)), ((), ())), preferred_element_type=F32)
        h_scr[g] = chunk_decay[:, g * gw:(g + 1) * gw] * h_prev + upd
        ys.append(y_diag + y_off)
    y = jnp.concatenate(ys, axis=1) + dx_ref[...] * xs
    y = y * _silu(z_ref[...])
    outs = []
    for g in range(SSD_GROUPS):
        yg = y[:, g * gw:(g + 1) * gw]
        outs.append(yg * lax.rsqrt(jnp.mean(yg * yg, axis=-1, keepdims=True) + EPS))
    o_ref[...] = (jnp.concatenate(outs, axis=1) * ng_ref[...]).astype(o_ref.dtype)


def _ssd(z, xbc, dt, conv_w, conv_b, dt_bias, a_log, d_x, norm_gain, expand, bsz, seq):
    m = z.shape[0]
    q = min(SSD_Q, seq)
    nq = seq // q
    const = lambda b, t: (0, 0)
    return pl.pallas_call(
        _ssd_kernel,
        out_shape=jax.ShapeDtypeStruct((m, SSD_INNER), BF16),
        grid=(bsz, nq),
        in_specs=[
            pl.BlockSpec((q, SSD_INNER), lambda b, t: (b * nq + t, 0)),
            pl.BlockSpec((q, SSD_XBC), lambda b, t: (b * nq + t, 0)),
            pl.BlockSpec((8, SSD_XBC), lambda b, t: (jnp.maximum((b * nq + t) * (q // 8) - 1, 0), 0)),
            pl.BlockSpec((q, LANES), lambda b, t: (b * nq + t, 0)),
            pl.BlockSpec(conv_w.shape, const),
            pl.BlockSpec(conv_b.shape, const),
            pl.BlockSpec(dt_bias.shape, const),
            pl.BlockSpec(a_log.shape, const),
            pl.BlockSpec(d_x.shape, const),
            pl.BlockSpec(norm_gain.shape, const),
            pl.BlockSpec(expand.shape, const),
        ],
        out_specs=pl.BlockSpec((q, SSD_INNER), lambda b, t: (b * nq + t, 0)),
        scratch_shapes=[pltpu.VMEM((SSD_GROUPS, SSD_STATE, SSD_INNER // SSD_GROUPS), F32)],
        compiler_params=_params("parallel", "arbitrary"),
        name="ssd",
    )(z, xbc, xbc, dt, conv_w, conv_b, dt_bias, a_log, d_x, norm_gain, expand)


def _cpow(base_re, base_im, exponent, nbits):
    shape = jnp.broadcast_shapes(base_re.shape, exponent.shape)
    pr = jnp.ones(shape, F32)
    pi = jnp.zeros(shape, F32)
    sr, si = base_re, base_im
    for k in range(nbits):
        bit = ((exponent >> k) & 1) == 1
        nr = pr * sr - pi * si
        ni = pr * si + pi * sr
        pr = jnp.where(bit, nr, pr)
        pi = jnp.where(bit, ni, pi)
        if k + 1 < nbits:
            sr, si = sr * sr - si * si, 2.0 * sr * si
    return pr, pi


def _s5_discretize(ar, ai, log_dt):
    dt = jnp.exp(log_dt)
    mag = jnp.exp(dt * ar)
    lam_re = mag * jnp.cos(dt * ai)
    lam_im = mag * jnp.sin(dt * ai)
    den = ar * ar + ai * ai
    f_re = ((lam_re - 1.0) * ar + lam_im * ai) / den
    f_im = (lam_im * ar - (lam_re - 1.0) * ai) / den
    return lam_re, lam_im, f_re, f_im


def _s5_kernel(u_ref, arow_ref, ldt_ref, bd_ref, cd_re_ref, cd_im_ref,
               ct_re_ref, ct_im_ref, d_ref, y_ref, kmat_scr, t_scr, f_scr):
    bsz, rows, q = u_ref.shape
    gsz = S5_GROUP
    nc = rows // gsz
    m = bsz * nc
    p2 = 2 * S5_STATE
    nbits = (q - 1).bit_length()
    log_dt = ldt_ref[...]

    lane = lax.broadcasted_iota(jnp.int32, (1, p2), 1)
    first_half_l = lane < S5_STATE
    rowp = lax.broadcasted_iota(jnp.int32, (p2, 1), 0)
    first_half_r = rowp < S5_STATE

    ar_r = arow_ref[0:1, :]
    ai_r = arow_ref[1:2, :]
    lr, li, fr, fi = _s5_discretize(ar_r, ai_r, log_dt)
    sign_b = jnp.where(first_half_l, -1.0, 1.0)
    bd1 = bd_ref[...]
    bd2 = pltpu.roll(bd1, S5_STATE, 1) * sign_b
    bb_a = fr * bd1 + fi * bd2
    bb_b = fr * bd2 - fi * bd1

    srow = lax.broadcasted_iota(jnp.int32, (q, 1), 0)
    pr_re, pr_im = _cpow(lr, li, (q - 1) - srow, nbits)
    for i in range(gsz):
        es = pr_re * bb_a[i:i + 1, :] + pr_im * bb_b[i:i + 1, :]
        t_scr[i * q:(i + 1) * q, gsz * q:gsz * q + p2] = es.astype(BF16)

    eye = (lax.broadcasted_iota(jnp.int32, (p2, p2), 0)
           == lax.broadcasted_iota(jnp.int32, (p2, p2), 1))
    lcr = jnp.sum(jnp.where(eye, lr, 0.0), axis=1, keepdims=True)
    lci = jnp.sum(jnp.where(eye, li, 0.0), axis=1, keepdims=True)
    dlane = lax.broadcasted_iota(jnp.int32, (1, q), 1)
    pw_re, pw_im = _cpow(lcr, lci, dlane, nbits)
    p1_re = pw_re * lcr - pw_im * lci
    p1_im = pw_re * lci + pw_im * lcr
    pow_stack = jnp.where(first_half_r, pw_re, pw_im)

    v1 = bb_a * jnp.where(first_half_l, 1.0, -1.0)
    v2 = -pltpu.roll(bb_a, S5_STATE, 1)
    cre = cd_re_ref[...]
    cim = cd_im_ref[...]
    coef = cre[:, None, :] * v1[None, :, :] + cim[:, None, :] * v2[None, :, :]
    kmat_scr[...] = _dot_exact(coef.reshape(gsz * gsz, p2), pow_stack)

    for o in range(gsz):
        c_r = ct_re_ref[:, o:o + 1]
        c_i = ct_im_ref[:, o:o + 1]
        f_o = jnp.where(first_half_r, c_r * p1_re - c_i * p1_im, -(c_r * p1_im + c_i * p1_re))
        f_scr[:, o * q:(o + 1) * q] = f_o.astype(BF16)

    trow = lax.broadcasted_iota(jnp.int32, (q, q), 0)
    tcol = lax.broadcasted_iota(jnp.int32, (q, q), 1)
    lower = tcol >= trow

    def build(i, carry):
        for o in range(gsz):
            k_row = kmat_scr[pl.ds(o * gsz + i, 1), :]
            blk = pltpu.roll(jnp.broadcast_to(k_row, (q, q)), 0, 1, stride=1, stride_axis=0)
            blk = jnp.where(lower, blk, 0.0).astype(BF16)
            t_scr[pl.ds(pl.multiple_of(i * q, q), q), o * q:(o + 1) * q] = blk
        return carry

    lax.fori_loop(0, gsz, build, 0)

    u_f32 = [jnp.concatenate([u_ref[b, pl.ds(i, nc, stride=gsz), :] for b in range(bsz)], axis=0)
             for i in range(gsz)]
    u_cat = jnp.concatenate([v.astype(BF16) for v in u_f32], axis=1)
    acc = _dot(u_cat, t_scr[...])
    x = acc[:, gsz * q:]

    sq_re, sq_im = lr, li
    for _ in range(q.bit_length() - 1):
        sq_re, sq_im = sq_re * sq_re - sq_im * sq_im, 2.0 * sq_re * sq_im
    crow = lax.broadcasted_iota(jnp.int32, (m, 1), 0) % nc
    k = 1
    while k < nc:
        sh = jnp.where(crow >= k, pltpu.roll(x, k, 0), 0.0)
        x = x + sq_re * sh + (sq_im * sign_b) * pltpu.roll(sh, S5_STATE, 1)
        sq_re, sq_im = sq_re * sq_re - sq_im * sq_im, 2.0 * sq_re * sq_im
        k *= 2
    h_prev = jnp.where(crow >= 1, pltpu.roll(x, 1, 0), 0.0)
    y = acc[:, :gsz * q] + _dot(h_prev.astype(BF16), f_scr[...])
    for o in range(gsz):
        y_o = y[:, o * q:(o + 1) * q] + d_ref[o:o + 1, :] * u_f32[o]
        for b in range(bsz):
            y_ref[b, pl.ds(o, nc, stride=gsz), :] = y_o[b * nc:(b + 1) * nc]


def _s5(u5, arow, ldt, bd, cd_re, cd_im, ct_re, ct_im, d_rows):
    bsz, _, nc, _, q = u5.shape
    rows = nc * S5_GROUP
    p2 = 2 * S5_STATE
    per_g = lambda g: (g, 0, 0)
    y4 = pl.pallas_call(
        _s5_kernel,
        out_shape=jax.ShapeDtypeStruct((bsz, S5_GROUPS, rows, q), F32),
        grid=(S5_GROUPS,),
        in_specs=[
            pl.BlockSpec((bsz, None, rows, q), lambda g: (0, g, 0, 0)),
            pl.BlockSpec((None, 2, p2), per_g),
            pl.BlockSpec((None, 1, 1), per_g),
            pl.BlockSpec((None, S5_GROUP, p2), per_g),
            pl.BlockSpec((None, S5_GROUP, p2), per_g),
            pl.BlockSpec((None, S5_GROUP, p2), per_g),
            pl.BlockSpec((None, p2, S5_GROUP), per_g),
            pl.BlockSpec((None, p2, S5_GROUP), per_g),
            pl.BlockSpec((None, S5_GROUP, q), per_g),
        ],
        out_specs=pl.BlockSpec((bsz, None, rows, q), lambda g: (0, g, 0, 0)),
        scratch_shapes=[
            pltpu.VMEM((S5_GROUP * S5_GROUP, q), F32),
            pltpu.VMEM((S5_GROUP * q, S5_GROUP * q + p2), BF16),
            pltpu.VMEM((p2, S5_GROUP * q), BF16),
        ],
        compiler_params=_params("parallel"),
        name="s5",
    )(u5.reshape(bsz, S5_GROUPS, rows, q), arow, ldt, bd, cd_re, cd_im, ct_re, ct_im, d_rows)
    return y4.reshape(u5.shape)


def _gelu_tanh(x):
    return 0.5 * x * (1.0 + jnp.tanh(0.7978845608028654 * (x + 0.044715 * (x * x * x))))


def _outproj_even_ffn_kernel(x_ref, ya_ref, ybt_ref, wglut_ref, bglu_ref, wa_ref, wb_ref,
                             fg_ref, wg_ref, wu_ref, wd_ref, o_ref, act_scr):
    q = ybt_ref.shape[-1]
    ybt = jnp.concatenate([ybt_ref[0, :, cc].reshape(S5_WIDTH, q) for cc in range(ybt_ref.shape[2])],
                          axis=1)
    gate = _dot(wglut_ref[...], _gelu_tanh(ybt).astype(BF16)) + bglu_ref[...]
    yb = (ybt * jax.nn.sigmoid(gate)).T.astype(BF16)
    x2 = x_ref[...] + _dot(ya_ref[...], wa_ref[...]) + _dot(yb, wb_ref[...])
    o_ref[...] = _ffn_apply(x2, fg_ref, wg_ref, wu_ref, wd_ref, act_scr)


def _outproj_even_ffn(x2d, ya, yb5, wglut, bglu, wa, wb, ffn, layer, bsz, seq):
    m, d = x2d.shape
    tm = min(FFN_TM, seq)
    per_b = seq // tm
    q = yb5.shape[-1]
    return pl.pallas_call(
        _outproj_even_ffn_kernel,
        out_shape=jax.ShapeDtypeStruct((m, d), F32),
        grid=(m // tm,),
        in_specs=[
            pl.BlockSpec((tm, d), lambda i: (i, 0)),
            pl.BlockSpec((tm, SSD_INNER), lambda i: (i, 0)),
            pl.BlockSpec((1, S5_GROUPS, tm // q, S5_GROUP, q),
                         lambda i: (i // per_b, 0, i % per_b, 0, 0)),
            _resident(wglut.shape), _resident(bglu.shape), _resident(wa.shape), _resident(wb.shape),
        ] + _ffn_specs(*ffn, layer),
        out_specs=pl.BlockSpec((tm, d), lambda i: (i, 0)),
        scratch_shapes=[pltpu.VMEM((tm, ffn[1].shape[-1]), BF16)],
        compiler_params=_params("parallel"),
        name="outproj_even_ffn",
    )(x2d, ya, yb5, wglut, bglu, wa, wb, *ffn)


def _head_rmsnorm(x, gain, bd):
    sq = x * x
    hi = sq.astype(BF16)
    lo = (sq - hi.astype(F32)).astype(BF16)
    ss = _dot(hi, bd) + _dot(lo, bd)
    return x * lax.rsqrt(ss * (1.0 / ATT_HEAD_DIM) + EPS) * gain


def _ffn_inproj_odd_kernel(x_ref, fg_ref, wg_ref, wu_ref, wd_ref, g_ref, wq_ref, wk_ref, wv_ref,
                           wup_ref, qg_ref, kg_ref, bd_ref, x1_ref, q_ref, k_ref, v_ref, u_ref, act_scr):
    x1 = _ffn_apply(x_ref[...], fg_ref, wg_ref, wu_ref, wd_ref, act_scr)
    x1_ref[...] = x1
    h = _rmsnorm_bf16(x1, g_ref[...])
    bd = bd_ref[...]
    q = _head_rmsnorm(_dot(h, wq_ref[...]), qg_ref[...], bd)
    q_ref[...] = (q * (ATT_HEAD_DIM ** -0.5 * LOG2E)).astype(BF16)
    k_ref[...] = _head_rmsnorm(_dot(h, wk_ref[...]), kg_ref[...], bd).astype(BF16)
    v_ref[...] = _dot(h, wv_ref[...]).astype(BF16)
    u_ref[...] = _dot(h, wup_ref[...])


def _ffn_inproj_odd(x2d, ffn, gain, wq, wk, wv, wup, qg, kg, bd, layer):
    m, d = x2d.shape
    tm = min(FFN_TM, m)
    w = ATT_WIDTH
    out = jax.ShapeDtypeStruct((m, w), BF16)
    tile = pl.BlockSpec((tm, w), lambda i: (i, 0))
    row_tile = pl.BlockSpec((tm, d), lambda i: (i, 0))
    return pl.pallas_call(
        _ffn_inproj_odd_kernel,
        out_shape=(jax.ShapeDtypeStruct((m, d), F32), out, out, out, jax.ShapeDtypeStruct((m, w), F32)),
        grid=(m // tm,),
        in_specs=[row_tile] + _ffn_specs(*ffn, layer) + [
            pl.BlockSpec((None, 1, d), lambda i: (layer, 0, 0)),
            _resident(wq.shape), _resident(wk.shape), _resident(wv.shape), _resident(wup.shape),
            _resident(qg.shape), _resident(kg.shape), _resident(bd.shape),
        ],
        out_specs=(row_tile, tile, tile, tile, tile),
        scratch_shapes=[pltpu.VMEM((tm, ffn[1].shape[-1]), BF16)],
        compiler_params=_params("parallel"),
        name="ffn_inproj_odd",
    )(x2d, *ffn, gain, wq, wk, wv, wup, qg, kg, bd)


def _attn_kernel(q_ref, k0_ref, k1_ref, k2_ref, v0_ref, v1_ref, v2_ref, brow_ref, o_ref, bias_ref):
    t = pl.program_id(1)
    tq = q_ref.shape[0]

    @pl.when((pl.program_id(0) == 0) & (t == 0))
    def _():
        rowi = lax.broadcasted_iota(jnp.int32, (tq, 3 * tq), 0)
        coli = lax.broadcasted_iota(jnp.int32, (tq, 3 * tq), 1)
        rel_chunk = coli // ATT_CHUNK - (2 * tq // ATT_CHUNK - LEFT_CHUNKS) - rowi // ATT_CHUNK
        rel_chunk = jnp.where(rel_chunk >= 0, rel_chunk, LEFT_CHUNKS + 1)
        for hh in range(ATT_HEADS):
            base = jnp.broadcast_to(brow_ref[hh:hh + 1, :], (tq, 4 * tq))
            shifted = pltpu.roll(base, 0, 1, stride=1, stride_axis=0)
            bias = shifted[:, tq:] * LOG2E
            for var in range(3):
                first_key = (2 - var) * tq
                keep = jnp.where(coli >= first_key, rel_chunk, LEFT_CHUNKS + 1) <= LEFT_CHUNKS
                bias_ref[var, hh] = jnp.where(keep, bias, -jnp.inf)

    var = jnp.minimum(t, 2)
    q = q_ref[...]
    kcat = jnp.concatenate([k0_ref[...], k1_ref[...], k2_ref[...]], axis=0)
    vcat = jnp.concatenate([v0_ref[...], v1_ref[...], v2_ref[...]], axis=0)
    lane = lax.broadcasted_iota(jnp.int32, (1, LANES), 1)
    heads_per_slab = LANES // ATT_HEAD_DIM
    zero = jnp.zeros((), q.dtype)
    for p in range(ATT_WIDTH // LANES):
        sl = slice(p * LANES, (p + 1) * LANES)
        qp, kp, vp = q[:, sl], kcat[:, sl], vcat[:, sl]
        out = jnp.zeros((tq, LANES), F32)
        for hh in range(heads_per_slab):
            in_head = (lane // ATT_HEAD_DIM) == hh
            qh = jnp.where(in_head, qp, zero)
            s = lax.dot_general(qh, kp, (((1,), (1,)), ((), ())), preferred_element_type=F32)
            s = s + bias_ref[var, p * heads_per_slab + hh]
            e = jnp.exp2(s - jnp.max(s, axis=-1, keepdims=True))
            den = jnp.sum(e, axis=-1, keepdims=True)
            vh = jnp.where(in_head, vp, zero)
            out = out + _dot(e.astype(BF16), vh) / den
        o_ref[:, sl] = out.astype(o_ref.dtype)


def _attention(q, k, v, brow, bsz, seq):
    m, w = q.shape
    tq = ATT_T
    assert 2 * tq >= LEFT_CHUNKS * ATT_CHUNK and tq > MAX_REL and seq % tq == 0
    nt = seq // tq
    cur = lambda b, t: (b * nt + t, 0)
    back1 = lambda b, t: (b * nt + jnp.maximum(t - 1, 0), 0)
    back2 = lambda b, t: (b * nt + jnp.maximum(t - 2, 0), 0)
    blk = lambda f: pl.BlockSpec((tq, w), f)
    return pl.pallas_call(
        _attn_kernel,
        out_shape=jax.ShapeDtypeStruct((m, w), BF16),
        grid=(bsz, nt),
        in_specs=[blk(cur), blk(back2), blk(back1), blk(cur), blk(back2), blk(back1), blk(cur),
                  pl.BlockSpec(brow.shape, lambda b, t: (0, 0))],
        out_specs=blk(cur),
        scratch_shapes=[pltpu.VMEM((3, ATT_HEADS, tq, 3 * tq), F32)],
        compiler_params=_params("arbitrary", "arbitrary"),
        name="band_attention",
    )(q, k, k, k, v, v, v, brow)


def _bias_rows(rel_bias, tq):
    rb = rel_bias.astype(F32)
    nh = rb.shape[0]
    far = jnp.broadcast_to(rb[:, -1:], (nh, 3 * tq - MAX_REL))
    near = jnp.broadcast_to(rb[:, :1], (nh, tq - MAX_REL - 1))
    return jnp.concatenate([far, rb[:, ::-1], near], axis=1)


def _pool_kernel(u_ref, uprev_ref, w_ref, sc_ref, o_ref):
    t = pl.program_id(1)
    tp = u_ref.shape[0]
    u = u_ref[...]
    prev = jnp.where(t > 0, uprev_ref[...], 0.0)
    ext = jnp.concatenate([prev, u], axis=0)
    sums = {}
    s = ext
    w = 1
    while w < POOL_MAX:
        s = s + pltpu.roll(s, w, 0)
        w *= 2
        sums[w] = s[POOL_MAX:]
    pos = t * tp + lax.broadcasted_iota(jnp.int32, (tp, 1), 0)
    outs = []
    for g, win in enumerate(POOL_WINDOWS):
        sl = slice(g * POOL_GROUP, (g + 1) * POOL_GROUP)
        count = jnp.minimum(pos + 1, win).astype(F32)
        pooled = sums[win][:, sl] / count - u[:, sl]
        outs.append(_dot(pooled.astype(BF16), w_ref[g]))
    o_ref[...] = (jnp.concatenate(outs, axis=1) * sc_ref[...]).astype(o_ref.dtype)


def _pool(u, w_pool, scale, bsz, seq):
    m, w = u.shape
    tp = min(POOL_T, seq)
    nt = seq // tp
    return pl.pallas_call(
        _pool_kernel,
        out_shape=jax.ShapeDtypeStruct((m, w), BF16),
        grid=(bsz, nt),
        in_specs=[
            pl.BlockSpec((tp, w), lambda b, t: (b * nt + t, 0)),
            pl.BlockSpec((POOL_MAX, w),
                         lambda b, t: (jnp.maximum((b * nt + t) * (tp // POOL_MAX) - 1, 0), 0)),
            pl.BlockSpec(w_pool.shape, lambda b, t: (0, 0, 0)),
            pl.BlockSpec(scale.shape, lambda b, t: (0, 0)),
        ],
        out_specs=pl.BlockSpec((tp, w), lambda b, t: (b * nt + t, 0)),
        compiler_params=_params("parallel", "arbitrary"),
        name="pool",
    )(u, u, w_pool, scale)


def _outproj_odd_ffn_kernel(x_ref, yc_ref, yd_ref, wa_ref, wb_ref, fg_ref, wg_ref, wu_ref, wd_ref,
                            o_ref, act_scr):
    x2 = x_ref[...] + _dot(yc_ref[...], wa_ref[...]) + _dot(yd_ref[...], wb_ref[...])
    o_ref[...] = _ffn_apply(x2, fg_ref, wg_ref, wu_ref, wd_ref, act_scr)


def _outproj_odd_ffn(x2d, yc, yd, wa, wb, ffn, layer):
    m, d = x2d.shape
    tm = min(FFN_TM, m)
    return pl.pallas_call(
        _outproj_odd_ffn_kernel,
        out_shape=jax.ShapeDtypeStruct((m, d), F32),
        grid=(m // tm,),
        in_specs=[
            pl.BlockSpec((tm, d), lambda i: (i, 0)),
            pl.BlockSpec((tm, ATT_WIDTH), lambda i: (i, 0)),
            pl.BlockSpec((tm, ATT_WIDTH), lambda i: (i, 0)),
            _resident(wa.shape), _resident(wb.shape),
        ] + _ffn_specs(*ffn, layer),
        out_specs=pl.BlockSpec((tm, d), lambda i: (i, 0)),
        scratch_shapes=[pltpu.VMEM((tm, ffn[1].shape[-1]), BF16)],
        compiler_params=_params("parallel"),
        name="outproj_odd_ffn",
    )(x2d, yc, yd, wa, wb, *ffn)


def _even_layer(x2d, ffn1, ffn2, gain, layer, bsz, seq, even_w_in, even_w_out, ssd_conv_w, ssd_conv_b,
                ssd_dt_bias, ssd_a_log, ssd_d, ssd_norm, s5_a_re, s5_a_im, s5_log_dt, s5_b_re, s5_b_im,
                s5_c_re, s5_c_im, s5_d, s5_w_glu, s5_b_glu):
    i = layer // 2
    w_in = even_w_in[i]
    o1 = SSD_INNER
    o2 = o1 + SSD_XBC
    o3 = o2 + SSD_HEADS
    wz = w_in[:, :o1].astype(BF16)
    wx = w_in[:, o1:o2].astype(BF16)
    wdt = jnp.pad(w_in[:, o2:o3], ((0, 0), (0, LANES - SSD_HEADS))).astype(BF16)
    wut = w_in[:, o3:].T.astype(BF16)
    x1, z, xbc, dt, ut = _ffn_inproj_even(x2d, ffn1, gain, wz, wx, wdt, wut, layer, bsz, seq)

    pad_h = lambda v: jnp.pad(v.astype(F32), (0, LANES - SSD_HEADS))[None, :]
    expand = (jnp.arange(LANES)[:, None] == (jnp.arange(SSD_INNER)[None, :] // SSD_HEAD_DIM)).astype(BF16)
    ya = _ssd(z, xbc, dt, ssd_conv_w[i], ssd_conv_b[i][None, :], pad_h(ssd_dt_bias[i]),
              pad_h(ssd_a_log[i]), jnp.repeat(ssd_d[i], SSD_HEAD_DIM)[None, :], ssd_norm[i][None, :],
              expand, bsz, seq)

    q = min(S5_Q, seq)
    are, aim = s5_a_re[i], s5_a_im[i]
    dup = lambda v: jnp.concatenate([v, v], axis=-1)
    arow = jnp.stack([dup(are), dup(aim)], axis=1)
    ldt = s5_log_dt[i][:, None, None]
    bd = jnp.concatenate([jnp.swapaxes(s5_b_re[i], 1, 2), jnp.swapaxes(s5_b_im[i], 1, 2)], axis=-1)
    cd_re, cd_im = dup(s5_c_re[i]), dup(s5_c_im[i])
    ct_re, ct_im = jnp.swapaxes(cd_re, 1, 2), jnp.swapaxes(cd_im, 1, 2)
    d_rows = jnp.broadcast_to(s5_d[i][:, :, None], (S5_GROUPS, S5_GROUP, q))
    yb5 = _s5(ut, arow, ldt, bd, cd_re, cd_im, ct_re, ct_im, d_rows)

    w_out = even_w_out[i]
    return _outproj_even_ffn(x1, ya, yb5, s5_w_glu[i].T.astype(BF16), s5_b_glu[i][:, None],
                             w_out[:SSD_INNER].astype(BF16), w_out[SSD_INNER:].astype(BF16),
                             ffn2, layer, bsz, seq)


def _odd_layer(x2d, ffn1, ffn2, gain, layer, bsz, seq, odd_w_in, odd_w_out, attn_q_norm, attn_k_norm,
               attn_rel_bias, pool_w, pool_scale):
    i = layer // 2
    w_in = odd_w_in[i].astype(BF16)
    w = ATT_WIDTH
    tile_gain = lambda v: jnp.tile(v.astype(F32), ATT_HEADS)[None, :]
    head = jnp.arange(w) // ATT_HEAD_DIM
    bd = (head[:, None] == head[None, :]).astype(BF16)
    x1, q, k, v, u = _ffn_inproj_odd(x2d, ffn1, gain, w_in[:, :w], w_in[:, w:2 * w], w_in[:, 2 * w:3 * w],
                                     w_in[:, 3 * w:], tile_gain(attn_q_norm[i]),
                                     tile_gain(attn_k_norm[i]), bd, layer)
    yc = _attention(q, k, v, _bias_rows(attn_rel_bias[i], ATT_T), bsz, seq)
    yd = _pool(u, pool_w[i].astype(BF16), pool_scale[i][None, :], bsz, seq)
    w_out = odd_w_out[i]
    return _outproj_odd_ffn(x1, yc, yd, w_out[:w].astype(BF16), w_out[w:].astype(BF16), ffn2, layer)


def kernel(x, ffn1_norm, ffn1_w_gate, ffn1_w_up, ffn1_w_down, mix_norm, even_w_in, even_w_out, ssd_conv_w, ssd_conv_b, ssd_dt_bias, ssd_a_log, ssd_d, ssd_norm, s5_a_re, s5_a_im, s5_log_dt, s5_b_re, s5_b_im, s5_c_re, s5_c_im, s5_d, s5_w_glu, s5_b_glu, odd_w_in, odd_w_out, attn_q_norm, attn_k_norm, attn_rel_bias, pool_w, pool_scale, ffn2_norm, ffn2_w_gate, ffn2_w_up, ffn2_w_down):
    bsz, seq, d = x.shape
    depth = ffn1_norm.shape[0]
    x2d = x.reshape(bsz * seq, d)
    g3 = lambda v: v.astype(F32)[:, None, :]
    f1n, mxn, f2n = g3(ffn1_norm), g3(mix_norm), g3(ffn2_norm)
    ffn1 = (f1n, ffn1_w_gate.astype(BF16), ffn1_w_up.astype(BF16), ffn1_w_down.astype(BF16))
    ffn2 = (f2n, ffn2_w_gate.astype(BF16), ffn2_w_up.astype(BF16), ffn2_w_down.astype(BF16))
    for layer in range(depth):
        if layer % 2 == 0:
            x2d = _even_layer(x2d, ffn1, ffn2, mxn, layer, bsz, seq, even_w_in, even_w_out, ssd_conv_w,
                              ssd_conv_b, ssd_dt_bias, ssd_a_log, ssd_d, ssd_norm, s5_a_re, s5_a_im,
                              s5_log_dt, s5_b_re, s5_b_im, s5_c_re, s5_c_im, s5_d, s5_w_glu, s5_b_glu)
        else:
            x2d = _odd_layer(x2d, ffn1, ffn2, mxn, layer, bsz, seq, odd_w_in, odd_w_out, attn_q_norm,
                             attn_k_norm, attn_rel_bias, pool_w, pool_scale)
    return x2d.reshape(bsz, seq, d)
```

```python
import functools

import jax
import jax.numpy as jnp
from jax import lax
from jax.experimental import pallas as pl
from jax.experimental.pallas import tpu as pltpu

F32 = jnp.float32
BF16 = jnp.bfloat16
HIGHEST = lax.Precision.HIGHEST

EPS = 1e-6
LOG2E = 1.4426950408889634
MASKED_FLOOR = -1e30
LANES = 128
VMEM_LIMIT = 56 * 1024 * 1024

SSD_HEADS = 8
SSD_HEAD_DIM = 64
SSD_INNER = 512
SSD_GROUPS = 2
SSD_STATE = 128
SSD_CONV = 4
SSD_XBC = 1024
S5_WIDTH = 512
S5_GROUP = 16
S5_GROUPS = 32
S5_STATE = 64
ATT_HEADS = 8
ATT_HEAD_DIM = 64
ATT_WIDTH = 512
ATT_CHUNK = 64
LEFT_CHUNKS = 8
MAX_REL = 128
POOL_WINDOWS = (2, 4, 8, 16)
POOL_GROUP = 128
POOL_MAX = 16

FFN_TM = 512
FFN_TF = 256
SSD_Q = 128
S5_Q = 128
ATT_T = 256
ATT_LOOKAHEAD = 5


def _params(*sem):
    return pltpu.CompilerParams(dimension_semantics=sem, vmem_limit_bytes=VMEM_LIMIT)


def _dot(a, b):
    return jnp.dot(a, b, preferred_element_type=F32)


def _dot_exact(a, b):
    return jnp.dot(a, b, preferred_element_type=F32, precision=HIGHEST)


def _split3(x):
    hi = x.astype(BF16)
    rest = x - hi.astype(F32)
    mid = rest.astype(BF16)
    lo = (rest - mid.astype(F32)).astype(BF16)
    return hi, mid, lo


def _select_left(sel, x):
    hi, mid, lo = _split3(x)
    return _dot(sel, hi) + _dot(sel, mid) + _dot(sel, lo)


def _select_right(x, sel):
    hi, mid, lo = _split3(x)
    return _dot(hi, sel) + _dot(mid, sel) + _dot(lo, sel)


def _rmsnorm_bf16(x, gain):
    ms = jnp.mean(x * x, axis=-1, keepdims=True)
    return (x * lax.rsqrt(ms + EPS) * gain).astype(BF16)


def _silu(x):
    return x * jax.nn.sigmoid(x)


def _ffn_apply(x, g_ref, wg_ref, wu_ref, wd_ref, act_scr):
    h = _rmsnorm_bf16(x, g_ref[...])
    dff = wg_ref.shape[1]
    for c in range(0, dff, FFN_TF):
        gate = _dot(h, wg_ref[:, c:c + FFN_TF])
        up = _dot(h, wu_ref[:, c:c + FFN_TF])
        act_scr[:, c:c + FFN_TF] = (_silu(gate) * up).astype(BF16)
    return x + 0.5 * _dot(act_scr[...], wd_ref[...])


def _resident(shape):
    return pl.BlockSpec(shape, lambda i: (0,) * len(shape), pipeline_mode=pl.Buffered(1))


def _ffn_specs(gain, wg, wu, wd, layer):
    d, dff = wg.shape[1:]
    assert dff % FFN_TF == 0
    whole = pl.Buffered(1)
    return [
        pl.BlockSpec((None, 1, d), lambda i: (layer, 0, 0)),
        pl.BlockSpec((None, d, dff), lambda i: (layer, 0, 0), pipeline_mode=whole),
        pl.BlockSpec((None, d, dff), lambda i: (layer, 0, 0), pipeline_mode=whole),
        pl.BlockSpec((None, dff, d), lambda i: (layer, 0, 0), pipeline_mode=whole),
    ]


def _ffn_inproj_even_kernel(x_ref, fg_ref, wg_ref, wu_ref, wd_ref, g_ref, wz_ref, wx_ref, wdt_ref,
                            wut_ref, x1_ref, z_ref, xbc_ref, dt_ref, ut_ref, act_scr):
    x1 = _ffn_apply(x_ref[...], fg_ref, wg_ref, wu_ref, wd_ref, act_scr)
    x1_ref[...] = x1
    h = _rmsnorm_bf16(x1, g_ref[...])
    z_ref[...] = _dot(h, wz_ref[...])
    xbc_ref[...] = _dot(h, wx_ref[...])
    dt_ref[...] = _dot(h, wdt_ref[...])
    ut = lax.dot_general(wut_ref[...], h, (((1,), (1,)), ((), ())), preferred_element_type=F32)
    q = ut_ref.shape[-1]
    for cc in range(ut_ref.shape[2]):
        ut_ref[0, :, cc] = ut[:, cc * q:(cc + 1) * q].reshape(S5_GROUPS, S5_GROUP, q)


def _ffn_inproj_even(x2d, ffn, gain, wz, wx, wdt, wut, layer, bsz, seq):
    m, d = x2d.shape
    tm = min(FFN_TM, seq)
    per_b = seq // tm
    q = min(S5_Q, seq)
    return pl.pallas_call(
        _ffn_inproj_even_kernel,
        out_shape=(
            jax.ShapeDtypeStruct((m, d), F32),
            jax.ShapeDtypeStruct((m, SSD_INNER), F32),
            jax.ShapeDtypeStruct((m, SSD_XBC), F32),
            jax.ShapeDtypeStruct((m, LANES), F32),
            jax.ShapeDtypeStruct((bsz, S5_GROUPS, seq // q, S5_GROUP, q), F32),
        ),
        grid=(m // tm,),
        in_specs=[pl.BlockSpec((tm, d), lambda i: (i, 0))] + _ffn_specs(*ffn, layer) + [
            pl.BlockSpec((None, 1, d), lambda i: (layer, 0, 0)),
            _resident(wz.shape), _resident(wx.shape), _resident(wdt.shape), _resident(wut.shape),
        ],
        out_specs=(
            pl.BlockSpec((tm, d), lambda i: (i, 0)),
            pl.BlockSpec((tm, SSD_INNER), lambda i: (i, 0)),
            pl.BlockSpec((tm, SSD_XBC), lambda i: (i, 0)),
            pl.BlockSpec((tm, LANES), lambda i: (i, 0)),
            pl.BlockSpec((1, S5_GROUPS, tm // q, S5_GROUP, q),
                         lambda i: (i // per_b, 0, i % per_b, 0, 0)),
        ),
        scratch_shapes=[pltpu.VMEM((tm, ffn[1].shape[-1]), BF16)],
        compiler_params=_params("parallel"),
        name="ffn_inproj_even",
    )(x2d, *ffn, gain, wz, wx, wdt, wut)


def _shift_rows(cur, prev8, j):
    rolled = pltpu.roll(cur, j, 0)
    prev_rolled = pltpu.roll(prev8, j, 0)
    row = lax.broadcasted_iota(jnp.int32, prev8.shape, 0)
    top = jnp.where(row < j, prev_rolled, rolled[:8])
    return jnp.concatenate([top, rolled[8:]], axis=0)


def _ssd_kernel(z_ref, xbc_ref, xprev_ref, dt_ref, cw_ref, cb_ref, dtb_ref, alog_ref,
                dx_ref, ng_ref, e_ref, o_ref, h_scr):
    t = pl.program_id(1)
    q = z_ref.shape[0]
    gw = SSD_INNER // SSD_GROUPS
    hpg = SSD_HEADS // SSD_GROUPS

    @pl.when(t == 0)
    def _():
        h_scr[...] = jnp.zeros_like(h_scr)

    xbc = xbc_ref[...]
    prev = jnp.where(t > 0, xprev_ref[...], 0.0)
    conv = xbc * cw_ref[SSD_CONV - 1:SSD_CONV, :] + cb_ref[...]
    for j in range(1, SSD_CONV):
        conv = conv + _shift_rows(xbc, prev, j) * cw_ref[SSD_CONV - 1 - j:SSD_CONV - j, :]
    xc = _silu(conv)
    xs = xc[:, :SSD_INNER]

    dtr = dt_ref[...] + dtb_ref[...]
    dt = jnp.maximum(dtr, 0.0) + jnp.log1p(jnp.exp(-jnp.abs(dtr)))
    a = -jnp.exp(alog_ref[...])
    row = lax.broadcasted_iota(jnp.int32, (q, q), 0)
    col = lax.broadcasted_iota(jnp.int32, (q, q), 1)
    causal = row >= col
    tril = jnp.where(causal, 1.0, 0.0).astype(BF16)
    acs = _select_left(tril, dt * a)
    acs_t = acs.T
    expanded = _select_right(jnp.concatenate([acs, dt], axis=0), e_ref[...])
    acs_x = expanded[:q]
    dt_x = expanded[q:]
    xdt = xs * dt_x
    decay_in = jnp.exp(acs_x)
    acs_end = acs_x[q - 1:q, :]
    to_end = jnp.exp(acs_end - acs_x)
    chunk_decay = jnp.exp(acs_end)
    xdt_end = xdt * to_end

    lane_head = lax.broadcasted_iota(jnp.int32, (1, gw), 1) // SSD_HEAD_DIM
    ys = []
    for g in range(SSD_GROUPS):
        bm = xc[:, SSD_INNER + g * SSD_STATE:SSD_INNER + (g + 1) * SSD_STATE].astype(BF16)
        cm = xc[:, SSD_INNER + (SSD_GROUPS + g) * SSD_STATE:
                SSD_INNER + (SSD_GROUPS + g + 1) * SSD_STATE].astype(BF16)
        cb = lax.dot_general(cm, bm, (((1,), (1,)), ((), ())), preferred_element_type=F32)
        xdt_g = xdt[:, g * gw:(g + 1) * gw]
        gmats, rhs = [], []
        for j in range(hpg):
            hh = g * hpg + j
            seg = acs[:, hh:hh + 1] - acs_t[hh:hh + 1, :]
            decay = jnp.exp(jnp.where(causal, seg, -jnp.inf))
            gmats.append((cb * decay).astype(BF16))
            rhs.append(jnp.where(lane_head == j, xdt_g, 0.0).astype(BF16))
        y_diag = _dot(jnp.concatenate(gmats, axis=1), jnp.concatenate(rhs, axis=0))
        h_prev = h_scr[g]
        y_off = _dot(cm, h_prev.astype(BF16)) * decay_in[:, g * gw:(g + 1) * gw]
        upd = lax.dot_general(bm, xdt_end[:, g * gw:(g + 1) * gw].astype(BF16),
                              (((0,), (0,)), ((), ())), preferred_element_type=F32)
        h_scr[g] = chunk_decay[:, g * gw:(g + 1) * gw] * h_prev + upd
        ys.append(y_diag + y_off)
    y = jnp.concatenate(ys, axis=1) + dx_ref[...] * xs
    y = y * _silu(z_ref[...])
    outs = []
    for g in range(SSD_GROUPS):
        yg = y[:, g * gw:(g + 1) * gw]
        outs.append(yg * lax.rsqrt(jnp.mean(yg * yg, axis=-1, keepdims=True) + EPS))
    o_ref[...] = (jnp.concatenate(outs, axis=1) * ng_ref[...]).astype(o_ref.dtype)


def _ssd(z, xbc, dt, conv_w, conv_b, dt_bias, a_log, d_x, norm_gain, expand, bsz, seq):
    m = z.shape[0]
    q = min(SSD_Q, seq)
    nq = seq // q
    const = lambda b, t: (0, 0)
    return pl.pallas_call(
        _ssd_kernel,
        out_shape=jax.ShapeDtypeStruct((m, SSD_INNER), BF16),
        grid=(bsz, nq),
        in_specs=[
            pl.BlockSpec((q, SSD_INNER), lambda b, t: (b * nq + t, 0)),
            pl.BlockSpec((q, SSD_XBC), lambda b, t: (b * nq + t, 0)),
            pl.BlockSpec((8, SSD_XBC), lambda b, t: (jnp.maximum((b * nq + t) * (q // 8) - 1, 0), 0)),
            pl.BlockSpec((q, LANES), lambda b, t: (b * nq + t, 0)),
            pl.BlockSpec(conv_w.shape, const),
            pl.BlockSpec(conv_b.shape, const),
            pl.BlockSpec(dt_bias.shape, const),
            pl.BlockSpec(a_log.shape, const),
            pl.BlockSpec(d_x.shape, const),
            pl.BlockSpec(norm_gain.shape, const),
            pl.BlockSpec(expand.shape, const),
        ],
        out_specs=pl.BlockSpec((q, SSD_INNER), lambda b, t: (b * nq + t, 0)),
        scratch_shapes=[pltpu.VMEM((SSD_GROUPS, SSD_STATE, SSD_INNER // SSD_GROUPS), F32)],
        compiler_params=_params("parallel", "arbitrary"),
        name="ssd",
    )(z, xbc, xbc, dt, conv_w, conv_b, dt_bias, a_log, d_x, norm_gain, expand)


def _cpow(base_re, base_im, exponent, nbits):
    shape = jnp.broadcast_shapes(base_re.shape, exponent.shape)
    pr = jnp.ones(shape, F32)
    pi = jnp.zeros(shape, F32)
    sr, si = base_re, base_im
    for k in range(nbits):
        bit = ((exponent >> k) & 1) == 1
        nr = pr * sr - pi * si
        ni = pr * si + pi * sr
        pr = jnp.where(bit, nr, pr)
        pi = jnp.where(bit, ni, pi)
        if k + 1 < nbits:
            sr, si = sr * sr - si * si, 2.0 * sr * si
    return pr, pi


def _s5_discretize(ar, ai, log_dt):
    dt = jnp.exp(log_dt)
    mag = jnp.exp(dt * ar)
    lam_re = mag * jnp.cos(dt * ai)
    lam_im = mag * jnp.sin(dt * ai)
    den = ar * ar + ai * ai
    f_re = ((lam_re - 1.0) * ar + lam_im * ai) / den
    f_im = (lam_im * ar - (lam_re - 1.0) * ai) / den
    return lam_re, lam_im, f_re, f_im


def _s5_kernel(u_ref, arow_ref, ldt_ref, bd_ref, cd_re_ref, cd_im_ref,
               ct_re_ref, ct_im_ref, d_ref, y_ref, kmat_scr, t_scr, f_scr):
    bsz, rows, q = u_ref.shape
    gsz = S5_GROUP
    nc = rows // gsz
    m = bsz * nc
    p2 = 2 * S5_STATE
    nbits = (q - 1).bit_length()
    log_dt = ldt_ref[...]

    lane = lax.broadcasted_iota(jnp.int32, (1, p2), 1)
    first_half_l = lane < S5_STATE
    rowp = lax.broadcasted_iota(jnp.int32, (p2, 1), 0)
    first_half_r = rowp < S5_STATE

    ar_r = arow_ref[0:1, :]
    ai_r = arow_ref[1:2, :]
    lr, li, fr, fi = _s5_discretize(ar_r, ai_r, log_dt)
    sign_b = jnp.where(first_half_l, -1.0, 1.0)
    bd1 = bd_ref[...]
    bd2 = pltpu.roll(bd1, S5_STATE, 1) * sign_b
    bb_a = fr * bd1 + fi * bd2
    bb_b = fr * bd2 - fi * bd1

    srow = lax.broadcasted_iota(jnp.int32, (q, 1), 0)
    pr_re, pr_im = _cpow(lr, li, (q - 1) - srow, nbits)
    for i in range(gsz):
        es = pr_re * bb_a[i:i + 1, :] + pr_im * bb_b[i:i + 1, :]
        t_scr[i * q:(i + 1) * q, gsz * q:gsz * q + p2] = es.astype(BF16)

    eye = (lax.broadcasted_iota(jnp.int32, (p2, p2), 0)
           == lax.broadcasted_iota(jnp.int32, (p2, p2), 1))
    lcr = jnp.sum(jnp.where(eye, lr, 0.0), axis=1, keepdims=True)
    lci = jnp.sum(jnp.where(eye, li, 0.0), axis=1, keepdims=True)
    dlane = lax.broadcasted_iota(jnp.int32, (1, q), 1)
    pw_re, pw_im = _cpow(lcr, lci, dlane, nbits)
    p1_re = pw_re * lcr - pw_im * lci
    p1_im = pw_re * lci + pw_im * lcr
    pow_stack = jnp.where(first_half_r, pw_re, pw_im)

    v1 = bb_a * jnp.where(first_half_l, 1.0, -1.0)
    v2 = -pltpu.roll(bb_a, S5_STATE, 1)
    cre = cd_re_ref[...]
    cim = cd_im_ref[...]
    coef = cre[:, None, :] * v1[None, :, :] + cim[:, None, :] * v2[None, :, :]
    kmat_scr[...] = _dot_exact(coef.reshape(gsz * gsz, p2), pow_stack)

    for o in range(gsz):
        c_r = ct_re_ref[:, o:o + 1]
        c_i = ct_im_ref[:, o:o + 1]
        f_o = jnp.where(first_half_r, c_r * p1_re - c_i * p1_im, -(c_r * p1_im + c_i * p1_re))
        f_scr[:, o * q:(o + 1) * q] = f_o.astype(BF16)

    trow = lax.broadcasted_iota(jnp.int32, (q, q), 0)
    tcol = lax.broadcasted_iota(jnp.int32, (q, q), 1)
    lower = tcol >= trow

    def build(i, carry):
        for o in range(gsz):
            k_row = kmat_scr[pl.ds(o * gsz + i, 1), :]
            blk = pltpu.roll(jnp.broadcast_to(k_row, (q, q)), 0, 1, stride=1, stride_axis=0)
            blk = jnp.where(lower, blk, 0.0).astype(BF16)
            t_scr[pl.ds(pl.multiple_of(i * q, q), q), o * q:(o + 1) * q] = blk
        return carry

    lax.fori_loop(0, gsz, build, 0)

    u_f32 = [jnp.concatenate([u_ref[b, pl.ds(i, nc, stride=gsz), :] for b in range(bsz)], axis=0)
             for i in range(gsz)]
    u_cat = jnp.concatenate([v.astype(BF16) for v in u_f32], axis=1)
    acc = _dot(u_cat, t_scr[...])
    x = acc[:, gsz * q:]

    sq_re, sq_im = lr, li
    for _ in range(q.bit_length() - 1):
        sq_re, sq_im = sq_re * sq_re - sq_im * sq_im, 2.0 * sq_re * sq_im
    crow = lax.broadcasted_iota(jnp.int32, (m, 1), 0) % nc
    k = 1
    while k < nc:
        sh = jnp.where(crow >= k, pltpu.roll(x, k, 0), 0.0)
        x = x + sq_re * sh + (sq_im * sign_b) * pltpu.roll(sh, S5_STATE, 1)
        sq_re, sq_im = sq_re * sq_re - sq_im * sq_im, 2.0 * sq_re * sq_im
        k *= 2
    h_prev = jnp.where(crow >= 1, pltpu.roll(x, 1, 0), 0.0)
    y = acc[:, :gsz * q] + _dot(h_prev.astype(BF16), f_scr[...])
    for o in range(gsz):
        y_o = y[:, o * q:(o + 1) * q] + d_ref[o:o + 1, :] * u_f32[o]
        for b in range(bsz):
            y_ref[b, pl.ds(o, nc, stride=gsz), :] = y_o[b * nc:(b + 1) * nc]


def _s5(u5, arow, ldt, bd, cd_re, cd_im, ct_re, ct_im, d_rows):
    bsz, _, nc, _, q = u5.shape
    rows = nc * S5_GROUP
    p2 = 2 * S5_STATE
    per_g = lambda g: (g, 0, 0)
    y4 = pl.pallas_call(
        _s5_kernel,
        out_shape=jax.ShapeDtypeStruct((bsz, S5_GROUPS, rows, q), F32),
        grid=(S5_GROUPS,),
        in_specs=[
            pl.BlockSpec((bsz, None, rows, q), lambda g: (0, g, 0, 0)),
            pl.BlockSpec((None, 2, p2), per_g),
            pl.BlockSpec((None, 1, 1), per_g),
            pl.BlockSpec((None, S5_GROUP, p2), per_g),
            pl.BlockSpec((None, S5_GROUP, p2), per_g),
            pl.BlockSpec((None, S5_GROUP, p2), per_g),
            pl.BlockSpec((None, p2, S5_GROUP), per_g),
            pl.BlockSpec((None, p2, S5_GROUP), per_g),
            pl.BlockSpec((None, S5_GROUP, q), per_g),
        ],
        out_specs=pl.BlockSpec((bsz, None, rows, q), lambda g: (0, g, 0, 0)),
        scratch_shapes=[
            pltpu.VMEM((S5_GROUP * S5_GROUP, q), F32),
            pltpu.VMEM((S5_GROUP * q, S5_GROUP * q + p2), BF16),
            pltpu.VMEM((p2, S5_GROUP * q), BF16),
        ],
        compiler_params=_params("parallel"),
        name="s5",
    )(u5.reshape(bsz, S5_GROUPS, rows, q), arow, ldt, bd, cd_re, cd_im, ct_re, ct_im, d_rows)
    return y4.reshape(u5.shape)


def _gelu_tanh(x):
    return 0.5 * x * (1.0 + jnp.tanh(0.7978845608028654 * (x + 0.044715 * (x * x * x))))


def _outproj_even_ffn_kernel(x_ref, ya_ref, ybt_ref, wglut_ref, bglu_ref, wa_ref, wb_ref,
                             fg_ref, wg_ref, wu_ref, wd_ref, o_ref, act_scr):
    q = ybt_ref.shape[-1]
    ybt = jnp.concatenate([ybt_ref[0, :, cc].reshape(S5_WIDTH, q) for cc in range(ybt_ref.shape[2])],
                          axis=1)
    gate = _dot(wglut_ref[...], _gelu_tanh(ybt).astype(BF16)) + bglu_ref[...]
    yb = (ybt * jax.nn.sigmoid(gate)).T.astype(BF16)
    x2 = x_ref[...] + _dot(ya_ref[...], wa_ref[...]) + _dot(yb, wb_ref[...])
    o_ref[...] = _ffn_apply(x2, fg_ref, wg_ref, wu_ref, wd_ref, act_scr)


def _outproj_even_ffn(x2d, ya, yb5, wglut, bglu, wa, wb, ffn, layer, bsz, seq):
    m, d = x2d.shape
    tm = min(FFN_TM, seq)
    per_b = seq // tm
    q = yb5.shape[-1]
    return pl.pallas_call(
        _outproj_even_ffn_kernel,
        out_shape=jax.ShapeDtypeStruct((m, d), F32),
        grid=(m // tm,),
        in_specs=[
            pl.BlockSpec((tm, d), lambda i: (i, 0)),
            pl.BlockSpec((tm, SSD_INNER), lambda i: (i, 0)),
            pl.BlockSpec((1, S5_GROUPS, tm // q, S5_GROUP, q),
                         lambda i: (i // per_b, 0, i % per_b, 0, 0)),
            _resident(wglut.shape), _resident(bglu.shape), _resident(wa.shape), _resident(wb.shape),
        ] + _ffn_specs(*ffn, layer),
        out_specs=pl.BlockSpec((tm, d), lambda i: (i, 0)),
        scratch_shapes=[pltpu.VMEM((tm, ffn[1].shape[-1]), BF16)],
        compiler_params=_params("parallel"),
        name="outproj_even_ffn",
    )(x2d, ya, yb5, wglut, bglu, wa, wb, *ffn)


def _head_rmsnorm(x, gain, bd):
    ss = _dot((x * x).astype(BF16), bd)
    return x * lax.rsqrt(ss * (1.0 / ATT_HEAD_DIM) + EPS) * gain


def _head_rmsnorm_t(xt, gain_col, bd):
    ss = _dot(bd, (xt * xt).astype(BF16))
    return xt * lax.rsqrt(ss * (1.0 / ATT_HEAD_DIM) + EPS) * gain_col


def _pool_apply(u, prev, pos, w_ref, sc_ref):
    ext = jnp.concatenate([prev, u], axis=0)
    sums = {}
    s = ext
    w = 1
    while w < POOL_MAX:
        s = s + pltpu.roll(s, w, 0)
        w *= 2
        sums[w] = s[POOL_MAX:]
    outs = []
    for g, win in enumerate(POOL_WINDOWS):
        sl = slice(g * POOL_GROUP, (g + 1) * POOL_GROUP)
        count = jnp.minimum(pos + 1, win).astype(F32)
        pooled = sums[win][:, sl] / count - u[:, sl]
        outs.append(_dot(pooled.astype(BF16), w_ref[g]))
    return jnp.concatenate(outs, axis=1) * sc_ref[...]


def _ffn_inproj_odd_kernel(x_ref, fg_ref, wg_ref, wu_ref, wd_ref, g_ref, wqt_ref, wk_ref, wvt_ref,
                           wup_ref, qg_ref, kg_ref, bd_ref, pw_ref, ps_ref,
                           x1_ref, qt_ref, k_ref, vt_ref, yd_ref, act_scr, carry_scr, *, tiles_per_seq):
    i = pl.program_id(0)
    tm = x_ref.shape[0]

    @pl.when(i == 0)
    def _():
        carry_scr[...] = jnp.zeros_like(carry_scr)

    x1 = _ffn_apply(x_ref[...], fg_ref, wg_ref, wu_ref, wd_ref, act_scr)
    x1_ref[...] = x1
    h = _rmsnorm_bf16(x1, g_ref[...])
    bd = bd_ref[...]
    nt = (((1,), (1,)), ((), ()))
    qt = lax.dot_general(wqt_ref[...], h, nt, preferred_element_type=F32)
    qt = _head_rmsnorm_t(qt, qg_ref[...], bd) * (ATT_HEAD_DIM ** -0.5 * LOG2E)
    qt_ref[0] = qt.astype(BF16)
    k_ref[...] = _head_rmsnorm(_dot(h, wk_ref[...]), kg_ref[...], bd).astype(BF16)
    vt_ref[0] = lax.dot_general(wvt_ref[...], h, nt, preferred_element_type=F32).astype(BF16)
    u = _dot(h, wup_ref[...])
    tile_in_seq = i % tiles_per_seq
    prev = jnp.where(tile_in_seq == 0, 0.0, carry_scr[...])
    pos = tile_in_seq * tm + lax.broadcasted_iota(jnp.int32, (tm, 1), 0)
    yd_ref[...] = _pool_apply(u, prev, pos, pw_ref, ps_ref).astype(yd_ref.dtype)
    carry_scr[...] = u[tm - POOL_MAX:]


def _ffn_inproj_odd(x2d, ffn, gain, wqt, wk, wvt, wup, qg_col, kg, bd, w_pool, pool_scale, layer, bsz, seq):
    m, d = x2d.shape
    tm = min(FFN_TM, seq)
    per_b = seq // tm
    w = ATT_WIDTH
    tile = pl.BlockSpec((tm, w), lambda i: (i, 0))
    tile_t = pl.BlockSpec((1, w, tm), lambda i: (i // per_b, 0, i % per_b))
    row_tile = pl.BlockSpec((tm, d), lambda i: (i, 0))
    tok = jax.ShapeDtypeStruct((m, w), BF16)
    chan = jax.ShapeDtypeStruct((bsz, w, seq), BF16)
    return pl.pallas_call(
        functools.partial(_ffn_inproj_odd_kernel, tiles_per_seq=per_b),
        out_shape=(jax.ShapeDtypeStruct((m, d), F32), chan, tok, chan, tok),
        grid=(m // tm,),
        in_specs=[row_tile] + _ffn_specs(*ffn, layer) + [
            pl.BlockSpec((None, 1, d), lambda i: (layer, 0, 0)),
            _resident(wqt.shape), _resident(wk.shape), _resident(wvt.shape), _resident(wup.shape),
            _resident(qg_col.shape), _resident(kg.shape), _resident(bd.shape),
            _resident(w_pool.shape), _resident(pool_scale.shape),
        ],
        out_specs=(row_tile, tile_t, tile, tile_t, tile),
        scratch_shapes=[pltpu.VMEM((tm, ffn[1].shape[-1]), BF16), pltpu.VMEM((POOL_MAX, w), F32)],
        compiler_params=_params("arbitrary"),
        name="ffn_inproj_odd",
    )(x2d, *ffn, gain, wqt, wk, wvt, wup, qg_col, kg, bd, w_pool, pool_scale)


def _attn_kernel(qt_ref, k0_ref, k1_ref, k2_ref, vt0_ref, vt1_ref, vt2_ref, brow_ref, o_ref, bias_ref):
    t = pl.program_id(1)
    tq = o_ref.shape[0]
    nk = 3 * tq

    @pl.when((pl.program_id(0) == 0) & (t == 0))
    def _():
        keyi = lax.broadcasted_iota(jnp.int32, (nk, tq), 0)
        qryi = lax.broadcasted_iota(jnp.int32, (nk, tq), 1)
        rel_chunk = keyi // ATT_CHUNK - (2 * tq // ATT_CHUNK - LEFT_CHUNKS) - qryi // ATT_CHUNK
        rel_chunk = jnp.where(rel_chunk >= 0, rel_chunk, LEFT_CHUNKS + 1)
        for hh in range(ATT_HEADS):
            base = jnp.broadcast_to(brow_ref[hh:hh + 1, :], (nk, 4 * tq))
            shifted = pltpu.roll(base, 0, 1, stride=1, stride_axis=0)
            bias = shifted[:, nk:] * LOG2E
            for var in range(3):
                first_key = (2 - var) * tq
                keep = jnp.where(keyi >= first_key, rel_chunk, LEFT_CHUNKS + 1) <= LEFT_CHUNKS
                bias_ref[var, hh] = jnp.where(keep, bias, -jnp.inf)

    var = jnp.minimum(t, 2)
    qt = qt_ref[0]
    kcat = jnp.concatenate([k0_ref[...], k1_ref[...], k2_ref[...]], axis=0)
    vtcat = jnp.concatenate([vt0_ref[0], vt1_ref[0], vt2_ref[0]], axis=1)
    hd = ATT_HEAD_DIM
    heads_per_slab = LANES // hd
    no_q = jnp.zeros((LANES - hd, tq), qt.dtype)
    ones_rows = jnp.ones((16, nk), qt.dtype)
    n_tiles = nk // tq

    def scores(head, j):
        p, hh = divmod(head, heads_per_slab)
        rows = slice(head * hd, (head + 1) * hd)
        qth = jnp.concatenate([qt[rows]] + [no_q] if hh == 0 else [no_q] + [qt[rows]], axis=0)
        ks = slice(j * tq, (j + 1) * tq)
        return _dot(kcat[ks, p * LANES:(p + 1) * LANES], qth) + bias_ref[var, head, ks, :]

    items = [(head, j) for head in range(ATT_HEADS) for j in range(n_tiles)]
    ahead = [scores(*it) for it in items[:ATT_LOOKAHEAD]]
    outs, parts, tops = [], [], []
    for idx, (head, j) in enumerate(items):
        st = ahead.pop(0)
        if idx + ATT_LOOKAHEAD < len(items):
            ahead.append(scores(*items[idx + ATT_LOOKAHEAD]))
        rows = slice(head * hd, (head + 1) * hd)
        ks = slice(j * tq, (j + 1) * tq)
        top = jnp.maximum(jnp.max(st, axis=0, keepdims=True), MASKED_FLOOR)
        e = jnp.exp2(st - top).astype(BF16)
        vth = jnp.concatenate([vtcat[rows, ks], ones_rows[:, ks]], axis=0)
        parts.append(_dot(vth, e))
        tops.append(top)
        if j == n_tiles - 1:
            top_all = functools.reduce(jnp.maximum, tops)
            ot = sum(part * jnp.exp2(tp - top_all) for part, tp in zip(parts, tops))
            outs.append(ot[:hd] / ot[hd:hd + 1])
            parts, tops = [], []
            if head % heads_per_slab == heads_per_slab - 1:
                p = head // heads_per_slab
                o_ref[:, p * LANES:(p + 1) * LANES] = jnp.concatenate(outs, axis=0).T.astype(o_ref.dtype)
                outs = []


def _attention(qt, k, vt, brow, bsz, seq):
    m, w = k.shape
    tq = ATT_T
    assert 2 * tq >= LEFT_CHUNKS * ATT_CHUNK and tq > MAX_REL and seq % tq == 0
    nt = seq // tq
    back = lambda n: (lambda b, t: (b * nt + jnp.maximum(t - n, 0), 0))
    back_t = lambda n: (lambda b, t: (b, 0, jnp.maximum(t - n, 0)))
    tok = lambda f: pl.BlockSpec((tq, w), f)
    chan = lambda f: pl.BlockSpec((1, w, tq), f)
    return pl.pallas_call(
        _attn_kernel,
        out_shape=jax.ShapeDtypeStruct((m, w), BF16),
        grid=(bsz, nt),
        in_specs=[chan(back_t(0)), tok(back(2)), tok(back(1)), tok(back(0)),
                  chan(back_t(2)), chan(back_t(1)), chan(back_t(0)),
                  pl.BlockSpec(brow.shape, lambda b, t: (0, 0))],
        out_specs=tok(back(0)),
        scratch_shapes=[pltpu.VMEM((3, ATT_HEADS, 3 * tq, tq), F32)],
        compiler_params=_params("arbitrary", "arbitrary"),
        name="band_attention",
    )(qt, k, k, k, vt, vt, vt, brow)


def _bias_rows(rel_bias, tq):
    rb = rel_bias.astype(F32)
    nh = rb.shape[0]
    near = jnp.broadcast_to(rb[:, :1], (nh, tq - MAX_REL))
    far = jnp.broadcast_to(rb[:, -1:], (nh, 3 * tq - MAX_REL - 1))
    return jnp.concatenate([near, rb, far], axis=1)


def _outproj_odd_ffn_kernel(x_ref, yc_ref, yd_ref, wa_ref, wb_ref, fg_ref, wg_ref, wu_ref, wd_ref,
                            o_ref, act_scr):
    x2 = x_ref[...] + _dot(yc_ref[...], wa_ref[...]) + _dot(yd_ref[...], wb_ref[...])
    o_ref[...] = _ffn_apply(x2, fg_ref, wg_ref, wu_ref, wd_ref, act_scr)


def _outproj_odd_ffn(x2d, yc, yd, wa, wb, ffn, layer):
    m, d = x2d.shape
    tm = min(FFN_TM, m)
    return pl.pallas_call(
        _outproj_odd_ffn_kernel,
        out_shape=jax.ShapeDtypeStruct((m, d), F32),
        grid=(m // tm,),
        in_specs=[
            pl.BlockSpec((tm, d), lambda i: (i, 0)),
            pl.BlockSpec((tm, ATT_WIDTH), lambda i: (i, 0)),
            pl.BlockSpec((tm, ATT_WIDTH), lambda i: (i, 0)),
            _resident(wa.shape), _resident(wb.shape),
        ] + _ffn_specs(*ffn, layer),
        out_specs=pl.BlockSpec((tm, d), lambda i: (i, 0)),
        scratch_shapes=[pltpu.VMEM((tm, ffn[1].shape[-1]), BF16)],
        compiler_params=_params("parallel"),
        name="outproj_odd_ffn",
    )(x2d, yc, yd, wa, wb, *ffn)


def _even_layer(x2d, ffn1, ffn2, gain, layer, bsz, seq, even_w_in, even_w_out, ssd_conv_w, ssd_conv_b,
                ssd_dt_bias, ssd_a_log, ssd_d, ssd_norm, s5_a_re, s5_a_im, s5_log_dt, s5_b_re, s5_b_im,
                s5_c_re, s5_c_im, s5_d, s5_w_glu, s5_b_glu):
    i = layer // 2
    w_in = even_w_in[i]
    o1 = SSD_INNER
    o2 = o1 + SSD_XBC
    o3 = o2 + SSD_HEADS
    wz = w_in[:, :o1].astype(BF16)
    wx = w_in[:, o1:o2].astype(BF16)
    wdt = jnp.pad(w_in[:, o2:o3], ((0, 0), (0, LANES - SSD_HEADS))).astype(BF16)
    wut = w_in[:, o3:].T.astype(BF16)
    x1, z, xbc, dt, ut = _ffn_inproj_even(x2d, ffn1, gain, wz, wx, wdt, wut, layer, bsz, seq)

    pad_h = lambda v: jnp.pad(v.astype(F32), (0, LANES - SSD_HEADS))[None, :]
    expand = (jnp.arange(LANES)[:, None] == (jnp.arange(SSD_INNER)[None, :] // SSD_HEAD_DIM)).astype(BF16)
    ya = _ssd(z, xbc, dt, ssd_conv_w[i], ssd_conv_b[i][None, :], pad_h(ssd_dt_bias[i]),
              pad_h(ssd_a_log[i]), jnp.repeat(ssd_d[i], SSD_HEAD_DIM)[None, :], ssd_norm[i][None, :],
              expand, bsz, seq)

    q = min(S5_Q, seq)
    are, aim = s5_a_re[i], s5_a_im[i]
    dup = lambda v: jnp.concatenate([v, v], axis=-1)
    arow = jnp.stack([dup(are), dup(aim)], axis=1)
    ldt = s5_log_dt[i][:, None, None]
    bd = jnp.concatenate([jnp.swapaxes(s5_b_re[i], 1, 2), jnp.swapaxes(s5_b_im[i], 1, 2)], axis=-1)
    cd_re, cd_im = dup(s5_c_re[i]), dup(s5_c_im[i])
    ct_re, ct_im = jnp.swapaxes(cd_re, 1, 2), jnp.swapaxes(cd_im, 1, 2)
    d_rows = jnp.broadcast_to(s5_d[i][:, :, None], (S5_GROUPS, S5_GROUP, q))
    yb5 = _s5(ut, arow, ldt, bd, cd_re, cd_im, ct_re, ct_im, d_rows)

    w_out = even_w_out[i]
    return _outproj_even_ffn(x1, ya, yb5, s5_w_glu[i].T.astype(BF16), s5_b_glu[i][:, None],
                             w_out[:SSD_INNER].astype(BF16), w_out[SSD_INNER:].astype(BF16),
                             ffn2, layer, bsz, seq)


def _odd_layer(x2d, ffn1, ffn2, gain, layer, bsz, seq, odd_w_in, odd_w_out, attn_q_norm, attn_k_norm,
               attn_rel_bias, pool_w, pool_scale):
    i = layer // 2
    w_in = odd_w_in[i].astype(BF16)
    w = ATT_WIDTH
    tile_gain = lambda v: jnp.tile(v.astype(F32), ATT_HEADS)[None, :]
    head = jnp.arange(w) // ATT_HEAD_DIM
    bd = (head[:, None] == head[None, :]).astype(BF16)
    x1, qt, k, vt, yd = _ffn_inproj_odd(
        x2d, ffn1, gain, w_in[:, :w].T, w_in[:, w:2 * w], w_in[:, 2 * w:3 * w].T, w_in[:, 3 * w:],
        tile_gain(attn_q_norm[i]).T, tile_gain(attn_k_norm[i]), bd,
        pool_w[i].astype(BF16), pool_scale[i][None, :], layer, bsz, seq)
    yc = _attention(qt, k, vt, _bias_rows(attn_rel_bias[i], ATT_T), bsz, seq)
    w_out = odd_w_out[i]
    return _outproj_odd_ffn(x1, yc, yd, w_out[:w].astype(BF16), w_out[w:].astype(BF16), ffn2, layer)


def kernel(x, ffn1_norm, ffn1_w_gate, ffn1_w_up, ffn1_w_down, mix_norm, even_w_in, even_w_out, ssd_conv_w, ssd_conv_b, ssd_dt_bias, ssd_a_log, ssd_d, ssd_norm, s5_a_re, s5_a_im, s5_log_dt, s5_b_re, s5_b_im, s5_c_re, s5_c_im, s5_d, s5_w_glu, s5_b_glu, odd_w_in, odd_w_out, attn_q_norm, attn_k_norm, attn_rel_bias, pool_w, pool_scale, ffn2_norm, ffn2_w_gate, ffn2_w_up, ffn2_w_down):
    bsz, seq, d = x.shape
    depth = ffn1_norm.shape[0]
    x2d = x.reshape(bsz * seq, d)
    g3 = lambda v: v.astype(F32)[:, None, :]
    f1n, mxn, f2n = g3(ffn1_norm), g3(mix_norm), g3(ffn2_norm)
    ffn1 = (f1n, ffn1_w_gate.astype(BF16), ffn1_w_up.astype(BF16), ffn1_w_down.astype(BF16))
    ffn2 = (f2n, ffn2_w_gate.astype(BF16), ffn2_w_up.astype(BF16), ffn2_w_down.astype(BF16))
    for layer in range(depth):
        if layer % 2 == 0:
            x2d = _even_layer(x2d, ffn1, ffn2, mxn, layer, bsz, seq, even_w_in, even_w_out, ssd_conv_w,
                              ssd_conv_b, ssd_dt_bias, ssd_a_log, ssd_d, ssd_norm, s5_a_re, s5_a_im,
                              s5_log_dt, s5_b_re, s5_b_im, s5_c_re, s5_c_im, s5_d, s5_w_glu, s5_b_glu)
        else:
            x2d = _odd_layer(x2d, ffn1, ffn2, mxn, layer, bsz, seq, odd_w_in, odd_w_out, attn_q_norm,
                             attn_k_norm, attn_rel_bias, pool_w, pool_scale)
    return x2d.reshape(bsz, seq, d)
```

```python
import functools

import jax
import jax.numpy as jnp
from jax import lax
from jax.experimental import pallas as pl
from jax.experimental.pallas import tpu as pltpu

F32 = jnp.float32
BF16 = jnp.bfloat16
HIGHEST = lax.Precision.HIGHEST

EPS = 1e-6
LOG2E = 1.4426950408889634
MASKED_FLOOR = -1e30
LANES = 128
VMEM_LIMIT = 56 * 1024 * 1024

SSD_HEADS = 8
SSD_HEAD_DIM = 64
SSD_INNER = 512
SSD_GROUPS = 2
SSD_STATE = 128
SSD_CONV = 4
SSD_XBC = 1024
S5_WIDTH = 512
S5_GROUP = 16
S5_GROUPS = 32
S5_STATE = 64
ATT_HEADS = 8
ATT_HEAD_DIM = 64
ATT_WIDTH = 512
ATT_CHUNK = 64
LEFT_CHUNKS = 8
MAX_REL = 128
POOL_WINDOWS = (2, 4, 8, 16)
POOL_GROUP = 128
POOL_MAX = 16

FFN_TM = 512
FFN_TF = 256
SSD_Q = 128
S5_Q = 128
ATT_T = 256
ATT_LOOKAHEAD = 5


def _params(*sem):
    return pltpu.CompilerParams(dimension_semantics=sem, vmem_limit_bytes=VMEM_LIMIT)


def _dot(a, b):
    return jnp.dot(a, b, preferred_element_type=F32)


def _dot_exact(a, b):
    return jnp.dot(a, b, preferred_element_type=F32, precision=HIGHEST)


def _split3(x):
    hi = x.astype(BF16)
    rest = x - hi.astype(F32)
    mid = rest.astype(BF16)
    lo = (rest - mid.astype(F32)).astype(BF16)
    return hi, mid, lo


def _select_left(sel, x):
    hi, mid, lo = _split3(x)
    return _dot(sel, hi) + _dot(sel, mid) + _dot(sel, lo)


def _select_right(x, sel):
    hi, mid, lo = _split3(x)
    return _dot(hi, sel) + _dot(mid, sel) + _dot(lo, sel)


def _rmsnorm_bf16(x, gain):
    ms = jnp.mean(x * x, axis=-1, keepdims=True)
    return (x * lax.rsqrt(ms + EPS) * gain).astype(BF16)


def _silu(x):
    return x * jax.nn.sigmoid(x)


def _ffn_stage(i, wg_ref, wu_ref, wd_ref, wg_s, wu_s, wd_s):
    @pl.when(i < wg_s.shape[0])
    def _():
        wg_s[i] = wg_ref[...].astype(BF16)
        wu_s[i] = wu_ref[...].astype(BF16)
        wd_s[pl.ds(pl.multiple_of(i * FFN_TF, FFN_TF), FFN_TF), :] = wd_ref[...].astype(BF16)


def _ffn_apply(x, g_ref, wg_s, wu_s, wd_s, act_scr):
    h = _rmsnorm_bf16(x, g_ref[...])
    for c in range(wg_s.shape[0]):
        gate = _dot(h, wg_s[c])
        up = _dot(h, wu_s[c])
        act_scr[:, c * FFN_TF:(c + 1) * FFN_TF] = (_silu(gate) * up).astype(BF16)
    return x + 0.5 * _dot(act_scr[...], wd_s[...])


def _resident(shape):
    return pl.BlockSpec(shape, lambda i: (0,) * len(shape), pipeline_mode=pl.Buffered(1))


def _ffn_chunks(ffn):
    dff = ffn[1].shape[-1]
    assert dff % FFN_TF == 0
    return dff // FFN_TF


def _ffn_specs(gain, wg, wu, wd, layer):
    d, dff = wg.shape[1:]
    last = dff // FFN_TF - 1
    return [
        pl.BlockSpec((None, 1, d), lambda i: (layer, 0, 0)),
        pl.BlockSpec((None, d, FFN_TF), lambda i: (layer, 0, jnp.minimum(i, last))),
        pl.BlockSpec((None, d, FFN_TF), lambda i: (layer, 0, jnp.minimum(i, last))),
        pl.BlockSpec((None, FFN_TF, d), lambda i: (layer, jnp.minimum(i, last), 0)),
    ]


def _ffn_scratch(ffn, tm):
    d, dff = ffn[1].shape[1:]
    n = dff // FFN_TF
    return [pltpu.VMEM((n, d, FFN_TF), BF16), pltpu.VMEM((n, d, FFN_TF), BF16),
            pltpu.VMEM((dff, d), BF16), pltpu.VMEM((tm, dff), BF16)]


def _ffn_inproj_even_kernel(x_ref, fg_ref, wg_ref, wu_ref, wd_ref, g_ref, wz_ref, wx_ref, wdt_ref,
                            wut_ref, x1_ref, z_ref, xbc_ref, dt_ref, ut_ref, wg_s, wu_s, wd_s, act_scr):
    i = pl.program_id(0)
    _ffn_stage(i, wg_ref, wu_ref, wd_ref, wg_s, wu_s, wd_s)

    @pl.when(i >= wg_s.shape[0])
    def _():
        x1 = _ffn_apply(x_ref[...], fg_ref, wg_s, wu_s, wd_s, act_scr)
        x1_ref[...] = x1
        h = _rmsnorm_bf16(x1, g_ref[...])
        z_ref[...] = _dot(h, wz_ref[...])
        xbc_ref[...] = _dot(h, wx_ref[...])
        dt_ref[...] = _dot(h, wdt_ref[...])
        ut = lax.dot_general(wut_ref[...], h, (((1,), (1,)), ((), ())), preferred_element_type=F32)
        q = ut_ref.shape[-1]
        for cc in range(ut_ref.shape[2]):
            ut_ref[0, :, cc] = ut[:, cc * q:(cc + 1) * q].reshape(S5_GROUPS, S5_GROUP, q)


def _ffn_inproj_even(x2d, ffn, gain, wz, wx, wdt, wut, layer, bsz, seq):
    m, d = x2d.shape
    tm = min(FFN_TM, seq)
    per_b = seq // tm
    q = min(S5_Q, seq)
    ns = _ffn_chunks(ffn)
    tile = lambda i: jnp.maximum(i - ns, 0)
    return pl.pallas_call(
        _ffn_inproj_even_kernel,
        out_shape=(
            jax.ShapeDtypeStruct((m, d), F32),
            jax.ShapeDtypeStruct((m, SSD_INNER), F32),
            jax.ShapeDtypeStruct((m, SSD_XBC), F32),
            jax.ShapeDtypeStruct((m, LANES), F32),
            jax.ShapeDtypeStruct((bsz, S5_GROUPS, seq // q, S5_GROUP, q), F32),
        ),
        grid=(ns + m // tm,),
        in_specs=[pl.BlockSpec((tm, d), lambda i: (tile(i), 0))] + _ffn_specs(*ffn, layer) + [
            pl.BlockSpec((None, 1, d), lambda i: (layer, 0, 0)),
            _resident(wz.shape), _resident(wx.shape), _resident(wdt.shape), _resident(wut.shape),
        ],
        out_specs=(
            pl.BlockSpec((tm, d), lambda i: (tile(i), 0)),
            pl.BlockSpec((tm, SSD_INNER), lambda i: (tile(i), 0)),
            pl.BlockSpec((tm, SSD_XBC), lambda i: (tile(i), 0)),
            pl.BlockSpec((tm, LANES), lambda i: (tile(i), 0)),
            pl.BlockSpec((1, S5_GROUPS, tm // q, S5_GROUP, q),
                         lambda i: (tile(i) // per_b, 0, tile(i) % per_b, 0, 0)),
        ),
        scratch_shapes=_ffn_scratch(ffn, tm),
        compiler_params=_params("arbitrary"),
        name="ffn_inproj_even",
    )(x2d, *ffn, gain, wz, wx, wdt, wut)


def _shift_rows(cur, prev8, j):
    rolled = pltpu.roll(cur, j, 0)
    prev_rolled = pltpu.roll(prev8, j, 0)
    row = lax.broadcasted_iota(jnp.int32, prev8.shape, 0)
    top = jnp.where(row < j, prev_rolled, rolled[:8])
    return jnp.concatenate([top, rolled[8:]], axis=0)


def _ssd_kernel(z_ref, xbc_ref, xprev_ref, dt_ref, cw_ref, cb_ref, dtb_ref, alog_ref,
                dx_ref, ng_ref, e_ref, o_ref, h_scr):
    t = pl.program_id(1)
    q = z_ref.shape[0]
    gw = SSD_INNER // SSD_GROUPS
    hpg = SSD_HEADS // SSD_GROUPS

    @pl.when(t == 0)
    def _():
        h_scr[...] = jnp.zeros_like(h_scr)

    xbc = xbc_ref[...]
    prev = jnp.where(t > 0, xprev_ref[...], 0.0)
    conv = xbc * cw_ref[SSD_CONV - 1:SSD_CONV, :] + cb_ref[...]
    for j in range(1, SSD_CONV):
        conv = conv + _shift_rows(xbc, prev, j) * cw_ref[SSD_CONV - 1 - j:SSD_CONV - j, :]
    xc = _silu(conv)
    xs = xc[:, :SSD_INNER]

    dtr = dt_ref[...] + dtb_ref[...]
    dt = jnp.maximum(dtr, 0.0) + jnp.log1p(jnp.exp(-jnp.abs(dtr)))
    a = -jnp.exp(alog_ref[...])
    row = lax.broadcasted_iota(jnp.int32, (q, q), 0)
    col = lax.broadcasted_iota(jnp.int32, (q, q), 1)
    causal = row >= col
    tril = jnp.where(causal, 1.0, 0.0).astype(BF16)
    acs = _select_left(tril, dt * a)
    acs_t = acs.T
    expanded = _select_right(jnp.concatenate([acs, dt], axis=0), e_ref[...])
    acs_x = expanded[:q]
    dt_x = expanded[q:]
    xdt = xs * dt_x
    decay_in = jnp.exp(acs_x)
    acs_end = acs_x[q - 1:q, :]
    to_end = jnp.exp(acs_end - acs_x)
    chunk_decay = jnp.exp(acs_end)
    xdt_end = xdt * to_end

    lane_head = lax.broadcasted_iota(jnp.int32, (1, gw), 1) // SSD_HEAD_DIM
    ys = []
    for g in range(SSD_GROUPS):
        bm = xc[:, SSD_INNER + g * SSD_STATE:SSD_INNER + (g + 1) * SSD_STATE].astype(BF16)
        cm = xc[:, SSD_INNER + (SSD_GROUPS + g) * SSD_STATE:
                SSD_INNER + (SSD_GROUPS + g + 1) * SSD_STATE].astype(BF16)
        cb = lax.dot_general(cm, bm, (((1,), (1,)), ((), ())), preferred_element_type=F32)
        xdt_g = xdt[:, g * gw:(g + 1) * gw]
        gmats, rhs = [], []
        for j in range(hpg):
            hh = g * hpg + j
            seg = acs[:, hh:hh + 1] - acs_t[hh:hh + 1, :]
            decay = jnp.exp(jnp.where(causal, seg, -jnp.inf))
            gmats.append((cb * decay).astype(BF16))
            rhs.append(jnp.where(lane_head == j, xdt_g, 0.0).astype(BF16))
        y_diag = _dot(jnp.concatenate(gmats, axis=1), jnp.concatenate(rhs, axis=0))
        h_prev = h_scr[g]
        y_off = _dot(cm, h_prev.astype(BF16)) * decay_in[:, g * gw:(g + 1) * gw]
        upd = lax.dot_general(bm, xdt_end[:, g * gw:(g + 1) * gw].astype(BF16),
                              (((0,), (0,)), ((), ())), preferred_element_type=F32)
        h_scr[g] = chunk_decay[:, g * gw:(g + 1) * gw] * h_prev + upd
        ys.append(y_diag + y_off)
    y = jnp.concatenate(ys, axis=1) + dx_ref[...] * xs
    y = y * _silu(z_ref[...])
    outs = []
    for g in range(SSD_GROUPS):
        yg = y[:, g * gw:(g + 1) * gw]
        outs.append(yg * lax.rsqrt(jnp.mean(yg * yg, axis=-1, keepdims=True) + EPS))
    o_ref[...] = (jnp.concatenate(outs, axis=1) * ng_ref[...]).astype(o_ref.dtype)


def _ssd(z, xbc, dt, conv_w, conv_b, dt_bias, a_log, d_x, norm_gain, expand, bsz, seq):
    m = z.shape[0]
    q = min(SSD_Q, seq)
    nq = seq // q
    const = lambda b, t: (0, 0)
    return pl.pallas_call(
        _ssd_kernel,
        out_shape=jax.ShapeDtypeStruct((m, SSD_INNER), BF16),
        grid=(bsz, nq),
        in_specs=[
            pl.BlockSpec((q, SSD_INNER), lambda b, t: (b * nq + t, 0)),
            pl.BlockSpec((q, SSD_XBC), lambda b, t: (b * nq + t, 0)),
            pl.BlockSpec((8, SSD_XBC), lambda b, t: (jnp.maximum((b * nq + t) * (q // 8) - 1, 0), 0)),
            pl.BlockSpec((q, LANES), lambda b, t: (b * nq + t, 0)),
            pl.BlockSpec(conv_w.shape, const),
            pl.BlockSpec(conv_b.shape, const),
            pl.BlockSpec(dt_bias.shape, const),
            pl.BlockSpec(a_log.shape, const),
            pl.BlockSpec(d_x.shape, const),
            pl.BlockSpec(norm_gain.shape, const),
            pl.BlockSpec(expand.shape, const),
        ],
        out_specs=pl.BlockSpec((q, SSD_INNER), lambda b, t: (b * nq + t, 0)),
        scratch_shapes=[pltpu.VMEM((SSD_GROUPS, SSD_STATE, SSD_INNER // SSD_GROUPS), F32)],
        compiler_params=_params("parallel", "arbitrary"),
        name="ssd",
    )(z, xbc, xbc, dt, conv_w, conv_b, dt_bias, a_log, d_x, norm_gain, expand)


def _cpow(base_re, base_im, exponent, nbits):
    shape = jnp.broadcast_shapes(base_re.shape, exponent.shape)
    pr = jnp.ones(shape, F32)
    pi = jnp.zeros(shape, F32)
    sr, si = base_re, base_im
    for k in range(nbits):
        bit = ((exponent >> k) & 1) == 1
        nr = pr * sr - pi * si
        ni = pr * si + pi * sr
        pr = jnp.where(bit, nr, pr)
        pi = jnp.where(bit, ni, pi)
        if k + 1 < nbits:
            sr, si = sr * sr - si * si, 2.0 * sr * si
    return pr, pi


def _s5_discretize(ar, ai, log_dt):
    dt = jnp.exp(log_dt)
    mag = jnp.exp(dt * ar)
    lam_re = mag * jnp.cos(dt * ai)
    lam_im = mag * jnp.sin(dt * ai)
    den = ar * ar + ai * ai
    f_re = ((lam_re - 1.0) * ar + lam_im * ai) / den
    f_im = (lam_im * ar - (lam_re - 1.0) * ai) / den
    return lam_re, lam_im, f_re, f_im


def _s5_kernel(u_ref, arow_ref, ldt_ref, bd_ref, cd_re_ref, cd_im_ref,
               ct_re_ref, ct_im_ref, d_ref, y_ref, kmat_scr, t_scr, f_scr):
    bsz, rows, q = u_ref.shape
    gsz = S5_GROUP
    nc = rows // gsz
    m = bsz * nc
    p2 = 2 * S5_STATE
    nbits = (q - 1).bit_length()
    log_dt = ldt_ref[...]

    lane = lax.broadcasted_iota(jnp.int32, (1, p2), 1)
    first_half_l = lane < S5_STATE
    rowp = lax.broadcasted_iota(jnp.int32, (p2, 1), 0)
    first_half_r = rowp < S5_STATE

    ar_r = arow_ref[0:1, :]
    ai_r = arow_ref[1:2, :]
    lr, li, fr, fi = _s5_discretize(ar_r, ai_r, log_dt)
    sign_b = jnp.where(first_half_l, -1.0, 1.0)
    bd1 = bd_ref[...]
    bd2 = pltpu.roll(bd1, S5_STATE, 1) * sign_b
    bb_a = fr * bd1 + fi * bd2
    bb_b = fr * bd2 - fi * bd1

    srow = lax.broadcasted_iota(jnp.int32, (q, 1), 0)
    pr_re, pr_im = _cpow(lr, li, (q - 1) - srow, nbits)
    for i in range(gsz):
        es = pr_re * bb_a[i:i + 1, :] + pr_im * bb_b[i:i + 1, :]
        t_scr[i * q:(i + 1) * q, gsz * q:gsz * q + p2] = es.astype(BF16)

    eye = (lax.broadcasted_iota(jnp.int32, (p2, p2), 0)
           == lax.broadcasted_iota(jnp.int32, (p2, p2), 1))
    lcr = jnp.sum(jnp.where(eye, lr, 0.0), axis=1, keepdims=True)
    lci = jnp.sum(jnp.where(eye, li, 0.0), axis=1, keepdims=True)
    dlane = lax.broadcasted_iota(jnp.int32, (1, q), 1)
    pw_re, pw_im = _cpow(lcr, lci, dlane, nbits)
    p1_re = pw_re * lcr - pw_im * lci
    p1_im = pw_re * lci + pw_im * lcr
    pow_stack = jnp.where(first_half_r, pw_re, pw_im)

    v1 = bb_a * jnp.where(first_half_l, 1.0, -1.0)
    v2 = -pltpu.roll(bb_a, S5_STATE, 1)
    cre = cd_re_ref[...]
    cim = cd_im_ref[...]
    coef = cre[:, None, :] * v1[None, :, :] + cim[:, None, :] * v2[None, :, :]
    kmat_scr[...] = _dot_exact(coef.reshape(gsz * gsz, p2), pow_stack)

    for o in range(gsz):
        c_r = ct_re_ref[:, o:o + 1]
        c_i = ct_im_ref[:, o:o + 1]
        f_o = jnp.where(first_half_r, c_r * p1_re - c_i * p1_im, -(c_r * p1_im + c_i * p1_re))
        f_scr[:, o * q:(o + 1) * q] = f_o.astype(BF16)

    trow = lax.broadcasted_iota(jnp.int32, (q, q), 0)
    tcol = lax.broadcasted_iota(jnp.int32, (q, q), 1)
    lower = tcol >= trow

    def build(i, carry):
        for o in range(gsz):
            k_row = kmat_scr[pl.ds(o * gsz + i, 1), :]
            blk = pltpu.roll(jnp.broadcast_to(k_row, (q, q)), 0, 1, stride=1, stride_axis=0)
            blk = jnp.where(lower, blk, 0.0).astype(BF16)
            t_scr[pl.ds(pl.multiple_of(i * q, q), q), o * q:(o + 1) * q] = blk
        return carry

    lax.fori_loop(0, gsz, build, 0)

    u_f32 = [jnp.concatenate([u_ref[b, pl.ds(i, nc, stride=gsz), :] for b in range(bsz)], axis=0)
             for i in range(gsz)]
    u_cat = jnp.concatenate([v.astype(BF16) for v in u_f32], axis=1)
    acc = _dot(u_cat, t_scr[...])
    x = acc[:, gsz * q:]

    sq_re, sq_im = lr, li
    for _ in range(q.bit_length() - 1):
        sq_re, sq_im = sq_re * sq_re - sq_im * sq_im, 2.0 * sq_re * sq_im
    crow = lax.broadcasted_iota(jnp.int32, (m, 1), 0) % nc
    k = 1
    while k < nc:
        sh = jnp.where(crow >= k, pltpu.roll(x, k, 0), 0.0)
        x = x + sq_re * sh + (sq_im * sign_b) * pltpu.roll(sh, S5_STATE, 1)
        sq_re, sq_im = sq_re * sq_re - sq_im * sq_im, 2.0 * sq_re * sq_im
        k *= 2
    h_prev = jnp.where(crow >= 1, pltpu.roll(x, 1, 0), 0.0)
    y = acc[:, :gsz * q] + _dot(h_prev.astype(BF16), f_scr[...])
    for o in range(gsz):
        y_o = y[:, o * q:(o + 1) * q] + d_ref[o:o + 1, :] * u_f32[o]
        for b in range(bsz):
            y_ref[b, pl.ds(o, nc, stride=gsz), :] = y_o[b * nc:(b + 1) * nc]


def _s5(u5, arow, ldt, bd, cd_re, cd_im, ct_re, ct_im, d_rows):
    bsz, _, nc, _, q = u5.shape
    rows = nc * S5_GROUP
    p2 = 2 * S5_STATE
    per_g = lambda g: (g, 0, 0)
    y4 = pl.pallas_call(
        _s5_kernel,
        out_shape=jax.ShapeDtypeStruct((bsz, S5_GROUPS, rows, q), F32),
        grid=(S5_GROUPS,),
        in_specs=[
            pl.BlockSpec((bsz, None, rows, q), lambda g: (0, g, 0, 0)),
            pl.BlockSpec((None, 2, p2), per_g),
            pl.BlockSpec((None, 1, 1), per_g),
            pl.BlockSpec((None, S5_GROUP, p2), per_g),
            pl.BlockSpec((None, S5_GROUP, p2), per_g),
            pl.BlockSpec((None, S5_GROUP, p2), per_g),
            pl.BlockSpec((None, p2, S5_GROUP), per_g),
            pl.BlockSpec((None, p2, S5_GROUP), per_g),
            pl.BlockSpec((None, S5_GROUP, q), per_g),
        ],
        out_specs=pl.BlockSpec((bsz, None, rows, q), lambda g: (0, g, 0, 0)),
        scratch_shapes=[
            pltpu.VMEM((S5_GROUP * S5_GROUP, q), F32),
            pltpu.VMEM((S5_GROUP * q, S5_GROUP * q + p2), BF16),
            pltpu.VMEM((p2, S5_GROUP * q), BF16),
        ],
        compiler_params=_params("parallel"),
        name="s5",
    )(u5.reshape(bsz, S5_GROUPS, rows, q), arow, ldt, bd, cd_re, cd_im, ct_re, ct_im, d_rows)
    return y4.reshape(u5.shape)


def _gelu_tanh(x):
    return 0.5 * x * (1.0 + jnp.tanh(0.7978845608028654 * (x + 0.044715 * (x * x * x))))


def _outproj_even_ffn_kernel(x_ref, ya_ref, ybt_ref, wglut_ref, bglu_ref, wa_ref, wb_ref,
                             fg_ref, wg_ref, wu_ref, wd_ref, o_ref, wg_s, wu_s, wd_s, act_scr):
    i = pl.program_id(0)
    _ffn_stage(i, wg_ref, wu_ref, wd_ref, wg_s, wu_s, wd_s)

    @pl.when(i >= wg_s.shape[0])
    def _():
        q = ybt_ref.shape[-1]
        ybt = jnp.concatenate([ybt_ref[0, :, cc].reshape(S5_WIDTH, q) for cc in range(ybt_ref.shape[2])],
                              axis=1)
        gate = _dot(wglut_ref[...], _gelu_tanh(ybt).astype(BF16)) + bglu_ref[...]
        yb = (ybt * jax.nn.sigmoid(gate)).T.astype(BF16)
        x2 = x_ref[...] + _dot(ya_ref[...], wa_ref[...]) + _dot(yb, wb_ref[...])
        o_ref[...] = _ffn_apply(x2, fg_ref, wg_s, wu_s, wd_s, act_scr)


def _outproj_even_ffn(x2d, ya, yb5, wglut, bglu, wa, wb, ffn, layer, bsz, seq):
    m, d = x2d.shape
    tm = min(FFN_TM, seq)
    per_b = seq // tm
    q = yb5.shape[-1]
    ns = _ffn_chunks(ffn)
    tile = lambda i: jnp.maximum(i - ns, 0)
    return pl.pallas_call(
        _outproj_even_ffn_kernel,
        out_shape=jax.ShapeDtypeStruct((m, d), F32),
        grid=(ns + m // tm,),
        in_specs=[
            pl.BlockSpec((tm, d), lambda i: (tile(i), 0)),
            pl.BlockSpec((tm, SSD_INNER), lambda i: (tile(i), 0)),
            pl.BlockSpec((1, S5_GROUPS, tm // q, S5_GROUP, q),
                         lambda i: (tile(i) // per_b, 0, tile(i) % per_b, 0, 0)),
            _resident(wglut.shape), _resident(bglu.shape), _resident(wa.shape), _resident(wb.shape),
        ] + _ffn_specs(*ffn, layer),
        out_specs=pl.BlockSpec((tm, d), lambda i: (tile(i), 0)),
        scratch_shapes=_ffn_scratch(ffn, tm),
        compiler_params=_params("arbitrary"),
        name="outproj_even_ffn",
    )(x2d, ya, yb5, wglut, bglu, wa, wb, *ffn)


def _head_rmsnorm(x, gain):
    assert LANES == 2 * ATT_HEAD_DIM
    first = lax.broadcasted_iota(jnp.int32, (1, LANES), 1) < ATT_HEAD_DIM
    tiles = []
    for p in range(x.shape[1] // LANES):
        xp = x[:, p * LANES:(p + 1) * LANES]
        sq = xp * xp
        s0 = jnp.sum(jnp.where(first, sq, 0.0), axis=-1, keepdims=True)
        s1 = jnp.sum(jnp.where(first, 0.0, sq), axis=-1, keepdims=True)
        ss = jnp.where(first, s0, s1)
        tiles.append(xp * lax.rsqrt(ss * (1.0 / ATT_HEAD_DIM) + EPS))
    return jnp.concatenate(tiles, axis=1) * gain


def _head_rmsnorm_t(xt, gain_col):
    c, n = xt.shape
    x3 = xt.reshape(c // ATT_HEAD_DIM, ATT_HEAD_DIM, n)
    ms = jnp.mean(x3 * x3, axis=1, keepdims=True)
    return (x3 * lax.rsqrt(ms + EPS)).reshape(c, n) * gain_col


def _pool_apply(u, prev, pos, w_ref, sc_ref):
    ext = jnp.concatenate([prev, u], axis=0)
    sums = {}
    s = ext
    w = 1
    while w < POOL_MAX:
        s = s + pltpu.roll(s, w, 0)
        w *= 2
        sums[w] = s[POOL_MAX:]
    outs = []
    for g, win in enumerate(POOL_WINDOWS):
        sl = slice(g * POOL_GROUP, (g + 1) * POOL_GROUP)
        count = jnp.minimum(pos + 1, win).astype(F32)
        pooled = sums[win][:, sl] / count - u[:, sl]
        outs.append(_dot(pooled.astype(BF16), w_ref[g]))
    return jnp.concatenate(outs, axis=1) * sc_ref[...]


def _ffn_inproj_odd_kernel(x_ref, fg_ref, wg_ref, wu_ref, wd_ref, g_ref, wqt_ref, wk_ref, wvt_ref,
                           wup_ref, qg_ref, kg_ref, pw_ref, ps_ref,
                           x1_ref, qt_ref, k_ref, vt_ref, yd_ref, wg_s, wu_s, wd_s, act_scr, carry_scr,
                           *, tiles_per_seq):
    i = pl.program_id(0)
    tm = x_ref.shape[0]
    n_stage = wg_s.shape[0]
    _ffn_stage(i, wg_ref, wu_ref, wd_ref, wg_s, wu_s, wd_s)

    @pl.when(i == 0)
    def _():
        carry_scr[...] = jnp.zeros_like(carry_scr)

    @pl.when(i >= n_stage)
    def _():
        x1 = _ffn_apply(x_ref[...], fg_ref, wg_s, wu_s, wd_s, act_scr)
        x1_ref[...] = x1
        h = _rmsnorm_bf16(x1, g_ref[...])
        nt = (((1,), (1,)), ((), ()))
        qt = lax.dot_general(wqt_ref[...], h, nt, preferred_element_type=F32)
        qt = _head_rmsnorm_t(qt, qg_ref[...]) * (ATT_HEAD_DIM ** -0.5 * LOG2E)
        qt_ref[0] = qt.astype(BF16)
        k_ref[...] = _head_rmsnorm(_dot(h, wk_ref[...]), kg_ref[...]).astype(BF16)
        vt_ref[0] = lax.dot_general(wvt_ref[...], h, nt, preferred_element_type=F32).astype(BF16)
        u = _dot(h, wup_ref[...])
        tile_in_seq = (i - n_stage) % tiles_per_seq
        prev = jnp.where(tile_in_seq == 0, 0.0, carry_scr[...])
        pos = tile_in_seq * tm + lax.broadcasted_iota(jnp.int32, (tm, 1), 0)
        yd_ref[...] = _pool_apply(u, prev, pos, pw_ref, ps_ref).astype(yd_ref.dtype)
        carry_scr[...] = u[tm - POOL_MAX:]


def _ffn_inproj_odd(x2d, ffn, gain, wqt, wk, wvt, wup, qg_col, kg, w_pool, pool_scale, layer, bsz, seq):
    m, d = x2d.shape
    tm = min(FFN_TM, seq)
    per_b = seq // tm
    w = ATT_WIDTH
    ns = _ffn_chunks(ffn)
    tl = lambda i: jnp.maximum(i - ns, 0)
    tile = pl.BlockSpec((tm, w), lambda i: (tl(i), 0))
    tile_t = pl.BlockSpec((1, w, tm), lambda i: (tl(i) // per_b, 0, tl(i) % per_b))
    row_tile = pl.BlockSpec((tm, d), lambda i: (tl(i), 0))
    tok = jax.ShapeDtypeStruct((m, w), BF16)
    chan = jax.ShapeDtypeStruct((bsz, w, seq), BF16)
    return pl.pallas_call(
        functools.partial(_ffn_inproj_odd_kernel, tiles_per_seq=per_b),
        out_shape=(jax.ShapeDtypeStruct((m, d), F32), chan, tok, chan, tok),
        grid=(ns + m // tm,),
        in_specs=[row_tile] + _ffn_specs(*ffn, layer) + [
            pl.BlockSpec((None, 1, d), lambda i: (layer, 0, 0)),
            _resident(wqt.shape), _resident(wk.shape), _resident(wvt.shape), _resident(wup.shape),
            _resident(qg_col.shape), _resident(kg.shape),
            _resident(w_pool.shape), _resident(pool_scale.shape),
        ],
        out_specs=(row_tile, tile_t, tile, tile_t, tile),
        scratch_shapes=_ffn_scratch(ffn, tm) + [pltpu.VMEM((POOL_MAX, w), F32)],
        compiler_params=_params("arbitrary"),
        name="ffn_inproj_odd",
    )(x2d, *ffn, gain, wqt, wk, wvt, wup, qg_col, kg, w_pool, pool_scale)


def _attn_kernel(qt_ref, k0_ref, k1_ref, k2_ref, vt0_ref, vt1_ref, vt2_ref, brow_ref, o_ref, bias_ref):
    t = pl.program_id(1)
    tq = o_ref.shape[0]
    nk = 3 * tq

    @pl.when((pl.program_id(0) == 0) & (t == 0))
    def _():
        keyi = lax.broadcasted_iota(jnp.int32, (nk, tq), 0)
        qryi = lax.broadcasted_iota(jnp.int32, (nk, tq), 1)
        rel_chunk = keyi // ATT_CHUNK - (2 * tq // ATT_CHUNK - LEFT_CHUNKS) - qryi // ATT_CHUNK
        rel_chunk = jnp.where(rel_chunk >= 0, rel_chunk, LEFT_CHUNKS + 1)
        for hh in range(ATT_HEADS):
            base = jnp.broadcast_to(brow_ref[hh:hh + 1, :], (nk, 4 * tq))
            shifted = pltpu.roll(base, 0, 1, stride=1, stride_axis=0)
            bias = shifted[:, nk:] * LOG2E
            for var in range(3):
                first_key = (2 - var) * tq
                keep = jnp.where(keyi >= first_key, rel_chunk, LEFT_CHUNKS + 1) <= LEFT_CHUNKS
                bias_ref[var, hh] = jnp.where(keep, bias, -jnp.inf)

    var = jnp.minimum(t, 2)
    qt = qt_ref[0]
    kcat = jnp.concatenate([k0_ref[...], k1_ref[...], k2_ref[...]], axis=0)
    vtcat = jnp.concatenate([vt0_ref[0], vt1_ref[0], vt2_ref[0]], axis=1)
    hd = ATT_HEAD_DIM
    heads_per_slab = LANES // hd
    no_q = jnp.zeros((LANES - hd, tq), qt.dtype)
    ones_rows = jnp.ones((16, nk), qt.dtype)
    n_tiles = nk // tq

    def scores(head, j):
        p, hh = divmod(head, heads_per_slab)
        rows = slice(head * hd, (head + 1) * hd)
        qth = jnp.concatenate([qt[rows]] + [no_q] if hh == 0 else [no_q] + [qt[rows]], axis=0)
        ks = slice(j * tq, (j + 1) * tq)
        return _dot(kcat[ks, p * LANES:(p + 1) * LANES], qth) + bias_ref[var, head, ks, :]

    items = [(head, j) for head in range(ATT_HEADS) for j in range(n_tiles)]
    ahead = [scores(*it) for it in items[:ATT_LOOKAHEAD]]
    outs, parts, tops = [], [], []
    for idx, (head, j) in enumerate(items):
        st = ahead.pop(0)
        if idx + ATT_LOOKAHEAD < len(items):
            ahead.append(scores(*items[idx + ATT_LOOKAHEAD]))
        rows = slice(head * hd, (head + 1) * hd)
        ks = slice(j * tq, (j + 1) * tq)
        top = jnp.maximum(jnp.max(st, axis=0, keepdims=True), MASKED_FLOOR)
        e = jnp.exp2(st - top).astype(BF16)
        vth = jnp.concatenate([vtcat[rows, ks], ones_rows[:, ks]], axis=0)
        parts.append(_dot(vth, e))
        tops.append(top)
        if j == n_tiles - 1:
            top_all = functools.reduce(jnp.maximum, tops)
            ot = sum(part * jnp.exp2(tp - top_all) for part, tp in zip(parts, tops))
            outs.append(ot[:hd] / ot[hd:hd + 1])
            parts, tops = [], []
            if head % heads_per_slab == heads_per_slab - 1:
                p = head // heads_per_slab
                o_ref[:, p * LANES:(p + 1) * LANES] = jnp.concatenate(outs, axis=0).T.astype(o_ref.dtype)
                outs = []


def _attention(qt, k, vt, brow, bsz, seq):
    m, w = k.shape
    tq = ATT_T
    assert 2 * tq >= LEFT_CHUNKS * ATT_CHUNK and tq > MAX_REL and seq % tq == 0
    nt = seq // tq
    back = lambda n: (lambda b, t: (b * nt + jnp.maximum(t - n, 0), 0))
    back_t = lambda n: (lambda b, t: (b, 0, jnp.maximum(t - n, 0)))
    tok = lambda f: pl.BlockSpec((tq, w), f)
    chan = lambda f: pl.BlockSpec((1, w, tq), f)
    return pl.pallas_call(
        _attn_kernel,
        out_shape=jax.ShapeDtypeStruct((m, w), BF16),
        grid=(bsz, nt),
        in_specs=[chan(back_t(0)), tok(back(2)), tok(back(1)), tok(back(0)),
                  chan(back_t(2)), chan(back_t(1)), chan(back_t(0)),
                  pl.BlockSpec(brow.shape, lambda b, t: (0, 0))],
        out_specs=tok(back(0)),
        scratch_shapes=[pltpu.VMEM((3, ATT_HEADS, 3 * tq, tq), F32)],
        compiler_params=_params("arbitrary", "arbitrary"),
        name="band_attention",
    )(qt, k, k, k, vt, vt, vt, brow)


def _bias_rows(rel_bias, tq):
    rb = rel_bias.astype(F32)
    nh = rb.shape[0]
    near = jnp.broadcast_to(rb[:, :1], (nh, tq - MAX_REL))
    far = jnp.broadcast_to(rb[:, -1:], (nh, 3 * tq - MAX_REL - 1))
    return jnp.concatenate([near, rb, far], axis=1)


def _outproj_odd_ffn_kernel(x_ref, yc_ref, yd_ref, wa_ref, wb_ref, fg_ref, wg_ref, wu_ref, wd_ref,
                            o_ref, wg_s, wu_s, wd_s, act_scr):
    i = pl.program_id(0)
    _ffn_stage(i, wg_ref, wu_ref, wd_ref, wg_s, wu_s, wd_s)

    @pl.when(i >= wg_s.shape[0])
    def _():
        x2 = x_ref[...] + _dot(yc_ref[...], wa_ref[...]) + _dot(yd_ref[...], wb_ref[...])
        o_ref[...] = _ffn_apply(x2, fg_ref, wg_s, wu_s, wd_s, act_scr)


def _outproj_odd_ffn(x2d, yc, yd, wa, wb, ffn, layer):
    m, d = x2d.shape
    tm = min(FFN_TM, m)
    ns = _ffn_chunks(ffn)
    tile = lambda i: jnp.maximum(i - ns, 0)
    return pl.pallas_call(
        _outproj_odd_ffn_kernel,
        out_shape=jax.ShapeDtypeStruct((m, d), F32),
        grid=(ns + m // tm,),
        in_specs=[
            pl.BlockSpec((tm, d), lambda i: (tile(i), 0)),
            pl.BlockSpec((tm, ATT_WIDTH), lambda i: (tile(i), 0)),
            pl.BlockSpec((tm, ATT_WIDTH), lambda i: (tile(i), 0)),
            _resident(wa.shape), _resident(wb.shape),
        ] + _ffn_specs(*ffn, layer),
        out_specs=pl.BlockSpec((tm, d), lambda i: (tile(i), 0)),
        scratch_shapes=_ffn_scratch(ffn, tm),
        compiler_params=_params("arbitrary"),
        name="outproj_odd_ffn",
    )(x2d, yc, yd, wa, wb, *ffn)


def _even_layer(x2d, ffn1, ffn2, gain, layer, bsz, seq, even_w_in, even_w_out, ssd_conv_w, ssd_conv_b,
                ssd_dt_bias, ssd_a_log, ssd_d, ssd_norm, s5_a_re, s5_a_im, s5_log_dt, s5_b_re, s5_b_im,
                s5_c_re, s5_c_im, s5_d, s5_w_glu, s5_b_glu):
    i = layer // 2
    w_in = even_w_in[i]
    o1 = SSD_INNER
    o2 = o1 + SSD_XBC
    o3 = o2 + SSD_HEADS
    wz = w_in[:, :o1].astype(BF16)
    wx = w_in[:, o1:o2].astype(BF16)
    wdt = jnp.pad(w_in[:, o2:o3], ((0, 0), (0, LANES - SSD_HEADS))).astype(BF16)
    wut = w_in[:, o3:].T.astype(BF16)
    x1, z, xbc, dt, ut = _ffn_inproj_even(x2d, ffn1, gain, wz, wx, wdt, wut, layer, bsz, seq)

    pad_h = lambda v: jnp.pad(v.astype(F32), (0, LANES - SSD_HEADS))[None, :]
    expand = (jnp.arange(LANES)[:, None] == (jnp.arange(SSD_INNER)[None, :] // SSD_HEAD_DIM)).astype(BF16)
    ya = _ssd(z, xbc, dt, ssd_conv_w[i], ssd_conv_b[i][None, :], pad_h(ssd_dt_bias[i]),
              pad_h(ssd_a_log[i]), jnp.repeat(ssd_d[i], SSD_HEAD_DIM)[None, :], ssd_norm[i][None, :],
              expand, bsz, seq)

    q = min(S5_Q, seq)
    are, aim = s5_a_re[i], s5_a_im[i]
    dup = lambda v: jnp.concatenate([v, v], axis=-1)
    arow = jnp.stack([dup(are), dup(aim)], axis=1)
    ldt = s5_log_dt[i][:, None, None]
    bd = jnp.concatenate([jnp.swapaxes(s5_b_re[i], 1, 2), jnp.swapaxes(s5_b_im[i], 1, 2)], axis=-1)
    cd_re, cd_im = dup(s5_c_re[i]), dup(s5_c_im[i])
    ct_re, ct_im = jnp.swapaxes(cd_re, 1, 2), jnp.swapaxes(cd_im, 1, 2)
    d_rows = jnp.broadcast_to(s5_d[i][:, :, None], (S5_GROUPS, S5_GROUP, q))
    yb5 = _s5(ut, arow, ldt, bd, cd_re, cd_im, ct_re, ct_im, d_rows)

    w_out = even_w_out[i]
    return _outproj_even_ffn(x1, ya, yb5, s5_w_glu[i].T.astype(BF16), s5_b_glu[i][:, None],
                             w_out[:SSD_INNER].astype(BF16), w_out[SSD_INNER:].astype(BF16),
                             ffn2, layer, bsz, seq)


def _odd_layer(x2d, ffn1, ffn2, gain, layer, bsz, seq, odd_w_in, odd_w_out, attn_q_norm, attn_k_norm,
               attn_rel_bias, pool_w, pool_scale):
    i = layer // 2
    w_in = odd_w_in[i].astype(BF16)
    w = ATT_WIDTH
    tile_gain = lambda v: jnp.tile(v.astype(F32), ATT_HEADS)[None, :]
    x1, qt, k, vt, yd = _ffn_inproj_odd(
        x2d, ffn1, gain, w_in[:, :w].T, w_in[:, w:2 * w], w_in[:, 2 * w:3 * w].T, w_in[:, 3 * w:],
        tile_gain(attn_q_norm[i]).T, tile_gain(attn_k_norm[i]),
        pool_w[i].astype(BF16), pool_scale[i][None, :], layer, bsz, seq)
    yc = _attention(qt, k, vt, _bias_rows(attn_rel_bias[i], ATT_T), bsz, seq)
    w_out = odd_w_out[i]
    return _outproj_odd_ffn(x1, yc, yd, w_out[:w].astype(BF16), w_out[w:].astype(BF16), ffn2, layer)


def kernel(x, ffn1_norm, ffn1_w_gate, ffn1_w_up, ffn1_w_down, mix_norm, even_w_in, even_w_out, ssd_conv_w, ssd_conv_b, ssd_dt_bias, ssd_a_log, ssd_d, ssd_norm, s5_a_re, s5_a_im, s5_log_dt, s5_b_re, s5_b_im, s5_c_re, s5_c_im, s5_d, s5_w_glu, s5_b_glu, odd_w_in, odd_w_out, attn_q_norm, attn_k_norm, attn_rel_bias, pool_w, pool_scale, ffn2_norm, ffn2_w_gate, ffn2_w_up, ffn2_w_down):
    bsz, seq, d = x.shape
    depth = ffn1_norm.shape[0]
    x2d = x.reshape(bsz * seq, d)
    g3 = lambda v: v.astype(F32)[:, None, :]
    f1n, mxn, f2n = g3(ffn1_norm), g3(mix_norm), g3(ffn2_norm)
    as32 = lambda v: v.astype(F32)
    ffn1 = (f1n, as32(ffn1_w_gate), as32(ffn1_w_up), as32(ffn1_w_down))
    ffn2 = (f2n, as32(ffn2_w_gate), as32(ffn2_w_up), as32(ffn2_w_down))
    for layer in range(depth):
        if layer % 2 == 0:
            x2d = _even_layer(x2d, ffn1, ffn2, mxn, layer, bsz, seq, even_w_in, even_w_out, ssd_conv_w,
                              ssd_conv_b, ssd_dt_bias, ssd_a_log, ssd_d, ssd_norm, s5_a_re, s5_a_im,
                              s5_log_dt, s5_b_re, s5_b_im, s5_c_re, s5_c_im, s5_d, s5_w_glu, s5_b_glu)
        else:
            x2d = _odd_layer(x2d, ffn1, ffn2, mxn, layer, bsz, seq, odd_w_in, odd_w_out, attn_q_norm,
                             attn_k_norm, attn_rel_bias, pool_w, pool_scale)
    return x2d.reshape(bsz, seq, d)
```

```python
import functools

import jax
import jax.numpy as jnp
from jax import lax
from jax.experimental import pallas as pl
from jax.experimental.pallas import tpu as pltpu

F32 = jnp.float32
BF16 = jnp.bfloat16
HIGHEST = lax.Precision.HIGHEST

EPS = 1e-6
LOG2E = 1.4426950408889634
MASKED_FLOOR = -1e30
LANES = 128
VMEM_LIMIT = 56 * 1024 * 1024

SSD_HEADS = 8
SSD_HEAD_DIM = 64
SSD_INNER = 512
SSD_GROUPS = 2
SSD_STATE = 128
SSD_CONV = 4
SSD_XBC = 1024
S5_WIDTH = 512
S5_GROUP = 16
S5_GROUPS = 32
S5_STATE = 64
ATT_HEADS = 8
ATT_HEAD_DIM = 64
ATT_WIDTH = 512
ATT_CHUNK = 64
LEFT_CHUNKS = 8
MAX_REL = 128
POOL_WINDOWS = (2, 4, 8, 16)
POOL_GROUP = 128
POOL_MAX = 16

FFN_TM = 512
FFN_TF = 256
SSD_Q = 128
S5_Q = 128
ATT_T = 256
ATT_LOOKAHEAD = 5


def _params(*sem):
    return pltpu.CompilerParams(dimension_semantics=sem, vmem_limit_bytes=VMEM_LIMIT)


def _dot(a, b):
    return jnp.dot(a, b, preferred_element_type=F32)


def _dot_exact(a, b):
    return jnp.dot(a, b, preferred_element_type=F32, precision=HIGHEST)


def _split3(x):
    hi = x.astype(BF16)
    rest = x - hi.astype(F32)
    mid = rest.astype(BF16)
    lo = (rest - mid.astype(F32)).astype(BF16)
    return hi, mid, lo


def _select_left(sel, x):
    hi, mid, lo = _split3(x)
    return _dot(sel, hi) + _dot(sel, mid) + _dot(sel, lo)


def _select_right(x, sel):
    hi, mid, lo = _split3(x)
    return _dot(hi, sel) + _dot(mid, sel) + _dot(lo, sel)


def _rmsnorm_bf16(x, gain):
    ms = jnp.mean(x * x, axis=-1, keepdims=True)
    return (x * lax.rsqrt(ms + EPS) * gain).astype(BF16)


def _silu(x):
    return x * jax.nn.sigmoid(x)


def _ffn_stage(i, wg_ref, wu_ref, wd_ref, wg_s, wu_s, wd_s):
    @pl.when(i < wg_s.shape[0])
    def _():
        wg_s[i] = wg_ref[...].astype(BF16)
        wu_s[i] = wu_ref[...].astype(BF16)
        wd_s[pl.ds(pl.multiple_of(i * FFN_TF, FFN_TF), FFN_TF), :] = wd_ref[...].astype(BF16)


def _ffn_apply(x, g_ref, wg_s, wu_s, wd_s, act_scr):
    h = _rmsnorm_bf16(x, g_ref[...])
    for c in range(wg_s.shape[0]):
        gate = _dot(h, wg_s[c])
        up = _dot(h, wu_s[c])
        act_scr[:, c * FFN_TF:(c + 1) * FFN_TF] = (_silu(gate) * up).astype(BF16)
    return x + 0.5 * _dot(act_scr[...], wd_s[...])


def _resident(shape):
    return pl.BlockSpec(shape, lambda i: (0,) * len(shape), pipeline_mode=pl.Buffered(1))


def _ffn_chunks(ffn):
    dff = ffn[1].shape[-1]
    assert dff % FFN_TF == 0
    return dff // FFN_TF


def _ffn_specs(gain, wg, wu, wd, layer):
    d, dff = wg.shape[1:]
    last = dff // FFN_TF - 1
    return [
        pl.BlockSpec((None, 1, d), lambda i: (layer, 0, 0)),
        pl.BlockSpec((None, d, FFN_TF), lambda i: (layer, 0, jnp.minimum(i, last))),
        pl.BlockSpec((None, d, FFN_TF), lambda i: (layer, 0, jnp.minimum(i, last))),
        pl.BlockSpec((None, FFN_TF, d), lambda i: (layer, jnp.minimum(i, last), 0)),
    ]


def _ffn_scratch(ffn, tm):
    d, dff = ffn[1].shape[1:]
    n = dff // FFN_TF
    return [pltpu.VMEM((n, d, FFN_TF), BF16), pltpu.VMEM((n, d, FFN_TF), BF16),
            pltpu.VMEM((dff, d), BF16), pltpu.VMEM((tm, dff), BF16)]


def _ffn_inproj_even_kernel(x_ref, fg_ref, wg_ref, wu_ref, wd_ref, g_ref, wz_ref, wx_ref, wdt_ref,
                            wut_ref, x1_ref, z_ref, xbc_ref, dt_ref, ut_ref, wg_s, wu_s, wd_s, act_scr):
    i = pl.program_id(0)
    _ffn_stage(i, wg_ref, wu_ref, wd_ref, wg_s, wu_s, wd_s)

    @pl.when(i >= wg_s.shape[0])
    def _():
        x1 = _ffn_apply(x_ref[...], fg_ref, wg_s, wu_s, wd_s, act_scr)
        x1_ref[...] = x1
        h = _rmsnorm_bf16(x1, g_ref[...])
        z_ref[...] = _dot(h, wz_ref[...])
        xbc_ref[...] = _dot(h, wx_ref[...])
        dt_ref[...] = _dot(h, wdt_ref[...])
        ut = lax.dot_general(wut_ref[...], h, (((1,), (1,)), ((), ())), preferred_element_type=F32)
        q = ut_ref.shape[-1]
        for cc in range(ut_ref.shape[2]):
            ut_ref[0, :, cc] = ut[:, cc * q:(cc + 1) * q].reshape(S5_GROUPS, S5_GROUP, q)


def _ffn_inproj_even(x2d, ffn, gain, wz, wx, wdt, wut, layer, bsz, seq):
    m, d = x2d.shape
    tm = min(FFN_TM, seq)
    per_b = seq // tm
    q = min(S5_Q, seq)
    ns = _ffn_chunks(ffn)
    tile = lambda i: jnp.maximum(i - ns, 0)
    return pl.pallas_call(
        _ffn_inproj_even_kernel,
        out_shape=(
            jax.ShapeDtypeStruct((m, d), F32),
            jax.ShapeDtypeStruct((m, SSD_INNER), F32),
            jax.ShapeDtypeStruct((m, SSD_XBC), F32),
            jax.ShapeDtypeStruct((m, LANES), F32),
            jax.ShapeDtypeStruct((bsz, S5_GROUPS, seq // q, S5_GROUP, q), F32),
        ),
        grid=(ns + m // tm,),
        in_specs=[pl.BlockSpec((tm, d), lambda i: (tile(i), 0))] + _ffn_specs(*ffn, layer) + [
            pl.BlockSpec((None, 1, d), lambda i: (layer, 0, 0)),
            _resident(wz.shape), _resident(wx.shape), _resident(wdt.shape), _resident(wut.shape),
        ],
        out_specs=(
            pl.BlockSpec((tm, d), lambda i: (tile(i), 0)),
            pl.BlockSpec((tm, SSD_INNER), lambda i: (tile(i), 0)),
            pl.BlockSpec((tm, SSD_XBC), lambda i: (tile(i), 0)),
            pl.BlockSpec((tm, LANES), lambda i: (tile(i), 0)),
            pl.BlockSpec((1, S5_GROUPS, tm // q, S5_GROUP, q),
                         lambda i: (tile(i) // per_b, 0, tile(i) % per_b, 0, 0)),
        ),
        scratch_shapes=_ffn_scratch(ffn, tm),
        compiler_params=_params("arbitrary"),
        name="ffn_inproj_even",
    )(x2d, *ffn, gain, wz, wx, wdt, wut)


def _shift_rows(cur, prev8, j):
    rolled = pltpu.roll(cur, j, 0)
    prev_rolled = pltpu.roll(prev8, j, 0)
    row = lax.broadcasted_iota(jnp.int32, prev8.shape, 0)
    top = jnp.where(row < j, prev_rolled, rolled[:8])
    return jnp.concatenate([top, rolled[8:]], axis=0)


def _ssd_kernel(z_ref, xbc_ref, xprev_ref, dt_ref, cw_ref, cb_ref, dtb_ref, alog_ref,
                dx_ref, ng_ref, e_ref, o_ref, h_scr):
    t = pl.program_id(1)
    q = z_ref.shape[0]
    gw = SSD_INNER // SSD_GROUPS
    hpg = SSD_HEADS // SSD_GROUPS

    @pl.when(t == 0)
    def _():
        h_scr[...] = jnp.zeros_like(h_scr)

    xbc = xbc_ref[...]
    prev = jnp.where(t > 0, xprev_ref[...], 0.0)
    conv = xbc * cw_ref[SSD_CONV - 1:SSD_CONV, :] + cb_ref[...]
    for j in range(1, SSD_CONV):
        conv = conv + _shift_rows(xbc, prev, j) * cw_ref[SSD_CONV - 1 - j:SSD_CONV - j, :]
    xc = _silu(conv)
    xs = xc[:, :SSD_INNER]

    dtr = dt_ref[...] + dtb_ref[...]
    dt = jnp.maximum(dtr, 0.0) + jnp.log1p(jnp.exp(-jnp.abs(dtr)))
    a = -jnp.exp(alog_ref[...])
    row = lax.broadcasted_iota(jnp.int32, (q, q), 0)
    col = lax.broadcasted_iota(jnp.int32, (q, q), 1)
    causal = row >= col
    tril = jnp.where(causal, 1.0, 0.0).astype(BF16)
    acs = _select_left(tril, dt * a)
    acs_t = acs.T
    expanded = _select_right(jnp.concatenate([acs, dt], axis=0), e_ref[...])
    acs_x = expanded[:q]
    dt_x = expanded[q:]
    xdt = xs * dt_x
    decay_in = jnp.exp(acs_x)
    acs_end = acs_x[q - 1:q, :]
    to_end = jnp.exp(acs_end - acs_x)
    chunk_decay = jnp.exp(acs_end)
    xdt_end = xdt * to_end

    lane_head = lax.broadcasted_iota(jnp.int32, (1, gw), 1) // SSD_HEAD_DIM
    ys = []
    for g in range(SSD_GROUPS):
        bm = xc[:, SSD_INNER + g * SSD_STATE:SSD_INNER + (g + 1) * SSD_STATE].astype(BF16)
        cm = xc[:, SSD_INNER + (SSD_GROUPS + g) * SSD_STATE:
                SSD_INNER + (SSD_GROUPS + g + 1) * SSD_STATE].astype(BF16)
        cb = lax.dot_general(cm, bm, (((1,), (1,)), ((), ())), preferred_element_type=F32)
        xdt_g = xdt[:, g * gw:(g + 1) * gw]
        gmats, rhs = [], []
        for j in range(hpg):
            hh = g * hpg + j
            seg = acs[:, hh:hh + 1] - acs_t[hh:hh + 1, :]
            decay = jnp.exp(jnp.where(causal, seg, -jnp.inf))
            gmats.append((cb * decay).astype(BF16))
            rhs.append(jnp.where(lane_head == j, xdt_g, 0.0).astype(BF16))
        y_diag = _dot(jnp.concatenate(gmats, axis=1), jnp.concatenate(rhs, axis=0))
        h_prev = h_scr[g]
        y_off = _dot(cm, h_prev.astype(BF16)) * decay_in[:, g * gw:(g + 1) * gw]
        upd = lax.dot_general(bm, xdt_end[:, g * gw:(g + 1) * gw].astype(BF16),
                              (((0,), (0,)), ((), ())), preferred_element_type=F32)
        h_scr[g] = chunk_decay[:, g * gw:(g + 1) * gw] * h_prev + upd
        ys.append(y_diag + y_off)
    y = jnp.concatenate(ys, axis=1) + dx_ref[...] * xs
    y = y * _silu(z_ref[...])
    outs = []
    for g in range(SSD_GROUPS):
        yg = y[:, g * gw:(g + 1) * gw]
        outs.append(yg * lax.rsqrt(jnp.mean(yg * yg, axis=-1, keepdims=True) + EPS))
    o_ref[...] = (jnp.concatenate(outs, axis=1) * ng_ref[...]).astype(o_ref.dtype)


def _ssd(z, xbc, dt, conv_w, conv_b, dt_bias, a_log, d_x, norm_gain, expand, bsz, seq):
    m = z.shape[0]
    q = min(SSD_Q, seq)
    nq = seq // q
    const = lambda b, t: (0, 0)
    return pl.pallas_call(
        _ssd_kernel,
        out_shape=jax.ShapeDtypeStruct((m, SSD_INNER), BF16),
        grid=(bsz, nq),
        in_specs=[
            pl.BlockSpec((q, SSD_INNER), lambda b, t: (b * nq + t, 0)),
            pl.BlockSpec((q, SSD_XBC), lambda b, t: (b * nq + t, 0)),
            pl.BlockSpec((8, SSD_XBC), lambda b, t: (jnp.maximum((b * nq + t) * (q // 8) - 1, 0), 0)),
            pl.BlockSpec((q, LANES), lambda b, t: (b * nq + t, 0)),
            pl.BlockSpec(conv_w.shape, const),
            pl.BlockSpec(conv_b.shape, const),
            pl.BlockSpec(dt_bias.shape, const),
            pl.BlockSpec(a_log.shape, const),
            pl.BlockSpec(d_x.shape, const),
            pl.BlockSpec(norm_gain.shape, const),
            pl.BlockSpec(expand.shape, const),
        ],
        out_specs=pl.BlockSpec((q, SSD_INNER), lambda b, t: (b * nq + t, 0)),
        scratch_shapes=[pltpu.VMEM((SSD_GROUPS, SSD_STATE, SSD_INNER // SSD_GROUPS), F32)],
        compiler_params=_params("parallel", "arbitrary"),
        name="ssd",
    )(z, xbc, xbc, dt, conv_w, conv_b, dt_bias, a_log, d_x, norm_gain, expand)


def _cpow(base_re, base_im, exponent, nbits):
    shape = jnp.broadcast_shapes(base_re.shape, exponent.shape)
    pr = jnp.ones(shape, F32)
    pi = jnp.zeros(shape, F32)
    sr, si = base_re, base_im
    for k in range(nbits):
        bit = ((exponent >> k) & 1) == 1
        nr = pr * sr - pi * si
        ni = pr * si + pi * sr
        pr = jnp.where(bit, nr, pr)
        pi = jnp.where(bit, ni, pi)
        if k + 1 < nbits:
            sr, si = sr * sr - si * si, 2.0 * sr * si
    return pr, pi


def _s5_discretize(ar, ai, log_dt):
    dt = jnp.exp(log_dt)
    mag = jnp.exp(dt * ar)
    lam_re = mag * jnp.cos(dt * ai)
    lam_im = mag * jnp.sin(dt * ai)
    den = ar * ar + ai * ai
    f_re = ((lam_re - 1.0) * ar + lam_im * ai) / den
    f_im = (lam_im * ar - (lam_re - 1.0) * ai) / den
    return lam_re, lam_im, f_re, f_im


def _s5_kernel(u_ref, arow_ref, ldt_ref, bd_ref, cd_re_ref, cd_im_ref,
               d_ref, y_ref, kmat_scr, t_scr, f_scr):
    bsz, rows, q = u_ref.shape
    gsz = S5_GROUP
    nc = rows // gsz
    m = bsz * nc
    p2 = 2 * S5_STATE
    nbits = (q - 1).bit_length()
    log_dt = ldt_ref[...]

    lane = lax.broadcasted_iota(jnp.int32, (1, p2), 1)
    first_half_l = lane < S5_STATE
    rowp = lax.broadcasted_iota(jnp.int32, (p2, 1), 0)
    first_half_r = rowp < S5_STATE

    ar_r = arow_ref[0:1, :]
    ai_r = arow_ref[1:2, :]
    lr, li, fr, fi = _s5_discretize(ar_r, ai_r, log_dt)
    sign_b = jnp.where(first_half_l, -1.0, 1.0)
    bd1 = bd_ref[...]
    bd2 = pltpu.roll(bd1, S5_STATE, 1) * sign_b
    bb_a = fr * bd1 + fi * bd2
    bb_b = fr * bd2 - fi * bd1

    srow = lax.broadcasted_iota(jnp.int32, (q, 1), 0)
    pr_re, pr_im = _cpow(lr, li, (q - 1) - srow, nbits)
    for i in range(gsz):
        es = pr_re * bb_a[i:i + 1, :] + pr_im * bb_b[i:i + 1, :]
        t_scr[i * q:(i + 1) * q, gsz * q:gsz * q + p2] = es.astype(BF16)

    pt_re, pt_im = _cpow(lr, li, srow, nbits)
    p1_re = pt_re * lr - pt_im * li
    p1_im = pt_re * li + pt_im * lr
    cre = cd_re_ref[...]
    cim = cd_im_ref[...]
    for o in range(gsz):
        c_r = cre[o:o + 1, :]
        c_i = cim[o:o + 1, :]
        f_o = jnp.where(first_half_l, c_r * p1_re - c_i * p1_im, -(c_r * p1_im + c_i * p1_re))
        f_scr[o * q:(o + 1) * q, :] = f_o.astype(BF16)

    eye = (lax.broadcasted_iota(jnp.int32, (p2, p2), 0)
           == lax.broadcasted_iota(jnp.int32, (p2, p2), 1))
    lcr = jnp.sum(jnp.where(eye, lr, 0.0), axis=1, keepdims=True)
    lci = jnp.sum(jnp.where(eye, li, 0.0), axis=1, keepdims=True)
    dlane = lax.broadcasted_iota(jnp.int32, (1, q), 1)
    pw_re, pw_im = _cpow(lcr, lci, dlane, nbits)
    pow_stack = jnp.where(first_half_r, pw_re, pw_im)

    v1 = bb_a * jnp.where(first_half_l, 1.0, -1.0)
    v2 = -pltpu.roll(bb_a, S5_STATE, 1)
    coef =cre[:, None, :] * v1[None, :, :] + cim[:, None, :] * v2[None, :, :]
    kmat_scr[...] = _dot_exact(coef.reshape(gsz * gsz, p2), pow_stack)

    trow = lax.broadcasted_iota(jnp.int32, (q, q), 0)
    tcol = lax.broadcasted_iota(jnp.int32, (q, q), 1)
    lower = tcol >= trow
    for i in range(gsz):
        for o in range(gsz):
            k_row = kmat_scr[o * gsz + i:o * gsz + i + 1, :]
            blk = pltpu.roll(jnp.broadcast_to(k_row, (q, q)), 0, 1, stride=1, stride_axis=0)
            t_scr[i * q:(i + 1) * q, o * q:(o + 1) * q] = jnp.where(lower, blk, 0.0).astype(BF16)

    u_f32 = [jnp.concatenate([u_ref[b, pl.ds(i, nc, stride=gsz), :] for b in range(bsz)], axis=0)
             for i in range(gsz)]
    u_cat = jnp.concatenate([v.astype(BF16) for v in u_f32], axis=1)
    acc = _dot(u_cat, t_scr[...])
    x = acc[:, gsz * q:]

    sq_re, sq_im = lr, li
    for _ in range(q.bit_length() - 1):
        sq_re, sq_im = sq_re * sq_re - sq_im * sq_im, 2.0 * sq_re * sq_im
    crow = lax.broadcasted_iota(jnp.int32, (m, 1), 0) % nc
    k = 1
    while k < nc:
        sh = jnp.where(crow >= k, pltpu.roll(x, k, 0), 0.0)
        x = x + sq_re * sh + (sq_im * sign_b) * pltpu.roll(sh, S5_STATE, 1)
        sq_re, sq_im = sq_re * sq_re - sq_im * sq_im, 2.0 * sq_re * sq_im
        k *= 2
    h_prev = jnp.where(crow >= 1, pltpu.roll(x, 1, 0), 0.0)
    y = acc[:, :gsz * q] + lax.dot_general(h_prev.astype(BF16), f_scr[...], (((1,), (1,)), ((), ())),
                                           preferred_element_type=F32)
    for o in range(gsz):
        y_o = y[:, o * q:(o + 1) * q] + d_ref[o:o + 1, :] * u_f32[o]
        for b in range(bsz):
            y_ref[b, pl.ds(o, nc, stride=gsz), :] = y_o[b * nc:(b + 1) * nc]


def _s5(u5, arow, ldt, bd, cd_re, cd_im, d_rows):
    bsz, _, nc, _, q = u5.shape
    rows = nc * S5_GROUP
    p2 = 2 * S5_STATE
    per_g = lambda g: (g, 0, 0)
    y4 = pl.pallas_call(
        _s5_kernel,
        out_shape=jax.ShapeDtypeStruct((bsz, S5_GROUPS, rows, q), F32),
        grid=(S5_GROUPS,),
        in_specs=[
            pl.BlockSpec((bsz, None, rows, q), lambda g: (0, g, 0, 0)),
            pl.BlockSpec((None, 2, p2), per_g),
            pl.BlockSpec((None, 1, 1), per_g),
            pl.BlockSpec((None, S5_GROUP, p2), per_g),
            pl.BlockSpec((None, S5_GROUP, p2), per_g),
            pl.BlockSpec((None, S5_GROUP, p2), per_g),
            pl.BlockSpec((None, S5_GROUP, q), per_g),
        ],
        out_specs=pl.BlockSpec((bsz, None, rows, q), lambda g: (0, g, 0, 0)),
        scratch_shapes=[
            pltpu.VMEM((S5_GROUP * S5_GROUP, q), F32),
            pltpu.VMEM((S5_GROUP * q, S5_GROUP * q + p2), BF16),
            pltpu.VMEM((S5_GROUP * q, p2), BF16),
        ],
        compiler_params=_params("parallel"),
        name="s5",
    )(u5.reshape(bsz, S5_GROUPS, rows, q), arow, ldt, bd, cd_re, cd_im, d_rows)
    return y4.reshape(u5.shape)


def _gelu_tanh(x):
    return 0.5 * x * (1.0 + jnp.tanh(0.7978845608028654 * (x + 0.044715 * (x * x * x))))


def _outproj_even_ffn_kernel(x_ref, ya_ref, ybt_ref, wglut_ref, bglu_ref, wa_ref, wb_ref,
                             fg_ref, wg_ref, wu_ref, wd_ref, o_ref, wg_s, wu_s, wd_s, act_scr):
    i = pl.program_id(0)
    _ffn_stage(i, wg_ref, wu_ref, wd_ref, wg_s, wu_s, wd_s)

    @pl.when(i >= wg_s.shape[0])
    def _():
        q = ybt_ref.shape[-1]
        ybt = jnp.concatenate([ybt_ref[0, :, cc].reshape(S5_WIDTH, q) for cc in range(ybt_ref.shape[2])],
                              axis=1)
        gate = _dot(wglut_ref[...], _gelu_tanh(ybt).astype(BF16)) + bglu_ref[...]
        yb = (ybt * jax.nn.sigmoid(gate)).T.astype(BF16)
        x2 = x_ref[...] + _dot(ya_ref[...], wa_ref[...]) + _dot(yb, wb_ref[...])
        o_ref[...] = _ffn_apply(x2, fg_ref, wg_s, wu_s, wd_s, act_scr)


def _outproj_even_ffn(x2d, ya, yb5, wglut, bglu, wa, wb, ffn, layer, bsz, seq):
    m, d = x2d.shape
    tm = min(FFN_TM, seq)
    per_b = seq // tm
    q = yb5.shape[-1]
    ns = _ffn_chunks(ffn)
    tile = lambda i: jnp.maximum(i - ns, 0)
    return pl.pallas_call(
        _outproj_even_ffn_kernel,
        out_shape=jax.ShapeDtypeStruct((m, d), F32),
        grid=(ns + m // tm,),
        in_specs=[
            pl.BlockSpec((tm, d), lambda i: (tile(i), 0)),
            pl.BlockSpec((tm, SSD_INNER), lambda i: (tile(i), 0)),
            pl.BlockSpec((1, S5_GROUPS, tm // q, S5_GROUP, q),
                         lambda i: (tile(i) // per_b, 0, tile(i) % per_b, 0, 0)),
            _resident(wglut.shape), _resident(bglu.shape), _resident(wa.shape), _resident(wb.shape),
        ] + _ffn_specs(*ffn, layer),
        out_specs=pl.BlockSpec((tm, d), lambda i: (tile(i), 0)),
        scratch_shapes=_ffn_scratch(ffn, tm),
        compiler_params=_params("arbitrary"),
        name="outproj_even_ffn",
    )(x2d, ya, yb5, wglut, bglu, wa, wb, *ffn)


def _head_rmsnorm(x, gain):
    assert LANES == 2 * ATT_HEAD_DIM
    first = lax.broadcasted_iota(jnp.int32, (1, LANES), 1) < ATT_HEAD_DIM
    tiles = []
    for p in range(x.shape[1] // LANES):
        xp = x[:, p * LANES:(p + 1) * LANES]
        sq = xp * xp
        s0 = jnp.sum(jnp.where(first, sq, 0.0), axis=-1, keepdims=True)
        s1 = jnp.sum(jnp.where(first, 0.0, sq), axis=-1, keepdims=True)
        ss = jnp.where(first, s0, s1)
        tiles.append(xp * lax.rsqrt(ss * (1.0 / ATT_HEAD_DIM) + EPS))
    return jnp.concatenate(tiles, axis=1) * gain


def _head_rmsnorm_t(xt, gain_col):
    c, n = xt.shape
    x3 = xt.reshape(c // ATT_HEAD_DIM, ATT_HEAD_DIM, n)
    ms = jnp.mean(x3 * x3, axis=1, keepdims=True)
    return (x3 * lax.rsqrt(ms + EPS)).reshape(c, n) * gain_col


def _pool_apply(u, prev, pos, w_ref, sc_ref):
    ext = jnp.concatenate([prev, u], axis=0)
    sums = {}
    s = ext
    w = 1
    while w < POOL_MAX:
        s = s + pltpu.roll(s, w, 0)
        w *= 2
        sums[w] = s[POOL_MAX:]
    outs = []
    for g, win in enumerate(POOL_WINDOWS):
        sl = slice(g * POOL_GROUP, (g + 1) * POOL_GROUP)
        count = jnp.minimum(pos + 1, win).astype(F32)
        pooled = sums[win][:, sl] / count - u[:, sl]
        outs.append(_dot(pooled.astype(BF16), w_ref[g]))
    return jnp.concatenate(outs, axis=1) * sc_ref[...]


def _ffn_inproj_odd_kernel(x_ref, fg_ref, wg_ref, wu_ref, wd_ref, g_ref, wqt_ref, wk_ref, wvt_ref,
                           wup_ref, qg_ref, kg_ref, pw_ref, ps_ref,
                           x1_ref, qt_ref, k_ref, vt_ref, yd_ref, wg_s, wu_s, wd_s, act_scr, carry_scr,
                           *, tiles_per_seq):
    i = pl.program_id(0)
    tm = x_ref.shape[0]
    n_stage = wg_s.shape[0]
    _ffn_stage(i, wg_ref, wu_ref, wd_ref, wg_s, wu_s, wd_s)

    @pl.when(i == 0)
    def _():
        carry_scr[...] = jnp.zeros_like(carry_scr)

    @pl.when(i >= n_stage)
    def _():
        x1 = _ffn_apply(x_ref[...], fg_ref, wg_s, wu_s, wd_s, act_scr)
        x1_ref[...] = x1
        h = _rmsnorm_bf16(x1, g_ref[...])
        nt = (((1,), (1,)), ((), ()))
        qt = lax.dot_general(wqt_ref[...], h, nt, preferred_element_type=F32)
        qt = _head_rmsnorm_t(qt, qg_ref[...]) * (ATT_HEAD_DIM ** -0.5 * LOG2E)
        qt_ref[0] = qt.astype(BF16)
        k_ref[...] = _head_rmsnorm(_dot(h, wk_ref[...]), kg_ref[...]).astype(BF16)
        vt_ref[0] = lax.dot_general(wvt_ref[...], h, nt, preferred_element_type=F32).astype(BF16)
        u = _dot(h, wup_ref[...])
        tile_in_seq = (i - n_stage) % tiles_per_seq
        prev = jnp.where(tile_in_seq == 0, 0.0, carry_scr[...])
        pos = tile_in_seq * tm + lax.broadcasted_iota(jnp.int32, (tm, 1), 0)
        yd_ref[...] = _pool_apply(u, prev, pos, pw_ref, ps_ref).astype(yd_ref.dtype)
        carry_scr[...] = u[tm - POOL_MAX:]


def _ffn_inproj_odd(x2d, ffn, gain, wqt, wk, wvt, wup, qg_col, kg, w_pool, pool_scale, layer, bsz, seq):
    m, d = x2d.shape
    tm = min(FFN_TM, seq)
    per_b = seq // tm
    w = ATT_WIDTH
    ns = _ffn_chunks(ffn)
    tl = lambda i: jnp.maximum(i - ns, 0)
    tile = pl.BlockSpec((tm, w), lambda i: (tl(i), 0))
    tile_t = pl.BlockSpec((1, w, tm), lambda i: (tl(i) // per_b, 0, tl(i) % per_b))
    row_tile = pl.BlockSpec((tm, d), lambda i: (tl(i), 0))
    tok = jax.ShapeDtypeStruct((m, w), BF16)
    chan = jax.ShapeDtypeStruct((bsz, w, seq), BF16)
    return pl.pallas_call(
        functools.partial(_ffn_inproj_odd_kernel, tiles_per_seq=per_b),
        out_shape=(jax.ShapeDtypeStruct((m, d), F32), chan, tok, chan, tok),
        grid=(ns + m // tm,),
        in_specs=[row_tile] + _ffn_specs(*ffn, layer) + [
            pl.BlockSpec((None, 1, d), lambda i: (layer, 0, 0)),
            _resident(wqt.shape), _resident(wk.shape), _resident(wvt.shape), _resident(wup.shape),
            _resident(qg_col.shape), _resident(kg.shape),
            _resident(w_pool.shape), _resident(pool_scale.shape),
        ],
        out_specs=(row_tile, tile_t, tile, tile_t, tile),
        scratch_shapes=_ffn_scratch(ffn, tm) + [pltpu.VMEM((POOL_MAX, w), F32)],
        compiler_params=_params("arbitrary"),
        name="ffn_inproj_odd",
    )(x2d, *ffn, gain, wqt, wk, wvt, wup, qg_col, kg, w_pool, pool_scale)


def _attn_kernel(qt_ref, k0_ref, k1_ref, k2_ref, vt0_ref, vt1_ref, vt2_ref, brow_ref, o_ref, bias_ref):
    t = pl.program_id(1)
    tq = o_ref.shape[0]
    nk = 3 * tq

    @pl.when((pl.program_id(0) == 0) & (t == 0))
    def _():
        keyi = lax.broadcasted_iota(jnp.int32, (nk, tq), 0)
        qryi = lax.broadcasted_iota(jnp.int32, (nk, tq), 1)
        rel_chunk = keyi // ATT_CHUNK - (2 * tq // ATT_CHUNK - LEFT_CHUNKS) - qryi // ATT_CHUNK
        rel_chunk = jnp.where(rel_chunk >= 0, rel_chunk, LEFT_CHUNKS + 1)
        for hh in range(ATT_HEADS):
            base = jnp.broadcast_to(brow_ref[hh:hh + 1, :], (nk, 4 * tq))
            shifted = pltpu.roll(base, 0, 1, stride=1, stride_axis=0)
            bias = shifted[:, nk:] * LOG2E
            for var in range(3):
                first_key = (2 - var) * tq
                keep = jnp.where(keyi >= first_key, rel_chunk, LEFT_CHUNKS + 1) <= LEFT_CHUNKS
                bias_ref[var, hh] = jnp.where(keep, bias, -jnp.inf)

    var = jnp.minimum(t, 2)
    qt = qt_ref[0]
    kcat = jnp.concatenate([k0_ref[...], k1_ref[...], k2_ref[...]], axis=0)
    vtcat = jnp.concatenate([vt0_ref[0], vt1_ref[0], vt2_ref[0]], axis=1)
    hd = ATT_HEAD_DIM
    heads_per_slab = LANES // hd
    no_q = jnp.zeros((LANES - hd, tq), qt.dtype)
    ones_rows = jnp.ones((16, nk), qt.dtype)
    n_tiles = nk // tq

    def scores(head, j):
        p, hh = divmod(head, heads_per_slab)
        rows = slice(head * hd, (head + 1) * hd)
        qth = jnp.concatenate([qt[rows]] + [no_q] if hh == 0 else [no_q] + [qt[rows]], axis=0)
        ks = slice(j * tq, (j + 1) * tq)
        return _dot(kcat[ks, p * LANES:(p + 1) * LANES], qth) + bias_ref[var, head, ks, :]

    items = [(head, j) for head in range(ATT_HEADS) for j in range(n_tiles)]
    ahead = [scores(*it) for it in items[:ATT_LOOKAHEAD]]
    outs, parts, tops = [], [], []
    for idx, (head, j) in enumerate(items):
        st = ahead.pop(0)
        if idx + ATT_LOOKAHEAD < len(items):
            ahead.append(scores(*items[idx + ATT_LOOKAHEAD]))
        rows = slice(head * hd, (head + 1) * hd)
        ks = slice(j * tq, (j + 1) * tq)
        top = jnp.maximum(jnp.max(st, axis=0, keepdims=True), MASKED_FLOOR)
        e = jnp.exp2(st - top).astype(BF16)
        vth = jnp.concatenate([vtcat[rows, ks], ones_rows[:, ks]], axis=0)
        parts.append(_dot(vth, e))
        tops.append(top)
        if j == n_tiles - 1:
            top_all = functools.reduce(jnp.maximum, tops)
            ot = sum(part * jnp.exp2(tp - top_all) for part, tp in zip(parts, tops))
            outs.append(ot[:hd] / ot[hd:hd + 1])
            parts, tops = [], []
            if head % heads_per_slab == heads_per_slab - 1:
                p = head // heads_per_slab
                o_ref[:, p * LANES:(p + 1) * LANES] = jnp.concatenate(outs, axis=0).T.astype(o_ref.dtype)
                outs = []


def _attention(qt, k, vt, brow, bsz, seq):
    m, w = k.shape
    tq = ATT_T
    assert 2 * tq >= LEFT_CHUNKS * ATT_CHUNK and tq > MAX_REL and seq % tq == 0
    nt = seq // tq
    back = lambda n: (lambda b, t: (b * nt + jnp.maximum(t - n, 0), 0))
    back_t = lambda n: (lambda b, t: (b, 0, jnp.maximum(t - n, 0)))
    tok = lambda f: pl.BlockSpec((tq, w), f)
    chan = lambda f: pl.BlockSpec((1, w, tq), f)
    return pl.pallas_call(
        _attn_kernel,
        out_shape=jax.ShapeDtypeStruct((m, w), BF16),
        grid=(bsz, nt),
        in_specs=[chan(back_t(0)), tok(back(2)), tok(back(1)), tok(back(0)),
                  chan(back_t(2)), chan(back_t(1)), chan(back_t(0)),
                  pl.BlockSpec(brow.shape, lambda b, t: (0, 0))],
        out_specs=tok(back(0)),
        scratch_shapes=[pltpu.VMEM((3, ATT_HEADS, 3 * tq, tq), F32)],
        compiler_params=_params("arbitrary", "arbitrary"),
        name="band_attention",
    )(qt, k, k, k, vt, vt, vt, brow)


def _bias_rows(rel_bias, tq):
    rb = rel_bias.astype(F32)
    nh = rb.shape[0]
    near = jnp.broadcast_to(rb[:, :1], (nh, tq - MAX_REL))
    far = jnp.broadcast_to(rb[:, -1:], (nh, 3 * tq - MAX_REL - 1))
    return jnp.concatenate([near, rb, far], axis=1)


def _outproj_odd_ffn_kernel(x_ref, yc_ref, yd_ref, wa_ref, wb_ref, fg_ref, wg_ref, wu_ref, wd_ref,
                            o_ref, wg_s, wu_s, wd_s, act_scr):
    i = pl.program_id(0)
    _ffn_stage(i, wg_ref, wu_ref, wd_ref, wg_s, wu_s, wd_s)

    @pl.when(i >= wg_s.shape[0])
    def _():
        x2 = x_ref[...] + _dot(yc_ref[...], wa_ref[...]) + _dot(yd_ref[...], wb_ref[...])
        o_ref[...] = _ffn_apply(x2, fg_ref, wg_s, wu_s, wd_s, act_scr)


def _outproj_odd_ffn(x2d, yc, yd, wa, wb, ffn, layer):
    m, d = x2d.shape
    tm = min(FFN_TM, m)
    ns = _ffn_chunks(ffn)
    tile = lambda i: jnp.maximum(i - ns, 0)
    return pl.pallas_call(
        _outproj_odd_ffn_kernel,
        out_shape=jax.ShapeDtypeStruct((m, d), F32),
        grid=(ns + m // tm,),
        in_specs=[
            pl.BlockSpec((tm, d), lambda i: (tile(i), 0)),
            pl.BlockSpec((tm, ATT_WIDTH), lambda i: (tile(i), 0)),
            pl.BlockSpec((tm, ATT_WIDTH), lambda i: (tile(i), 0)),
            _resident(wa.shape), _resident(wb.shape),
        ] + _ffn_specs(*ffn, layer),
        out_specs=pl.BlockSpec((tm, d), lambda i: (tile(i), 0)),
        scratch_shapes=_ffn_scratch(ffn, tm),
        compiler_params=_params("arbitrary"),
        name="outproj_odd_ffn",
    )(x2d, yc, yd, wa, wb, *ffn)


def _even_layer(x2d, ffn1, ffn2, gain, layer, bsz, seq, even_w_in, even_w_out, ssd_conv_w, ssd_conv_b,
                ssd_dt_bias, ssd_a_log, ssd_d, ssd_norm, s5_a_re, s5_a_im, s5_log_dt, s5_b_re, s5_b_im,
                s5_c_re, s5_c_im, s5_d, s5_w_glu, s5_b_glu):
    i = layer // 2
    w_in = even_w_in[i]
    o1 = SSD_INNER
    o2 = o1 + SSD_XBC
    o3 = o2 + SSD_HEADS
    wz = w_in[:, :o1].astype(BF16)
    wx = w_in[:, o1:o2].astype(BF16)
    wdt = jnp.pad(w_in[:, o2:o3], ((0, 0), (0, LANES - SSD_HEADS))).astype(BF16)
    wut = w_in[:, o3:].T.astype(BF16)
    x1, z, xbc, dt, ut = _ffn_inproj_even(x2d, ffn1, gain, wz, wx, wdt, wut, layer, bsz, seq)

    pad_h = lambda v: jnp.pad(v.astype(F32), (0, LANES - SSD_HEADS))[None, :]
    expand = (jnp.arange(LANES)[:, None] == (jnp.arange(SSD_INNER)[None, :] // SSD_HEAD_DIM)).astype(BF16)
    ya = _ssd(z, xbc, dt, ssd_conv_w[i], ssd_conv_b[i][None, :], pad_h(ssd_dt_bias[i]),
              pad_h(ssd_a_log[i]), jnp.repeat(ssd_d[i], SSD_HEAD_DIM)[None, :], ssd_norm[i][None, :],
              expand, bsz, seq)

    q = min(S5_Q, seq)
    are, aim = s5_a_re[i], s5_a_im[i]
    dup = lambda v: jnp.concatenate([v, v], axis=-1)
    arow = jnp.stack([dup(are), dup(aim)], axis=1)
    ldt = s5_log_dt[i][:, None, None]
    bd = jnp.concatenate([jnp.swapaxes(s5_b_re[i], 1, 2), jnp.swapaxes(s5_b_im[i], 1, 2)], axis=-1)
    cd_re, cd_im = dup(s5_c_re[i]), dup(s5_c_im[i])
    d_rows = jnp.broadcast_to(s5_d[i][:, :, None], (S5_GROUPS, S5_GROUP, q))
    yb5 = _s5(ut, arow, ldt, bd, cd_re, cd_im, d_rows)

    w_out = even_w_out[i]
    return _outproj_even_ffn(x1, ya, yb5, s5_w_glu[i].T.astype(BF16), s5_b_glu[i][:, None],
                             w_out[:SSD_INNER].astype(BF16), w_out[SSD_INNER:].astype(BF16),
                             ffn2, layer, bsz, seq)


def _odd_layer(x2d, ffn1, ffn2, gain, layer, bsz, seq, odd_w_in, odd_w_out, attn_q_norm, attn_k_norm,
               attn_rel_bias, pool_w, pool_scale):
    i = layer // 2
    w_in = odd_w_in[i].astype(BF16)
    w = ATT_WIDTH
    tile_gain = lambda v: jnp.tile(v.astype(F32), ATT_HEADS)[None, :]
    x1, qt, k, vt, yd = _ffn_inproj_odd(
        x2d, ffn1, gain, w_in[:, :w].T, w_in[:, w:2 * w], w_in[:, 2 * w:3 * w].T, w_in[:, 3 * w:],
        tile_gain(attn_q_norm[i]).T, tile_gain(attn_k_norm[i]),
        pool_w[i].astype(BF16), pool_scale[i][None, :], layer, bsz, seq)
    yc = _attention(qt, k, vt, _bias_rows(attn_rel_bias[i], ATT_T), bsz, seq)
    w_out = odd_w_out[i]
    return _outproj_odd_ffn(x1, yc, yd, w_out[:w].astype(BF16), w_out[w:].astype(BF16), ffn2, layer)


def kernel(x, ffn1_norm, ffn1_w_gate, ffn1_w_up, ffn1_w_down, mix_norm, even_w_in, even_w_out, ssd_conv_w, ssd_conv_b, ssd_dt_bias, ssd_a_log, ssd_d, ssd_norm, s5_a_re, s5_a_im, s5_log_dt, s5_b_re, s5_b_im, s5_c_re, s5_c_im, s5_d, s5_w_glu, s5_b_glu, odd_w_in, odd_w_out, attn_q_norm, attn_k_norm, attn_rel_bias, pool_w, pool_scale, ffn2_norm, ffn2_w_gate, ffn2_w_up, ffn2_w_down):
    bsz, seq, d = x.shape
    depth = ffn1_norm.shape[0]
    x2d = x.reshape(bsz * seq, d)
    g3 = lambda v: v.astype(F32)[:, None, :]
    f1n, mxn, f2n = g3(ffn1_norm), g3(mix_norm), g3(ffn2_norm)
    as32 = lambda v: v.astype(F32)
    ffn1 = (f1n, as32(ffn1_w_gate), as32(ffn1_w_up), as32(ffn1_w_down))
    ffn2 = (f2n, as32(ffn2_w_gate), as32(ffn2_w_up), as32(ffn2_w_down))
    for layer in range(depth):
        if layer % 2 == 0:
            x2d = _even_layer(x2d, ffn1, ffn2, mxn, layer, bsz, seq, even_w_in, even_w_out, ssd_conv_w,
                              ssd_conv_b, ssd_dt_bias, ssd_a_log, ssd_d, ssd_norm, s5_a_re, s5_a_im,
                              s5_log_dt, s5_b_re, s5_b_im, s5_c_re, s5_c_im, s5_d, s5_w_glu, s5_b_glu)
        else:
            x2d = _odd_layer(x2d, ffn1, ffn2, mxn, layer, bsz, seq, odd_w_in, odd_w_out, attn_q_norm,
                             attn_k_norm, attn_rel_bias, pool_w, pool_scale)
    return x2d.reshape(bsz, seq, d)
```

```python
import functools
import math

import jax
import jax.numpy as jnp
from jax import lax
from jax.experimental import pallas as pl
from jax.experimental.pallas import tpu as pltpu

F32 = jnp.float32
BF16 = jnp.bfloat16
HIGHEST = lax.Precision.HIGHEST

EPS = 1e-6
LOG2E = 1.4426950408889634
MASKED_FLOOR = -1e30
LANES = 128
VMEM_LIMIT = 56 * 1024 * 1024

SSD_HEADS = 8
SSD_HEAD_DIM = 64
SSD_INNER = 512
SSD_GROUPS = 2
SSD_STATE = 128
SSD_CONV = 4
SSD_XBC = 1024
S5_WIDTH = 512
S5_GROUP = 16
S5_GROUPS = 32
S5_STATE = 64
ATT_HEADS = 8
ATT_HEAD_DIM = 64
ATT_WIDTH = 512
ATT_CHUNK = 64
LEFT_CHUNKS = 8
MAX_REL = 128
POOL_WINDOWS = (2, 4, 8, 16)
POOL_GROUP = 128
POOL_MAX = 16

FFN_TM = 512
FFN_TF = 256
SSD_Q = 128
SSD_PER_STEP = 4
S5_Q = 128
S5_PER_STEP = 2
ATT_T = 256
ATT_LOOKAHEAD = 5


def _params(*sem):
    return pltpu.CompilerParams(dimension_semantics=sem, vmem_limit_bytes=VMEM_LIMIT)


def _dot(a, b):
    return jnp.dot(a, b, preferred_element_type=F32)


def _dot_exact(a, b):
    return jnp.dot(a, b, preferred_element_type=F32, precision=HIGHEST)


def _split3(x):
    hi = x.astype(BF16)
    rest = x - hi.astype(F32)
    mid = rest.astype(BF16)
    lo = (rest - mid.astype(F32)).astype(BF16)
    return hi, mid, lo


def _select_left(sel, x):
    hi, mid, lo = _split3(x)
    return _dot(sel, hi) + _dot(sel, mid) + _dot(sel, lo)


def _select_right(x, sel):
    hi, mid, lo = _split3(x)
    return _dot(hi, sel) + _dot(mid, sel) + _dot(lo, sel)


def _rmsnorm_bf16(x, gain):
    ms = jnp.mean(x * x, axis=-1, keepdims=True)
    return (x * lax.rsqrt(ms + EPS) * gain).astype(BF16)


def _silu(x):
    return x * jax.nn.sigmoid(x)


def _ffn_stage(i, wg_ref, wu_ref, wd_ref, wg_s, wu_s, wd_s):
    @pl.when(i < wg_s.shape[0])
    def _():
        wg_s[i] = wg_ref[...].astype(BF16)
        wu_s[i] = wu_ref[...].astype(BF16)
        wd_s[pl.ds(pl.multiple_of(i * FFN_TF, FFN_TF), FFN_TF), :] = wd_ref[...].astype(BF16)


def _ffn_apply(x, g_ref, wg_s, wu_s, wd_s, act_scr):
    h = _rmsnorm_bf16(x, g_ref[...])
    for c in range(wg_s.shape[0]):
        gate = _dot(h, wg_s[c])
        up = _dot(h, wu_s[c])
        act_scr[:, c * FFN_TF:(c + 1) * FFN_TF] = (_silu(gate) * up).astype(BF16)
    return x + 0.5 * _dot(act_scr[...], wd_s[...])


def _resident(shape):
    return pl.BlockSpec(shape, lambda i: (0,) * len(shape), pipeline_mode=pl.Buffered(1))


def _ffn_chunks(ffn):
    dff = ffn[1].shape[-1]
    assert dff % FFN_TF == 0
    return dff // FFN_TF


def _ffn_specs(gain, wg, wu, wd, layer):
    d, dff = wg.shape[1:]
    last = dff // FFN_TF - 1
    return [
        pl.BlockSpec((None, 1, d), lambda i: (layer, 0, 0)),
        pl.BlockSpec((None, d, FFN_TF), lambda i: (layer, 0, jnp.minimum(i, last))),
        pl.BlockSpec((None, d, FFN_TF), lambda i: (layer, 0, jnp.minimum(i, last))),
        pl.BlockSpec((None, FFN_TF, d), lambda i: (layer, jnp.minimum(i, last), 0)),
    ]


def _ffn_scratch(ffn, tm):
    d, dff = ffn[1].shape[1:]
    n = dff // FFN_TF
    return [pltpu.VMEM((n, d, FFN_TF), BF16), pltpu.VMEM((n, d, FFN_TF), BF16),
            pltpu.VMEM((dff, d), BF16), pltpu.VMEM((tm, dff), BF16)]


def _ffn_inproj_even_kernel(x_ref, fg_ref, wg_ref, wu_ref, wd_ref, g_ref, wz_ref, wx_ref, wdt_ref,
                            wut_ref, x1_ref, z_ref, xbc_ref, dt_ref, ut_ref, wg_s, wu_s, wd_s, act_scr):
    i = pl.program_id(0)
    _ffn_stage(i, wg_ref, wu_ref, wd_ref, wg_s, wu_s, wd_s)

    @pl.when(i >= wg_s.shape[0])
    def _():
        x1 = _ffn_apply(x_ref[...], fg_ref, wg_s, wu_s, wd_s, act_scr)
        x1_ref[...] = x1
        h = _rmsnorm_bf16(x1, g_ref[...])
        z_ref[...] = _dot(h, wz_ref[...])
        xbc_ref[...] = _dot(h, wx_ref[...])
        dt_ref[...] = _dot(h, wdt_ref[...])
        ut = lax.dot_general(wut_ref[...], h, (((1,), (1,)), ((), ())), preferred_element_type=F32)
        q = ut_ref.shape[-1]
        for cc in range(ut_ref.shape[2]):
            ut_ref[0, :, cc] = ut[:, cc * q:(cc + 1) * q].reshape(S5_GROUPS, S5_GROUP, q)


def _ffn_inproj_even(x2d, ffn, gain, wz, wx, wdt, wut, layer, bsz, seq):
    m, d = x2d.shape
    tm = min(FFN_TM, seq)
    per_b = seq // tm
    q = min(S5_Q, seq)
    ns = _ffn_chunks(ffn)
    tile = lambda i: jnp.maximum(i - ns, 0)
    return pl.pallas_call(
        _ffn_inproj_even_kernel,
        out_shape=(
            jax.ShapeDtypeStruct((m, d), F32),
            jax.ShapeDtypeStruct((m, SSD_INNER), F32),
            jax.ShapeDtypeStruct((m, SSD_XBC), F32),
            jax.ShapeDtypeStruct((m, LANES), F32),
            jax.ShapeDtypeStruct((bsz, S5_GROUPS, seq // q, S5_GROUP, q), F32),
        ),
        grid=(ns + m // tm,),
        in_specs=[pl.BlockSpec((tm, d), lambda i: (tile(i), 0))] + _ffn_specs(*ffn, layer) + [
            pl.BlockSpec((None, 1, d), lambda i: (layer, 0, 0)),
            _resident(wz.shape), _resident(wx.shape), _resident(wdt.shape), _resident(wut.shape),
        ],
        out_specs=(
            pl.BlockSpec((tm, d), lambda i: (tile(i), 0)),
            pl.BlockSpec((tm, SSD_INNER), lambda i: (tile(i), 0)),
            pl.BlockSpec((tm, SSD_XBC), lambda i: (tile(i), 0)),
            pl.BlockSpec((tm, LANES), lambda i: (tile(i), 0)),
            pl.BlockSpec((1, S5_GROUPS, tm // q, S5_GROUP, q),
                         lambda i: (tile(i) // per_b, 0, tile(i) % per_b, 0, 0)),
        ),
        scratch_shapes=_ffn_scratch(ffn, tm),
        compiler_params=_params("arbitrary"),
        name="ffn_inproj_even",
    )(x2d, *ffn, gain, wz, wx, wdt, wut)


def _shift_rows(cur, prev8, j):
    rolled = pltpu.roll(cur, j, 0)
    prev_rolled = pltpu.roll(prev8, j, 0)
    row = lax.broadcasted_iota(jnp.int32, prev8.shape, 0)
    top = jnp.where(row < j, prev_rolled, rolled[:8])
    return jnp.concatenate([top, rolled[8:]], axis=0)


def _ssd_kernel(z_ref, xbc_ref, xprev_ref, dt_ref, *rest):
    consts, o_ref, h_scr = rest[:-2], rest[-2], rest[-1]

    @pl.when(pl.program_id(1) == 0)
    def _():
        h_scr[...] = jnp.zeros_like(h_scr)

    for bb in range(z_ref.shape[0]):
        _ssd_chunk(z_ref.at[bb], xbc_ref.at[bb], xprev_ref.at[bb], dt_ref.at[bb], *consts,
                   o_ref.at[bb], h_scr.at[bb])


def _ssd_chunk(z_ref, xbc_ref, xprev_ref, dt_ref, cw_ref, cb_ref, dtb_ref, alog_ref,
               dx_ref, ng_ref, e_ref, o_ref, h_scr):
    t = pl.program_id(1)
    q = z_ref.shape[0]
    gw = SSD_INNER // SSD_GROUPS
    hpg = SSD_HEADS // SSD_GROUPS

    xbc = xbc_ref[...]
    prev = jnp.where(t > 0, xprev_ref[...], 0.0)
    conv = xbc * cw_ref[SSD_CONV - 1:SSD_CONV, :] + cb_ref[...]
    for j in range(1, SSD_CONV):
        conv = conv + _shift_rows(xbc, prev, j) * cw_ref[SSD_CONV - 1 - j:SSD_CONV - j, :]
    xc = _silu(conv)
    xs = xc[:, :SSD_INNER]

    dtr = dt_ref[...] + dtb_ref[...]
    dt = jnp.maximum(dtr, 0.0) + jnp.log1p(jnp.exp(-jnp.abs(dtr)))
    a = -jnp.exp(alog_ref[...])
    row = lax.broadcasted_iota(jnp.int32, (q, q), 0)
    col = lax.broadcasted_iota(jnp.int32, (q, q), 1)
    causal = row >= col
    tril = jnp.where(causal, 1.0, 0.0).astype(BF16)
    acs = _select_left(tril, dt * a)
    acs_t = acs.T
    expanded = _select_right(jnp.concatenate([acs, dt], axis=0), e_ref[...])
    acs_x = expanded[:q]
    dt_x = expanded[q:]
    xdt = xs * dt_x
    decay_in = jnp.exp(acs_x)
    acs_end = acs_x[q - 1:q, :]
    to_end = jnp.exp(acs_end - acs_x)
    chunk_decay = jnp.exp(acs_end)
    xdt_end = xdt * to_end

    lane_head = lax.broadcasted_iota(jnp.int32, (1, gw), 1) // SSD_HEAD_DIM
    ys = []
    for g in range(SSD_GROUPS):
        bm = xc[:, SSD_INNER + g * SSD_STATE:SSD_INNER + (g + 1) * SSD_STATE].astype(BF16)
        cm = xc[:, SSD_INNER + (SSD_GROUPS + g) * SSD_STATE:
                SSD_INNER + (SSD_GROUPS + g + 1) * SSD_STATE].astype(BF16)
        cb = lax.dot_general(cm, bm, (((1,), (1,)), ((), ())), preferred_element_type=F32)
        xdt_g = xdt[:, g * gw:(g + 1) * gw]
        gmats, rhs = [], []
        for j in range(hpg):
            hh = g * hpg + j
            seg = acs[:, hh:hh + 1] - acs_t[hh:hh + 1, :]
            decay = jnp.exp(jnp.where(causal, seg, -jnp.inf))
            gmats.append((cb * decay).astype(BF16))
            rhs.append(jnp.where(lane_head == j, xdt_g, 0.0).astype(BF16))
        y_diag = _dot(jnp.concatenate(gmats, axis=1), jnp.concatenate(rhs, axis=0))
        h_prev = h_scr[g]
        y_off = _dot(cm, h_prev.astype(BF16)) * decay_in[:, g * gw:(g + 1) * gw]
        upd = lax.dot_general(bm, xdt_end[:, g * gw:(g + 1) * gw].astype(BF16),
                              (((0,), (0,)), ((), ())), preferred_element_type=F32)
        h_scr[g] = chunk_decay[:, g * gw:(g + 1) * gw] * h_prev + upd
        ys.append(y_diag + y_off)
    y = jnp.concatenate(ys, axis=1) + dx_ref[...] * xs
    y = y * _silu(z_ref[...])
    outs = []
    for g in range(SSD_GROUPS):
        yg = y[:, g * gw:(g + 1) * gw]
        outs.append(yg * lax.rsqrt(jnp.mean(yg * yg, axis=-1, keepdims=True) + EPS))
    o_ref[...] = (jnp.concatenate(outs, axis=1) * ng_ref[...]).astype(o_ref.dtype)


def _ssd(z, xbc, dt, conv_w, conv_b, dt_bias, a_log, d_x, norm_gain, expand, bsz, seq):
    m = z.shape[0]
    q = min(SSD_Q, seq)
    nq = seq // q
    nb = math.gcd(bsz, SSD_PER_STEP)
    const = lambda b, t: (0, 0)
    per_seq = lambda v: v.reshape(bsz, seq, v.shape[-1])
    out = pl.pallas_call(
        _ssd_kernel,
        out_shape=jax.ShapeDtypeStruct((bsz, seq, SSD_INNER), BF16),
        grid=(bsz // nb, nq),
        in_specs=[
            pl.BlockSpec((nb, q, SSD_INNER), lambda b, t: (b, t, 0)),
            pl.BlockSpec((nb, q, SSD_XBC), lambda b, t: (b, t, 0)),
            pl.BlockSpec((nb, 8, SSD_XBC), lambda b, t: (b, jnp.maximum(t * (q // 8) - 1, 0), 0)),
            pl.BlockSpec((nb, q, LANES), lambda b, t: (b, t, 0)),
            pl.BlockSpec(conv_w.shape, const),
            pl.BlockSpec(conv_b.shape, const),
            pl.BlockSpec(dt_bias.shape, const),
            pl.BlockSpec(a_log.shape, const),
            pl.BlockSpec(d_x.shape, const),
            pl.BlockSpec(norm_gain.shape, const),
            pl.BlockSpec(expand.shape, const),
        ],
        out_specs=pl.BlockSpec((nb, q, SSD_INNER), lambda b, t: (b, t, 0)),
        scratch_shapes=[pltpu.VMEM((nb, SSD_GROUPS, SSD_STATE, SSD_INNER // SSD_GROUPS), F32)],
        compiler_params=_params("parallel", "arbitrary"),
        name="ssd",
    )(per_seq(z), per_seq(xbc), per_seq(xbc), per_seq(dt), conv_w, conv_b, dt_bias, a_log, d_x,
      norm_gain, expand)
    return out.reshape(m, SSD_INNER)


def _cpow(base_re, base_im, exponent, nbits):
    shape = jnp.broadcast_shapes(base_re.shape, exponent.shape)
    pr = jnp.ones(shape, F32)
    pi = jnp.zeros(shape, F32)
    sr, si = base_re, base_im
    for k in range(nbits):
        bit = ((exponent >> k) & 1) == 1
        nr = pr * sr - pi * si
        ni = pr * si + pi * sr
        pr = jnp.where(bit, nr, pr)
        pi = jnp.where(bit, ni, pi)
        if k + 1 < nbits:
            sr, si = sr * sr - si * si, 2.0 * sr * si
    return pr, pi


def _s5_discretize(ar, ai, log_dt):
    dt = jnp.exp(log_dt)
    mag = jnp.exp(dt * ar)
    lam_re = mag * jnp.cos(dt * ai)
    lam_im = mag * jnp.sin(dt * ai)
    den = ar * ar + ai * ai
    f_re = ((lam_re - 1.0) * ar + lam_im * ai) / den
    f_im = (lam_im * ar - (lam_re - 1.0) * ai) / den
    return lam_re, lam_im, f_re, f_im


def _s5_kernel(u_ref, arow_ref, ldt_ref, bd_ref, cd_re_ref, cd_im_ref, d_ref, y_ref, *scratch):
    n = u_ref.shape[1]
    scr = [scratch[3 * gg:3 * gg + 3] for gg in range(n)]
    lams = [_s5_prepare(arow_ref.at[gg], ldt_ref.at[gg], bd_ref.at[gg], cd_re_ref.at[gg],
                        cd_im_ref.at[gg], *scr[gg], u_ref.shape[-1]) for gg in range(n)]
    for gg in range(n):
        _s5_apply(lams[gg], u_ref.at[:, gg], d_ref.at[gg], y_ref.at[:, gg], *scr[gg][1:])


def _s5_prepare(arow_ref, ldt_ref, bd_ref, cd_re_ref, cd_im_ref, kmat_scr, t_scr, f_scr, q):
    gsz = S5_GROUP
    p2 = 2 * S5_STATE
    nbits = (q - 1).bit_length()
    log_dt = ldt_ref[...]

    lane = lax.broadcasted_iota(jnp.int32, (1, p2), 1)
    first_half_l = lane < S5_STATE
    rowp = lax.broadcasted_iota(jnp.int32, (p2, 1), 0)
    first_half_r = rowp < S5_STATE

    ar_r = arow_ref[0:1, :]
    ai_r = arow_ref[1:2, :]
    lr, li, fr, fi = _s5_discretize(ar_r, ai_r, log_dt)
    sign_b = jnp.where(first_half_l, -1.0, 1.0)
    bd1 = bd_ref[...]
    bd2 = pltpu.roll(bd1, S5_STATE, 1) * sign_b
    bb_a = fr * bd1 + fi * bd2
    bb_b = fr * bd2 - fi * bd1

    srow = lax.broadcasted_iota(jnp.int32, (q, 1), 0)
    pr_re, pr_im = _cpow(lr, li, (q - 1) - srow, nbits)
    for i in range(gsz):
        es = pr_re * bb_a[i:i + 1, :] + pr_im * bb_b[i:i + 1, :]
        t_scr[i * q:(i + 1) * q, gsz * q:gsz * q + p2] = es.astype(BF16)

    pt_re, pt_im = _cpow(lr, li, srow, nbits)
    p1_re = pt_re * lr - pt_im * li
    p1_im = pt_re * li + pt_im * lr
    cre = cd_re_ref[...]
    cim = cd_im_ref[...]
    for o in range(gsz):
        c_r = cre[o:o + 1, :]
        c_i = cim[o:o + 1, :]
        f_o = jnp.where(first_half_l, c_r * p1_re - c_i * p1_im, -(c_r * p1_im + c_i * p1_re))
        f_scr[o * q:(o + 1) * q, :] = f_o.astype(BF16)

    eye = (lax.broadcasted_iota(jnp.int32, (p2, p2), 0)
           == lax.broadcasted_iota(jnp.int32, (p2, p2), 1))
    lcr = jnp.sum(jnp.where(eye, lr, 0.0), axis=1, keepdims=True)
    lci = jnp.sum(jnp.where(eye, li, 0.0), axis=1, keepdims=True)
    dlane = lax.broadcasted_iota(jnp.int32, (1, q), 1)
    pw_re, pw_im = _cpow(lcr, lci, dlane, nbits)
    pow_stack = jnp.where(first_half_r, pw_re, pw_im)

    v1 = bb_a * jnp.where(first_half_l, 1.0, -1.0)
    v2 = -pltpu.roll(bb_a, S5_STATE, 1)
    coef =cre[:, None, :] * v1[None, :, :] + cim[:, None, :] * v2[None, :, :]
    kmat_scr[...] = _dot_exact(coef.reshape(gsz * gsz, p2), pow_stack)

    trow = lax.broadcasted_iota(jnp.int32, (q, q), 0)
    tcol = lax.broadcasted_iota(jnp.int32, (q, q), 1)
    lower = tcol >= trow
    for i in range(gsz):
        for o in range(gsz):
            k_row = kmat_scr[o * gsz + i:o * gsz + i + 1, :]
            blk = pltpu.roll(jnp.broadcast_to(k_row, (q, q)), 0, 1, stride=1, stride_axis=0)
            t_scr[i * q:(i + 1) * q, o * q:(o + 1) * q] = jnp.where(lower, blk, 0.0).astype(BF16)
    return lr, li


def _s5_apply(lam, u_ref, d_ref, y_ref, t_scr, f_scr):
    lr, li = lam
    bsz, rows, q = u_ref.shape
    gsz = S5_GROUP
    nc = rows // gsz
    m = bsz * nc
    sign_b = jnp.where(lax.broadcasted_iota(jnp.int32, (1, 2 * S5_STATE), 1) < S5_STATE, -1.0, 1.0)
    u_f32 = [jnp.concatenate([u_ref[b, pl.ds(i, nc, stride=gsz), :] for b in range(bsz)], axis=0)
             for i in range(gsz)]
    u_cat = jnp.concatenate([v.astype(BF16) for v in u_f32], axis=1)
    acc = _dot(u_cat, t_scr[...])
    x = acc[:, gsz * q:]

    sq_re, sq_im = lr, li
    for _ in range(q.bit_length() - 1):
        sq_re, sq_im = sq_re * sq_re - sq_im * sq_im, 2.0 * sq_re * sq_im
    crow = lax.broadcasted_iota(jnp.int32, (m, 1), 0) % nc
    k = 1
    while k < nc:
        sh = jnp.where(crow >= k, pltpu.roll(x, k, 0), 0.0)
        x = x + sq_re * sh + (sq_im * sign_b) * pltpu.roll(sh, S5_STATE, 1)
        sq_re, sq_im = sq_re * sq_re - sq_im * sq_im, 2.0 * sq_re * sq_im
        k *= 2
    h_prev = jnp.where(crow >= 1, pltpu.roll(x, 1, 0), 0.0)
    y = acc[:, :gsz * q] + lax.dot_general(h_prev.astype(BF16), f_scr[...], (((1,), (1,)), ((), ())),
                                           preferred_element_type=F32)
    for o in range(gsz):
        y_o = y[:, o * q:(o + 1) * q] + d_ref[o:o + 1, :] * u_f32[o]
        for b in range(bsz):
            y_ref[b, pl.ds(o, nc, stride=gsz), :] = y_o[b * nc:(b + 1) * nc]


def _s5(u5, arow, ldt, bd, cd_re, cd_im, d_rows):
    bsz, _, nc, _, q = u5.shape
    rows = nc * S5_GROUP
    p2 = 2 * S5_STATE
    per_g = lambda g: (g, 0, 0)
    n = S5_PER_STEP
    assert S5_GROUPS % n == 0
    y4 = pl.pallas_call(
        _s5_kernel,
        out_shape=jax.ShapeDtypeStruct((bsz, S5_GROUPS, rows, q), F32),
        grid=(S5_GROUPS // n,),
        in_specs=[
            pl.BlockSpec((bsz, n, rows, q), lambda g: (0, g, 0, 0)),
            pl.BlockSpec((n, 2, p2), per_g),
            pl.BlockSpec((n, 1, 1), per_g),
            pl.BlockSpec((n, S5_GROUP, p2), per_g),
            pl.BlockSpec((n, S5_GROUP, p2), per_g),
            pl.BlockSpec((n, S5_GROUP, p2), per_g),
            pl.BlockSpec((n, S5_GROUP, q), per_g),
        ],
        out_specs=pl.BlockSpec((bsz, n, rows, q), lambda g: (0, g, 0, 0)),
        scratch_shapes=n * [
            pltpu.VMEM((S5_GROUP * S5_GROUP, q), F32),
            pltpu.VMEM((S5_GROUP * q, S5_GROUP * q + p2), BF16),
            pltpu.VMEM((S5_GROUP * q, p2), BF16),
        ],
        compiler_params=_params("parallel"),
        name="s5",
    )(u5.reshape(bsz, S5_GROUPS, rows, q), arow, ldt, bd, cd_re, cd_im, d_rows)
    return y4.reshape(u5.shape)


def _gelu_tanh(x):
    return 0.5 * x * (1.0 + jnp.tanh(0.7978845608028654 * (x + 0.044715 * (x * x * x))))


def _outproj_even_ffn_kernel(x_ref, ya_ref, ybt_ref, wglut_ref, bglu_ref, wa_ref, wb_ref,
                             fg_ref, wg_ref, wu_ref, wd_ref, o_ref, wg_s, wu_s, wd_s, act_scr):
    i = pl.program_id(0)
    _ffn_stage(i, wg_ref, wu_ref, wd_ref, wg_s, wu_s, wd_s)

    @pl.when(i >= wg_s.shape[0])
    def _():
        q = ybt_ref.shape[-1]
        ybt = jnp.concatenate([ybt_ref[0, :, cc].reshape(S5_WIDTH, q) for cc in range(ybt_ref.shape[2])],
                              axis=1)
        gate = _dot(wglut_ref[...], _gelu_tanh(ybt).astype(BF16)) + bglu_ref[...]
        yb = (ybt * jax.nn.sigmoid(gate)).T.astype(BF16)
        x2 = x_ref[...] + _dot(ya_ref[...], wa_ref[...]) + _dot(yb, wb_ref[...])
        o_ref[...] = _ffn_apply(x2, fg_ref, wg_s, wu_s, wd_s, act_scr)


def _outproj_even_ffn(x2d, ya, yb5, wglut, bglu, wa, wb, ffn, layer, bsz, seq):
    m, d = x2d.shape
    tm = min(FFN_TM, seq)
    per_b = seq // tm
    q = yb5.shape[-1]
    ns = _ffn_chunks(ffn)
    tile = lambda i: jnp.maximum(i - ns, 0)
    return pl.pallas_call(
        _outproj_even_ffn_kernel,
        out_shape=jax.ShapeDtypeStruct((m, d), F32),
        grid=(ns + m // tm,),
        in_specs=[
            pl.BlockSpec((tm, d), lambda i: (tile(i), 0)),
            pl.BlockSpec((tm, SSD_INNER), lambda i: (tile(i), 0)),
            pl.BlockSpec((1, S5_GROUPS, tm // q, S5_GROUP, q),
                         lambda i: (tile(i) // per_b, 0, tile(i) % per_b, 0, 0)),
            _resident(wglut.shape), _resident(bglu.shape), _resident(wa.shape), _resident(wb.shape),
        ] + _ffn_specs(*ffn, layer),
        out_specs=pl.BlockSpec((tm, d), lambda i: (tile(i), 0)),
        scratch_shapes=_ffn_scratch(ffn, tm),
        compiler_params=_params("arbitrary"),
        name="outproj_even_ffn",
    )(x2d, ya, yb5, wglut, bglu, wa, wb, *ffn)


def _head_rmsnorm(x, gain):
    assert LANES == 2 * ATT_HEAD_DIM
    first = lax.broadcasted_iota(jnp.int32, (1, LANES), 1) < ATT_HEAD_DIM
    tiles = []
    for p in range(x.shape[1] // LANES):
        xp = x[:, p * LANES:(p + 1) * LANES]
        sq = xp * xp
        s0 = jnp.sum(jnp.where(first, sq, 0.0), axis=-1, keepdims=True)
        s1 = jnp.sum(jnp.where(first, 0.0, sq), axis=-1, keepdims=True)
        ss = jnp.where(first, s0, s1)
        tiles.append(xp * lax.rsqrt(ss * (1.0 / ATT_HEAD_DIM) + EPS))
    return jnp.concatenate(tiles, axis=1) * gain


def _head_rmsnorm_t(xt, gain_col):
    c, n = xt.shape
    x3 = xt.reshape(c // ATT_HEAD_DIM, ATT_HEAD_DIM, n)
    ms = jnp.mean(x3 * x3, axis=1, keepdims=True)
    return (x3 * lax.rsqrt(ms + EPS)).reshape(c, n) * gain_col


def _pool_apply(u, prev, pos, w_ref, sc_ref):
    ext = jnp.concatenate([prev, u], axis=0)
    sums = {}
    s = ext
    w = 1
    while w < POOL_MAX:
        s = s + pltpu.roll(s, w, 0)
        w *= 2
        sums[w] = s[POOL_MAX:]
    outs = []
    for g, win in enumerate(POOL_WINDOWS):
        sl = slice(g * POOL_GROUP, (g + 1) * POOL_GROUP)
        count = jnp.minimum(pos + 1, win).astype(F32)
        pooled = sums[win][:, sl] / count - u[:, sl]
        outs.append(_dot(pooled.astype(BF16), w_ref[g]))
    return jnp.concatenate(outs, axis=1) * sc_ref[...]


def _ffn_inproj_odd_kernel(x_ref, fg_ref, wg_ref, wu_ref, wd_ref, g_ref, wqt_ref, wk_ref, wvt_ref,
                           wup_ref, qg_ref, kg_ref, pw_ref, ps_ref,
                           x1_ref, qt_ref, k_ref, vt_ref, yd_ref, wg_s, wu_s, wd_s, act_scr, carry_scr,
                           *, tiles_per_seq):
    i = pl.program_id(0)
    tm = x_ref.shape[0]
    n_stage = wg_s.shape[0]
    _ffn_stage(i, wg_ref, wu_ref, wd_ref, wg_s, wu_s, wd_s)

    @pl.when(i == 0)
    def _():
        carry_scr[...] = jnp.zeros_like(carry_scr)

    @pl.when(i >= n_stage)
    def _():
        x1 = _ffn_apply(x_ref[...], fg_ref, wg_s, wu_s, wd_s, act_scr)
        x1_ref[...] = x1
        h = _rmsnorm_bf16(x1, g_ref[...])
        nt = (((1,), (1,)), ((), ()))
        qt = lax.dot_general(wqt_ref[...], h, nt, preferred_element_type=F32)
        qt = _head_rmsnorm_t(qt, qg_ref[...]) * (ATT_HEAD_DIM ** -0.5 * LOG2E)
        qt_ref[0] = qt.astype(BF16)
        k_ref[...] = _head_rmsnorm(_dot(h, wk_ref[...]), kg_ref[...]).astype(BF16)
        vt_ref[0] = lax.dot_general(wvt_ref[...], h, nt, preferred_element_type=F32).astype(BF16)
        u = _dot(h, wup_ref[...])
        tile_in_seq = (i - n_stage) % tiles_per_seq
        prev = jnp.where(tile_in_seq == 0, 0.0, carry_scr[...])
        pos = tile_in_seq * tm + lax.broadcasted_iota(jnp.int32, (tm, 1), 0)
        yd_ref[...] = _pool_apply(u, prev, pos, pw_ref, ps_ref).astype(yd_ref.dtype)
        carry_scr[...] = u[tm - POOL_MAX:]


def _ffn_inproj_odd(x2d, ffn, gain, wqt, wk, wvt, wup, qg_col, kg, w_pool, pool_scale, layer, bsz, seq):
    m, d = x2d.shape
    tm = min(FFN_TM, seq)
    per_b = seq // tm
    w = ATT_WIDTH
    ns = _ffn_chunks(ffn)
    tl = lambda i: jnp.maximum(i - ns, 0)
    tile = pl.BlockSpec((tm, w), lambda i: (tl(i), 0))
    tile_t = pl.BlockSpec((1, w, tm), lambda i: (tl(i) // per_b, 0, tl(i) % per_b))
    row_tile = pl.BlockSpec((tm, d), lambda i: (tl(i), 0))
    tok = jax.ShapeDtypeStruct((m, w), BF16)
    chan = jax.ShapeDtypeStruct((bsz, w, seq), BF16)
    return pl.pallas_call(
        functools.partial(_ffn_inproj_odd_kernel, tiles_per_seq=per_b),
        out_shape=(jax.ShapeDtypeStruct((m, d), F32), chan, tok, chan, tok),
        grid=(ns + m // tm,),
        in_specs=[row_tile] + _ffn_specs(*ffn, layer) + [
            pl.BlockSpec((None, 1, d), lambda i: (layer, 0, 0)),
            _resident(wqt.shape), _resident(wk.shape), _resident(wvt.shape), _resident(wup.shape),
            _resident(qg_col.shape), _resident(kg.shape),
            _resident(w_pool.shape), _resident(pool_scale.shape),
        ],
        out_specs=(row_tile, tile_t, tile, tile_t, tile),
        scratch_shapes=_ffn_scratch(ffn, tm) + [pltpu.VMEM((POOL_MAX, w), F32)],
        compiler_params=_params("arbitrary"),
        name="ffn_inproj_odd",
    )(x2d, *ffn, gain, wqt, wk, wvt, wup, qg_col, kg, w_pool, pool_scale)


def _attn_kernel(qt_ref, k0_ref, k1_ref, k2_ref, vt0_ref, vt1_ref, vt2_ref, brow_ref, o_ref, bias_ref):
    t = pl.program_id(1)
    tq = o_ref.shape[0]
    nk = 3 * tq

    @pl.when((pl.program_id(0) == 0) & (t == 0))
    def _():
        keyi = lax.broadcasted_iota(jnp.int32, (nk, tq), 0)
        qryi = lax.broadcasted_iota(jnp.int32, (nk, tq), 1)
        rel_chunk = keyi // ATT_CHUNK - (2 * tq // ATT_CHUNK - LEFT_CHUNKS) - qryi // ATT_CHUNK
        rel_chunk = jnp.where(rel_chunk >= 0, rel_chunk, LEFT_CHUNKS + 1)
        for hh in range(ATT_HEADS):
            base = jnp.broadcast_to(brow_ref[hh:hh + 1, :], (nk, 4 * tq))
            shifted = pltpu.roll(base, 0, 1, stride=1, stride_axis=0)
            bias = shifted[:, nk:] * LOG2E
            for var in range(3):
                first_key = (2 - var) * tq
                keep = jnp.where(keyi >= first_key, rel_chunk, LEFT_CHUNKS + 1) <= LEFT_CHUNKS
                bias_ref[var, hh] = jnp.where(keep, bias, -jnp.inf)

    var = jnp.minimum(t, 2)
    qt = qt_ref[0]
    kcat = jnp.concatenate([k0_ref[...], k1_ref[...], k2_ref[...]], axis=0)
    vtcat = jnp.concatenate([vt0_ref[0], vt1_ref[0], vt2_ref[0]], axis=1)
    hd = ATT_HEAD_DIM
    heads_per_slab = LANES // hd
    no_q = jnp.zeros((LANES - hd, tq), qt.dtype)
    ones_rows = jnp.ones((16, nk), qt.dtype)
    n_tiles = nk // tq

    def scores(head, j):
        p, hh = divmod(head, heads_per_slab)
        rows = slice(head * hd, (head + 1) * hd)
        qth = jnp.concatenate([qt[rows]] + [no_q] if hh == 0 else [no_q] + [qt[rows]], axis=0)
        ks = slice(j * tq, (j + 1) * tq)
        return _dot(kcat[ks, p * LANES:(p + 1) * LANES], qth) + bias_ref[var, head, ks, :]

    items = [(head, j) for head in range(ATT_HEADS) for j in range(n_tiles)]
    ahead = [scores(*it) for it in items[:ATT_LOOKAHEAD]]
    outs, parts, tops = [], [], []
    for idx, (head, j) in enumerate(items):
        st = ahead.pop(0)
        if idx + ATT_LOOKAHEAD < len(items):
            ahead.append(scores(*items[idx + ATT_LOOKAHEAD]))
        rows = slice(head * hd, (head + 1) * hd)
        ks = slice(j * tq, (j + 1) * tq)
        top = jnp.maximum(jnp.max(st, axis=0, keepdims=True), MASKED_FLOOR)
        e = jnp.exp2(st - top).astype(BF16)
        vth = jnp.concatenate([vtcat[rows, ks], ones_rows[:, ks]], axis=0)
        parts.append(_dot(vth, e))
        tops.append(top)
        if j == n_tiles - 1:
            top_all = functools.reduce(jnp.maximum, tops)
            ot = sum(part * jnp.exp2(tp - top_all) for part, tp in zip(parts, tops))
            outs.append(ot[:hd] / ot[hd:hd + 1])
            parts, tops = [], []
            if head % heads_per_slab == heads_per_slab - 1:
                p = head // heads_per_slab
                o_ref[:, p * LANES:(p + 1) * LANES] = jnp.concatenate(outs, axis=0).T.astype(o_ref.dtype)
                outs = []


def _attention(qt, k, vt, brow, bsz, seq):
    m, w = k.shape
    tq = ATT_T
    assert 2 * tq >= LEFT_CHUNKS * ATT_CHUNK and tq > MAX_REL and seq % tq == 0
    nt = seq // tq
    back = lambda n: (lambda b, t: (b * nt + jnp.maximum(t - n, 0), 0))
    back_t = lambda n: (lambda b, t: (b, 0, jnp.maximum(t - n, 0)))
    tok = lambda f: pl.BlockSpec((tq, w), f)
    chan = lambda f: pl.BlockSpec((1, w, tq), f)
    return pl.pallas_call(
        _attn_kernel,
        out_shape=jax.ShapeDtypeStruct((m, w), BF16),
        grid=(bsz, nt),
        in_specs=[chan(back_t(0)), tok(back(2)), tok(back(1)), tok(back(0)),
                  chan(back_t(2)), chan(back_t(1)), chan(back_t(0)),
                  pl.BlockSpec(brow.shape, lambda b, t: (0, 0))],
        out_specs=tok(back(0)),
        scratch_shapes=[pltpu.VMEM((3, ATT_HEADS, 3 * tq, tq), F32)],
        compiler_params=_params("arbitrary", "arbitrary"),
        name="band_attention",
    )(qt, k, k, k, vt, vt, vt, brow)


def _bias_rows(rel_bias, tq):
    rb = rel_bias.astype(F32)
    nh = rb.shape[0]
    near = jnp.broadcast_to(rb[:, :1], (nh, tq - MAX_REL))
    far = jnp.broadcast_to(rb[:, -1:], (nh, 3 * tq - MAX_REL - 1))
    return jnp.concatenate([near, rb, far], axis=1)


def _outproj_odd_ffn_kernel(x_ref, yc_ref, yd_ref, wa_ref, wb_ref, fg_ref, wg_ref, wu_ref, wd_ref,
                            o_ref, wg_s, wu_s, wd_s, act_scr):
    i = pl.program_id(0)
    _ffn_stage(i, wg_ref, wu_ref, wd_ref, wg_s, wu_s, wd_s)

    @pl.when(i >= wg_s.shape[0])
    def _():
        x2 = x_ref[...] + _dot(yc_ref[...], wa_ref[...]) + _dot(yd_ref[...], wb_ref[...])
        o_ref[...] = _ffn_apply(x2, fg_ref, wg_s, wu_s, wd_s, act_scr)


def _outproj_odd_ffn(x2d, yc, yd, wa, wb, ffn, layer):
    m, d = x2d.shape
    tm = min(FFN_TM, m)
    ns = _ffn_chunks(ffn)
    tile = lambda i: jnp.maximum(i - ns, 0)
    return pl.pallas_call(
        _outproj_odd_ffn_kernel,
        out_shape=jax.ShapeDtypeStruct((m, d), F32),
        grid=(ns + m // tm,),
        in_specs=[
            pl.BlockSpec((tm, d), lambda i: (tile(i), 0)),
            pl.BlockSpec((tm, ATT_WIDTH), lambda i: (tile(i), 0)),
            pl.BlockSpec((tm, ATT_WIDTH), lambda i: (tile(i), 0)),
            _resident(wa.shape), _resident(wb.shape),
        ] + _ffn_specs(*ffn, layer),
        out_specs=pl.BlockSpec((tm, d), lambda i: (tile(i), 0)),
        scratch_shapes=_ffn_scratch(ffn, tm),
        compiler_params=_params("arbitrary"),
        name="outproj_odd_ffn",
    )(x2d, yc, yd, wa, wb, *ffn)


def _even_layer(x2d, ffn1, ffn2, gain, layer, bsz, seq, even_w_in, even_w_out, ssd_conv_w, ssd_conv_b,
                ssd_dt_bias, ssd_a_log, ssd_d, ssd_norm, s5_a_re, s5_a_im, s5_log_dt, s5_b_re, s5_b_im,
                s5_c_re, s5_c_im, s5_d, s5_w_glu, s5_b_glu):
    i = layer // 2
    w_in = even_w_in[i]
    o1 = SSD_INNER
    o2 = o1 + SSD_XBC
    o3 = o2 + SSD_HEADS
    wz = w_in[:, :o1].astype(BF16)
    wx = w_in[:, o1:o2].astype(BF16)
    wdt = jnp.pad(w_in[:, o2:o3], ((0, 0), (0, LANES - SSD_HEADS))).astype(BF16)
    wut = w_in[:, o3:].T.astype(BF16)
    x1, z, xbc, dt, ut = _ffn_inproj_even(x2d, ffn1, gain, wz, wx, wdt, wut, layer, bsz, seq)

    pad_h = lambda v: jnp.pad(v.astype(F32), (0, LANES - SSD_HEADS))[None, :]
    expand = (jnp.arange(LANES)[:, None] == (jnp.arange(SSD_INNER)[None, :] // SSD_HEAD_DIM)).astype(BF16)
    ya = _ssd(z, xbc, dt, ssd_conv_w[i], ssd_conv_b[i][None, :], pad_h(ssd_dt_bias[i]),
              pad_h(ssd_a_log[i]), jnp.repeat(ssd_d[i], SSD_HEAD_DIM)[None, :], ssd_norm[i][None, :],
              expand, bsz, seq)

    q = min(S5_Q, seq)
    are, aim = s5_a_re[i], s5_a_im[i]
    dup = lambda v: jnp.concatenate([v, v], axis=-1)
    arow = jnp.stack([dup(are), dup(aim)], axis=1)
    ldt = s5_log_dt[i][:, None, None]
    bd = jnp.concatenate([jnp.swapaxes(s5_b_re[i], 1, 2), jnp.swapaxes(s5_b_im[i], 1, 2)], axis=-1)
    cd_re, cd_im = dup(s5_c_re[i]), dup(s5_c_im[i])
    d_rows = jnp.broadcast_to(s5_d[i][:, :, None], (S5_GROUPS, S5_GROUP, q))
    yb5 = _s5(ut, arow, ldt, bd, cd_re, cd_im, d_rows)

    w_out = even_w_out[i]
    return _outproj_even_ffn(x1, ya, yb5, s5_w_glu[i].T.astype(BF16), s5_b_glu[i][:, None],
                             w_out[:SSD_INNER].astype(BF16), w_out[SSD_INNER:].astype(BF16),
                             ffn2, layer, bsz, seq)


def _odd_layer(x2d, ffn1, ffn2, gain, layer, bsz, seq, odd_w_in, odd_w_out, attn_q_norm, attn_k_norm,
               attn_rel_bias, pool_w, pool_scale):
    i = layer // 2
    w_in = odd_w_in[i].astype(BF16)
    w = ATT_WIDTH
    tile_gain = lambda v: jnp.tile(v.astype(F32), ATT_HEADS)[None, :]
    x1, qt, k, vt, yd = _ffn_inproj_odd(
        x2d, ffn1, gain, w_in[:, :w].T, w_in[:, w:2 * w], w_in[:, 2 * w:3 * w].T, w_in[:, 3 * w:],
        tile_gain(attn_q_norm[i]).T, tile_gain(attn_k_norm[i]),
        pool_w[i].astype(BF16), pool_scale[i][None, :], layer, bsz, seq)
    yc = _attention(qt, k, vt, _bias_rows(attn_rel_bias[i], ATT_T), bsz, seq)
    w_out = odd_w_out[i]
    return _outproj_odd_ffn(x1, yc, yd, w_out[:w].astype(BF16), w_out[w:].astype(BF16), ffn2, layer)


def kernel(x, ffn1_norm, ffn1_w_gate, ffn1_w_up, ffn1_w_down, mix_norm, even_w_in, even_w_out, ssd_conv_w, ssd_conv_b, ssd_dt_bias, ssd_a_log, ssd_d, ssd_norm, s5_a_re, s5_a_im, s5_log_dt, s5_b_re, s5_b_im, s5_c_re, s5_c_im, s5_d, s5_w_glu, s5_b_glu, odd_w_in, odd_w_out, attn_q_norm, attn_k_norm, attn_rel_bias, pool_w, pool_scale, ffn2_norm, ffn2_w_gate, ffn2_w_up, ffn2_w_down):
    bsz, seq, d = x.shape
    depth = ffn1_norm.shape[0]
    x2d = x.reshape(bsz * seq, d)
    g3 = lambda v: v.astype(F32)[:, None, :]
    f1n, mxn, f2n = g3(ffn1_norm), g3(mix_norm), g3(ffn2_norm)
    as32 = lambda v: v.astype(F32)
    ffn1 = (f1n, as32(ffn1_w_gate), as32(ffn1_w_up), as32(ffn1_w_down))
    ffn2 = (f2n, as32(ffn2_w_gate), as32(ffn2_w_up), as32(ffn2_w_down))
    for layer in range(depth):
        if layer % 2 == 0:
            x2d = _even_layer(x2d, ffn1, ffn2, mxn, layer, bsz, seq, even_w_in, even_w_out, ssd_conv_w,
                              ssd_conv_b, ssd_dt_bias, ssd_a_log, ssd_d, ssd_norm, s5_a_re, s5_a_im,
                              s5_log_dt, s5_b_re, s5_b_im, s5_c_re, s5_c_im, s5_d, s5_w_glu, s5_b_glu)
        else:
            x2d = _odd_layer(x2d, ffn1, ffn2, mxn, layer, bsz, seq, odd_w_in, odd_w_out, attn_q_norm,
                             attn_k_norm, attn_rel_bias, pool_w, pool_scale)
    return x2d.reshape(bsz, seq, d)
```

```python
import functools
import math

import jax
import jax.numpy as jnp
from jax import lax
from jax.experimental import pallas as pl
from jax.experimental.pallas import tpu as pltpu

F32 = jnp.float32
BF16 = jnp.bfloat16
HIGHEST = lax.Precision.HIGHEST

EPS = 1e-6
LOG2E = 1.4426950408889634
MASKED_FLOOR = -1e30
LANES = 128
VMEM_LIMIT = 56 * 1024 * 1024

SSD_HEADS = 8
SSD_HEAD_DIM = 64
SSD_INNER = 512
SSD_GROUPS = 2
SSD_STATE = 128
SSD_CONV = 4
SSD_XBC = 1024
S5_WIDTH = 512
S5_GROUP = 16
S5_GROUPS = 32
S5_STATE = 64
ATT_HEADS = 8
ATT_HEAD_DIM = 64
ATT_WIDTH = 512
ATT_CHUNK = 64
LEFT_CHUNKS = 8
MAX_REL = 128
POOL_WINDOWS = (2, 4, 8, 16)
POOL_GROUP = 128
POOL_MAX = 16

FFN_TM = 512
OUT_TM = 1024
FFN_TF = 256
SSD_Q = 128
SSD_PER_STEP = 4
S5_Q = 128
S5_PER_STEP = 2
ATT_T = 256
ATT_LOOKAHEAD = 5


def _params(*sem):
    return pltpu.CompilerParams(dimension_semantics=sem, vmem_limit_bytes=VMEM_LIMIT)


def _dot(a, b):
    return jnp.dot(a, b, preferred_element_type=F32)


def _dot_exact(a, b):
    return jnp.dot(a, b, preferred_element_type=F32, precision=HIGHEST)


def _split3(x):
    hi = x.astype(BF16)
    rest = x - hi.astype(F32)
    mid = rest.astype(BF16)
    lo = (rest - mid.astype(F32)).astype(BF16)
    return hi, mid, lo


def _select_left(sel, x):
    hi, mid, lo = _split3(x)
    return _dot(sel, hi) + _dot(sel, mid) + _dot(sel, lo)


def _select_right(x, sel):
    hi, mid, lo = _split3(x)
    return _dot(hi, sel) + _dot(mid, sel) + _dot(lo, sel)


def _rmsnorm_bf16(x, gain):
    ms = jnp.mean(x * x, axis=-1, keepdims=True)
    return (x * lax.rsqrt(ms + EPS) * gain).astype(BF16)


def _silu(x):
    return x * jax.nn.sigmoid(x)


def _ffn_stage(i, wg_ref, wu_ref, wd_ref, wg_s, wu_s, wd_s):
    @pl.when(i < wg_s.shape[0])
    def _():
        wg_s[i] = wg_ref[...].astype(BF16)
        wu_s[i] = wu_ref[...].astype(BF16)
        wd_s[pl.ds(pl.multiple_of(i * FFN_TF, FFN_TF), FFN_TF), :] = wd_ref[...].astype(BF16)


def _ffn_apply(x, g_ref, wg_s, wu_s, wd_s, act_scr):
    h = _rmsnorm_bf16(x, g_ref[...])
    for c in range(wg_s.shape[0]):
        gate = _dot(h, wg_s[c])
        up = _dot(h, wu_s[c])
        act_scr[:, c * FFN_TF:(c + 1) * FFN_TF] = (_silu(gate) * up).astype(BF16)
    return x + 0.5 * _dot(act_scr[...], wd_s[...])


def _resident(shape):
    return pl.BlockSpec(shape, lambda i: (0,) * len(shape), pipeline_mode=pl.Buffered(1))


def _ffn_chunks(ffn):
    dff = ffn[1].shape[-1]
    assert dff % FFN_TF == 0
    return dff // FFN_TF


def _ffn_specs(gain, wg, wu, wd, layer):
    d, dff = wg.shape[1:]
    last = dff // FFN_TF - 1
    return [
        pl.BlockSpec((None, 1, d), lambda i: (layer, 0, 0)),
        pl.BlockSpec((None, d, FFN_TF), lambda i: (layer, 0, jnp.minimum(i, last))),
        pl.BlockSpec((None, d, FFN_TF), lambda i: (layer, 0, jnp.minimum(i, last))),
        pl.BlockSpec((None, FFN_TF, d), lambda i: (layer, jnp.minimum(i, last), 0)),
    ]


def _ffn_scratch(ffn, tm):
    d, dff = ffn[1].shape[1:]
    n = dff // FFN_TF
    return [pltpu.VMEM((n, d, FFN_TF), BF16), pltpu.VMEM((n, d, FFN_TF), BF16),
            pltpu.VMEM((dff, d), BF16), pltpu.VMEM((tm, dff), BF16)]


def _ffn_inproj_even_kernel(x_ref, fg_ref, wg_ref, wu_ref, wd_ref, g_ref, wz_ref, wx_ref,
                            wut_ref, x1_ref, z_ref, xbc_ref, dtt_ref, ut_ref, wg_s, wu_s, wd_s, act_scr):
    i = pl.program_id(0)
    _ffn_stage(i, wg_ref, wu_ref, wd_ref, wg_s, wu_s, wd_s)

    @pl.when(i >= wg_s.shape[0])
    def _():
        x1 = _ffn_apply(x_ref[...], fg_ref, wg_s, wu_s, wd_s, act_scr)
        x1_ref[...] = x1
        h = _rmsnorm_bf16(x1, g_ref[...])
        z_ref[...] = _dot(h, wz_ref[...])
        xbc_ref[...] = _dot(h, wx_ref[...])
        ut = lax.dot_general(wut_ref[...], h, (((1,), (1,)), ((), ())), preferred_element_type=F32)
        dtt_ref[0] = ut[S5_WIDTH:S5_WIDTH + SSD_HEADS, :]
        q = ut_ref.shape[-1]
        for cc in range(ut_ref.shape[2]):
            ut_ref[0, :, cc] = ut[:S5_WIDTH, cc * q:(cc + 1) * q].reshape(S5_GROUPS, S5_GROUP, q)


def _ffn_inproj_even(x2d, ffn, gain, wz, wx, wut, layer, bsz, seq):
    m, d = x2d.shape
    tm = min(FFN_TM, seq)
    per_b = seq // tm
    q = min(S5_Q, seq)
    ns = _ffn_chunks(ffn)
    tile = lambda i: jnp.maximum(i - ns, 0)
    return pl.pallas_call(
        _ffn_inproj_even_kernel,
        out_shape=(
            jax.ShapeDtypeStruct((m, d), F32),
            jax.ShapeDtypeStruct((m, SSD_INNER), F32),
            jax.ShapeDtypeStruct((m, SSD_XBC), F32),
            jax.ShapeDtypeStruct((bsz, SSD_HEADS, seq), F32),
            jax.ShapeDtypeStruct((bsz, S5_GROUPS, seq // q, S5_GROUP, q), F32),
        ),
        grid=(ns + m // tm,),
        in_specs=[pl.BlockSpec((tm, d), lambda i: (tile(i), 0))] + _ffn_specs(*ffn, layer) + [
            pl.BlockSpec((None, 1, d), lambda i: (layer, 0, 0)),
            _resident(wz.shape), _resident(wx.shape), _resident(wut.shape),
        ],
        out_specs=(
            pl.BlockSpec((tm, d), lambda i: (tile(i), 0)),
            pl.BlockSpec((tm, SSD_INNER), lambda i: (tile(i), 0)),
            pl.BlockSpec((tm, SSD_XBC), lambda i: (tile(i), 0)),
            pl.BlockSpec((1, SSD_HEADS, tm), lambda i: (tile(i) // per_b, 0, tile(i) % per_b)),
            pl.BlockSpec((1, S5_GROUPS, tm // q, S5_GROUP, q),
                         lambda i: (tile(i) // per_b, 0, tile(i) % per_b, 0, 0)),
        ),
        scratch_shapes=_ffn_scratch(ffn, tm),
        compiler_params=_params("arbitrary"),
        name="ffn_inproj_even",
    )(x2d, *ffn, gain, wz, wx, wut)


def _shift_rows(cur, prev8, j):
    rolled = pltpu.roll(cur, j, 0)
    prev_rolled = pltpu.roll(prev8, j, 0)
    row = lax.broadcasted_iota(jnp.int32, prev8.shape, 0)
    top = jnp.where(row < j, prev_rolled, rolled[:8])
    return jnp.concatenate([top, rolled[8:]], axis=0)


def _ssd_kernel(z_ref, xbc_ref, xprev_ref, dt_ref, *rest):
    consts, o_ref, h_scr = rest[:-2], rest[-2], rest[-1]

    @pl.when(pl.program_id(1) == 0)
    def _():
        h_scr[...] = jnp.zeros_like(h_scr)

    for bb in range(z_ref.shape[0]):
        _ssd_chunk(z_ref.at[bb], xbc_ref.at[bb], xprev_ref.at[bb], dt_ref.at[bb], *consts,
                   o_ref.at[bb], h_scr.at[bb])


def _ssd_chunk(z_ref, xbc_ref, xprev_ref, dt_ref, cw_ref, cb_ref, dtb_ref, alog_ref,
               dx_ref, ng_ref, e_ref, o_ref, h_scr):
    t = pl.program_id(1)
    q = z_ref.shape[0]
    gw = SSD_INNER // SSD_GROUPS
    hpg = SSD_HEADS // SSD_GROUPS

    xbc = xbc_ref[...]
    prev = jnp.where(t > 0, xprev_ref[...], 0.0)
    conv = xbc * cw_ref[SSD_CONV - 1:SSD_CONV, :] + cb_ref[...]
    for j in range(1, SSD_CONV):
        conv = conv + _shift_rows(xbc, prev, j) * cw_ref[SSD_CONV - 1 - j:SSD_CONV - j, :]
    xc = _silu(conv)
    xs = xc[:, :SSD_INNER]

    dtr = dt_ref[...] + dtb_ref[...]
    dt_t = jnp.maximum(dtr, 0.0) + jnp.log1p(jnp.exp(-jnp.abs(dtr)))
    a = -jnp.exp(alog_ref[...])
    row = lax.broadcasted_iota(jnp.int32, (q, q), 0)
    col = lax.broadcasted_iota(jnp.int32, (q, q), 1)
    causal = row >= col
    triu = jnp.where(row <= col, 1.0, 0.0).astype(BF16)
    acs_t = _select_right(dt_t * a, triu)
    pad = jnp.zeros((LANES - SSD_HEADS, q), F32)
    acs = jnp.concatenate([acs_t, pad], axis=0).T
    dt = jnp.concatenate([dt_t, pad], axis=0).T
    expanded = _select_right(jnp.concatenate([acs, dt], axis=0), e_ref[...])
    acs_x = expanded[:q]
    dt_x = expanded[q:]
    xdt = xs * dt_x
    decay_in = jnp.exp(acs_x)
    acs_end = acs_x[q - 1:q, :]
    to_end = jnp.exp(acs_end - acs_x)
    chunk_decay = jnp.exp(acs_end)
    xdt_end = xdt * to_end

    lane_head = lax.broadcasted_iota(jnp.int32, (1, gw), 1) // SSD_HEAD_DIM
    ys = []
    for g in range(SSD_GROUPS):
        bm = xc[:, SSD_INNER + g * SSD_STATE:SSD_INNER + (g + 1) * SSD_STATE].astype(BF16)
        cm = xc[:, SSD_INNER + (SSD_GROUPS + g) * SSD_STATE:
                SSD_INNER + (SSD_GROUPS + g + 1) * SSD_STATE].astype(BF16)
        cb = lax.dot_general(cm, bm, (((1,), (1,)), ((), ())), preferred_element_type=F32)
        xdt_g = xdt[:, g * gw:(g + 1) * gw]
        gmats, rhs = [], []
        for j in range(hpg):
            hh = g * hpg + j
            seg = acs[:, hh:hh + 1] - acs_t[hh:hh + 1, :]
            decay = jnp.exp(jnp.where(causal, seg, -jnp.inf))
            gmats.append((cb * decay).astype(BF16))
            rhs.append(jnp.where(lane_head == j, xdt_g, 0.0).astype(BF16))
        y_diag = _dot(jnp.concatenate(gmats, axis=1), jnp.concatenate(rhs, axis=0))
        h_prev = h_scr[g]
        y_off = _dot(cm, h_prev.astype(BF16)) * decay_in[:, g * gw:(g + 1) * gw]
        upd = lax.dot_general(bm, xdt_end[:, g * gw:(g + 1) * gw].astype(BF16),
                              (((0,), (0,)), ((), ())), preferred_element_type=F32)
        h_scr[g] = chunk_decay[:, g * gw:(g + 1) * gw] * h_prev + upd
        ys.append(y_diag + y_off)
    y = jnp.concatenate(ys, axis=1) + dx_ref[...] * xs
    y = y * _silu(z_ref[...])
    outs = []
    for g in range(SSD_GROUPS):
        yg = y[:, g * gw:(g + 1) * gw]
        outs.append(yg * lax.rsqrt(jnp.mean(yg * yg, axis=-1, keepdims=True) + EPS))
    o_ref[...] = (jnp.concatenate(outs, axis=1) * ng_ref[...]).astype(o_ref.dtype)


def _ssd(z, xbc, dt, conv_w, conv_b, dt_bias, a_log, d_x, norm_gain, expand, bsz, seq):
    m = z.shape[0]
    q = min(SSD_Q, seq)
    nq = seq // q
    nb = math.gcd(bsz, SSD_PER_STEP)
    const = lambda b, t: (0, 0)
    per_seq = lambda v: v.reshape(bsz, seq, v.shape[-1])
    out = pl.pallas_call(
        _ssd_kernel,
        out_shape=jax.ShapeDtypeStruct((bsz, seq, SSD_INNER), BF16),
        grid=(bsz // nb, nq),
        in_specs=[
            pl.BlockSpec((nb, q, SSD_INNER), lambda b, t: (b, t, 0)),
            pl.BlockSpec((nb, q, SSD_XBC), lambda b, t: (b, t, 0)),
            pl.BlockSpec((nb, 8, SSD_XBC), lambda b, t: (b, jnp.maximum(t * (q // 8) - 1, 0), 0)),
            pl.BlockSpec((nb, SSD_HEADS, q), lambda b, t: (b, 0, t)),
            pl.BlockSpec(conv_w.shape, const),
            pl.BlockSpec(conv_b.shape, const),
            pl.BlockSpec(dt_bias.shape, const),
            pl.BlockSpec(a_log.shape, const),
            pl.BlockSpec(d_x.shape, const),
            pl.BlockSpec(norm_gain.shape, const),
            pl.BlockSpec(expand.shape, const),
        ],
        out_specs=pl.BlockSpec((nb, q, SSD_INNER), lambda b, t: (b, t, 0)),
        scratch_shapes=[pltpu.VMEM((nb, SSD_GROUPS, SSD_STATE, SSD_INNER // SSD_GROUPS), F32)],
        compiler_params=_params("parallel", "arbitrary"),
        name="ssd",
    )(per_seq(z), per_seq(xbc), per_seq(xbc), dt, conv_w, conv_b, dt_bias, a_log, d_x,
      norm_gain, expand)
    return out.reshape(m, SSD_INNER)


def _cpow(base_re, base_im, exponent, nbits):
    shape = jnp.broadcast_shapes(base_re.shape, exponent.shape)
    pr = jnp.ones(shape, F32)
    pi = jnp.zeros(shape, F32)
    sr, si = base_re, base_im
    for k in range(nbits):
        bit = ((exponent >> k) & 1) == 1
        nr = pr * sr - pi * si
        ni = pr * si + pi * sr
        pr = jnp.where(bit, nr, pr)
        pi = jnp.where(bit, ni, pi)
        if k + 1 < nbits:
            sr, si = sr * sr - si * si, 2.0 * sr * si
    return pr, pi


def _s5_discretize(ar, ai, log_dt):
    dt = jnp.exp(log_dt)
    mag = jnp.exp(dt * ar)
    lam_re = mag * jnp.cos(dt * ai)
    lam_im = mag * jnp.sin(dt * ai)
    den = ar * ar + ai * ai
    f_re = ((lam_re - 1.0) * ar + lam_im * ai) / den
    f_im = (lam_im * ar - (lam_re - 1.0) * ai) / den
    return lam_re, lam_im, f_re, f_im


def _s5_kernel(u_ref, arow_ref, ldt_ref, bd_ref, cd_re_ref, cd_im_ref, d_ref, y_ref, *scratch):
    n = u_ref.shape[1]
    scr = [scratch[3 * gg:3 * gg + 3] for gg in range(n)]
    lams = [_s5_prepare(arow_ref.at[gg], ldt_ref.at[gg], bd_ref.at[gg], cd_re_ref.at[gg],
                        cd_im_ref.at[gg], *scr[gg], u_ref.shape[-1]) for gg in range(n)]
    for gg in range(n):
        _s5_apply(lams[gg], u_ref.at[:, gg], d_ref.at[gg], y_ref.at[:, gg], *scr[gg][1:])


def _s5_prepare(arow_ref, ldt_ref, bd_ref, cd_re_ref, cd_im_ref, kmat_scr, t_scr, f_scr, q):
    gsz = S5_GROUP
    p2 = 2 * S5_STATE
    nbits = (q - 1).bit_length()
    log_dt = ldt_ref[...]

    lane = lax.broadcasted_iota(jnp.int32, (1, p2), 1)
    first_half_l = lane < S5_STATE
    rowp = lax.broadcasted_iota(jnp.int32, (p2, 1), 0)
    first_half_r = rowp < S5_STATE

    ar_r = arow_ref[0:1, :]
    ai_r = arow_ref[1:2, :]
    lr, li, fr, fi = _s5_discretize(ar_r, ai_r, log_dt)
    sign_b = jnp.where(first_half_l, -1.0, 1.0)
    bd1 = bd_ref[...]
    bd2 = pltpu.roll(bd1, S5_STATE, 1) * sign_b
    bb_a = fr * bd1 + fi * bd2
    bb_b = fr * bd2 - fi * bd1

    srow = lax.broadcasted_iota(jnp.int32, (q, 1), 0)
    pr_re, pr_im = _cpow(lr, li, (q - 1) - srow, nbits)
    for i in range(gsz):
        es = pr_re * bb_a[i:i + 1, :] + pr_im * bb_b[i:i + 1, :]
        t_scr[i * q:(i + 1) * q, gsz * q:gsz * q + p2] = es.astype(BF16)

    pt_re, pt_im = _cpow(lr, li, srow, nbits)
    p1_re = pt_re * lr - pt_im * li
    p1_im = pt_re * li + pt_im * lr
    cre = cd_re_ref[...]
    cim = cd_im_ref[...]
    for o in range(gsz):
        c_r = cre[o:o + 1, :]
        c_i = cim[o:o + 1, :]
        f_o = jnp.where(first_half_l, c_r * p1_re - c_i * p1_im, -(c_r * p1_im + c_i * p1_re))
        f_scr[o * q:(o + 1) * q, :] = f_o.astype(BF16)

    eye = (lax.broadcasted_iota(jnp.int32, (p2, p2), 0)
           == lax.broadcasted_iota(jnp.int32, (p2, p2), 1))
    lcr = jnp.sum(jnp.where(eye, lr, 0.0), axis=1, keepdims=True)
    lci = jnp.sum(jnp.where(eye, li, 0.0), axis=1, keepdims=True)
    dlane = lax.broadcasted_iota(jnp.int32, (1, q), 1)
    pw_re, pw_im = _cpow(lcr, lci, dlane, nbits)
    pow_stack = jnp.where(first_half_r, pw_re, pw_im)

    v1 = bb_a * jnp.where(first_half_l, 1.0, -1.0)
    v2 = -pltpu.roll(bb_a, S5_STATE, 1)
    coef =cre[:, None, :] * v1[None, :, :] + cim[:, None, :] * v2[None, :, :]
    kmat_scr[...] = _dot_exact(coef.reshape(gsz * gsz, p2), pow_stack)

    trow = lax.broadcasted_iota(jnp.int32, (q, q), 0)
    tcol = lax.broadcasted_iota(jnp.int32, (q, q), 1)
    lower = tcol >= trow
    for i in range(gsz):
        for o in range(gsz):
            k_row = kmat_scr[o * gsz + i:o * gsz + i + 1, :]
            blk = pltpu.roll(jnp.broadcast_to(k_row, (q, q)), 0, 1, stride=1, stride_axis=0)
            t_scr[i * q:(i + 1) * q, o * q:(o + 1) * q] = jnp.where(lower, blk, 0.0).astype(BF16)
    return lr, li


def _s5_apply(lam, u_ref, d_ref, y_ref, t_scr, f_scr):
    lr, li = lam
    bsz, rows, q = u_ref.shape
    gsz = S5_GROUP
    nc = rows // gsz
    m = bsz * nc
    sign_b = jnp.where(lax.broadcasted_iota(jnp.int32, (1, 2 * S5_STATE), 1) < S5_STATE, -1.0, 1.0)
    u_f32 = [jnp.concatenate([u_ref[b, pl.ds(i, nc, stride=gsz), :] for b in range(bsz)], axis=0)
             for i in range(gsz)]
    u_cat = jnp.concatenate([v.astype(BF16) for v in u_f32], axis=1)
    acc = _dot(u_cat, t_scr[...])
    x = acc[:, gsz * q:]

    sq_re, sq_im = lr, li
    for _ in range(q.bit_length() - 1):
        sq_re, sq_im = sq_re * sq_re - sq_im * sq_im, 2.0 * sq_re * sq_im
    crow = lax.broadcasted_iota(jnp.int32, (m, 1), 0) % nc
    k = 1
    while k < nc:
        sh = jnp.where(crow >= k, pltpu.roll(x, k, 0), 0.0)
        x = x + sq_re * sh + (sq_im * sign_b) * pltpu.roll(sh, S5_STATE, 1)
        sq_re, sq_im = sq_re * sq_re - sq_im * sq_im, 2.0 * sq_re * sq_im
        k *= 2
    h_prev = jnp.where(crow >= 1, pltpu.roll(x, 1, 0), 0.0)
    y = acc[:, :gsz * q] + lax.dot_general(h_prev.astype(BF16), f_scr[...], (((1,), (1,)), ((), ())),
                                           preferred_element_type=F32)
    for o in range(gsz):
        y_o = y[:, o * q:(o + 1) * q] + d_ref[o:o + 1, :] * u_f32[o]
        for b in range(bsz):
            y_ref[b, pl.ds(o, nc, stride=gsz), :] = y_o[b * nc:(b + 1) * nc]


def _s5(u5, arow, ldt, bd, cd_re, cd_im, d_rows):
    bsz, _, nc, _, q = u5.shape
    rows = nc * S5_GROUP
    p2 = 2 * S5_STATE
    per_g = lambda g: (g, 0, 0)
    n = S5_PER_STEP
    assert S5_GROUPS % n == 0
    y4 = pl.pallas_call(
        _s5_kernel,
        out_shape=jax.ShapeDtypeStruct((bsz, S5_GROUPS, rows, q), F32),
        grid=(S5_GROUPS // n,),
        in_specs=[
            pl.BlockSpec((bsz, n, rows, q), lambda g: (0, g, 0, 0)),
            pl.BlockSpec((n, 2, p2), per_g),
            pl.BlockSpec((n, 1, 1), per_g),
            pl.BlockSpec((n, S5_GROUP, p2), per_g),
            pl.BlockSpec((n, S5_GROUP, p2), per_g),
            pl.BlockSpec((n, S5_GROUP, p2), per_g),
            pl.BlockSpec((n, S5_GROUP, q), per_g),
        ],
        out_specs=pl.BlockSpec((bsz, n, rows, q), lambda g: (0, g, 0, 0)),
        scratch_shapes=n * [
            pltpu.VMEM((S5_GROUP * S5_GROUP, q), F32),
            pltpu.VMEM((S5_GROUP * q, S5_GROUP * q + p2), BF16),
            pltpu.VMEM((S5_GROUP * q, p2), BF16),
        ],
        compiler_params=_params("parallel"),
        name="s5",
    )(u5.reshape(bsz, S5_GROUPS, rows, q), arow, ldt, bd, cd_re, cd_im, d_rows)
    return y4.reshape(u5.shape)


def _gelu_tanh(x):
    return 0.5 * x * (1.0 + jnp.tanh(0.7978845608028654 * (x + 0.044715 * (x * x * x))))


def _outproj_even_ffn_kernel(x_ref, ya_ref, ybt_ref, wglut_ref, bglu_ref, wa_ref, wb_ref,
                             fg_ref, wg_ref, wu_ref, wd_ref, o_ref, wg_s, wu_s, wd_s, act_scr):
    i = pl.program_id(0)
    _ffn_stage(i, wg_ref, wu_ref, wd_ref, wg_s, wu_s, wd_s)

    @pl.when(i >= wg_s.shape[0])
    def _():
        q = ybt_ref.shape[-1]
        ybt = jnp.concatenate([ybt_ref[0, :, cc].reshape(S5_WIDTH, q) for cc in range(ybt_ref.shape[2])],
                              axis=1)
        gate = _dot(wglut_ref[...], _gelu_tanh(ybt).astype(BF16)) + bglu_ref[...]
        yb = (ybt * jax.nn.sigmoid(gate)).T.astype(BF16)
        x2 = x_ref[...] + _dot(ya_ref[...], wa_ref[...]) + _dot(yb, wb_ref[...])
        o_ref[...] = _ffn_apply(x2, fg_ref, wg_s, wu_s, wd_s, act_scr)


def _outproj_even_ffn(x2d, ya, yb5, wglut, bglu, wa, wb, ffn, layer, bsz, seq):
    m, d = x2d.shape
    tm = min(FFN_TM, seq)
    per_b = seq // tm
    q = yb5.shape[-1]
    ns = _ffn_chunks(ffn)
    tile = lambda i: jnp.maximum(i - ns, 0)
    return pl.pallas_call(
        _outproj_even_ffn_kernel,
        out_shape=jax.ShapeDtypeStruct((m, d), F32),
        grid=(ns + m // tm,),
        in_specs=[
            pl.BlockSpec((tm, d), lambda i: (tile(i), 0)),
            pl.BlockSpec((tm, SSD_INNER), lambda i: (tile(i), 0)),
            pl.BlockSpec((1, S5_GROUPS, tm // q, S5_GROUP, q),
                         lambda i: (tile(i) // per_b, 0, tile(i) % per_b, 0, 0)),
            _resident(wglut.shape), _resident(bglu.shape), _resident(wa.shape), _resident(wb.shape),
        ] + _ffn_specs(*ffn, layer),
        out_specs=pl.BlockSpec((tm, d), lambda i: (tile(i), 0)),
        scratch_shapes=_ffn_scratch(ffn, tm),
        compiler_params=_params("arbitrary"),
        name="outproj_even_ffn",
    )(x2d, ya, yb5, wglut, bglu, wa, wb, *ffn)


def _head_rmsnorm(x, gain):
    assert LANES == 2 * ATT_HEAD_DIM
    first = lax.broadcasted_iota(jnp.int32, (1, LANES), 1) < ATT_HEAD_DIM
    tiles = []
    for p in range(x.shape[1] // LANES):
        xp = x[:, p * LANES:(p + 1) * LANES]
        sq = xp * xp
        s0 = jnp.sum(jnp.where(first, sq, 0.0), axis=-1, keepdims=True)
        s1 = jnp.sum(jnp.where(first, 0.0, sq), axis=-1, keepdims=True)
        ss = jnp.where(first, s0, s1)
        tiles.append(xp * lax.rsqrt(ss * (1.0 / ATT_HEAD_DIM) + EPS))
    return jnp.concatenate(tiles, axis=1) * gain


def _head_rmsnorm_t(xt, gain_col):
    c, n = xt.shape
    x3 = xt.reshape(c // ATT_HEAD_DIM, ATT_HEAD_DIM, n)
    ms = jnp.mean(x3 * x3, axis=1, keepdims=True)
    return (x3 * lax.rsqrt(ms + EPS)).reshape(c, n) * gain_col


def _pool_apply(u, prev, pos, w_ref, sc_ref):
    ext = jnp.concatenate([prev, u], axis=0)
    sums = {}
    s = ext
    w = 1
    while w < POOL_MAX:
        s = s + pltpu.roll(s, w, 0)
        w *= 2
        sums[w] = s[POOL_MAX:]
    outs = []
    for g, win in enumerate(POOL_WINDOWS):
        sl = slice(g * POOL_GROUP, (g + 1) * POOL_GROUP)
        count = jnp.minimum(pos + 1, win).astype(F32)
        pooled = sums[win][:, sl] / count - u[:, sl]
        outs.append(_dot(pooled.astype(BF16), w_ref[g]))
    return jnp.concatenate(outs, axis=1) * sc_ref[...]


def _ffn_inproj_odd_kernel(x_ref, fg_ref, wg_ref, wu_ref, wd_ref, g_ref, wqt_ref, wk_ref, wvt_ref,
                           wup_ref, qg_ref, kg_ref, pw_ref, ps_ref,
                           x1_ref, qt_ref, k_ref, vt_ref, yd_ref, wg_s, wu_s, wd_s, act_scr, carry_scr,
                           *, tiles_per_seq):
    i = pl.program_id(0)
    tm = x_ref.shape[0]
    n_stage = wg_s.shape[0]
    _ffn_stage(i, wg_ref, wu_ref, wd_ref, wg_s, wu_s, wd_s)

    @pl.when(i == 0)
    def _():
        carry_scr[...] = jnp.zeros_like(carry_scr)

    @pl.when(i >= n_stage)
    def _():
        x1 = _ffn_apply(x_ref[...], fg_ref, wg_s, wu_s, wd_s, act_scr)
        x1_ref[...] = x1
        h = _rmsnorm_bf16(x1, g_ref[...])
        nt = (((1,), (1,)), ((), ()))
        qt = lax.dot_general(wqt_ref[...], h, nt, preferred_element_type=F32)
        qt = _head_rmsnorm_t(qt, qg_ref[...]) * (ATT_HEAD_DIM ** -0.5 * LOG2E)
        qt_ref[0] = qt.astype(BF16)
        k_ref[...] = _head_rmsnorm(_dot(h, wk_ref[...]), kg_ref[...]).astype(BF16)
        vt_ref[0] = lax.dot_general(wvt_ref[...], h, nt, preferred_element_type=F32).astype(BF16)
        u = _dot(h, wup_ref[...])
        tile_in_seq = (i - n_stage) % tiles_per_seq
        prev = jnp.where(tile_in_seq == 0, 0.0, carry_scr[...])
        pos = tile_in_seq * tm + lax.broadcasted_iota(jnp.int32, (tm, 1), 0)
        yd_ref[...] = _pool_apply(u, prev, pos, pw_ref, ps_ref).astype(yd_ref.dtype)
        carry_scr[...] = u[tm - POOL_MAX:]


def _ffn_inproj_odd(x2d, ffn, gain, wqt, wk, wvt, wup, qg_col, kg, w_pool, pool_scale, layer, bsz, seq):
    m, d = x2d.shape
    tm = min(FFN_TM, seq)
    per_b = seq // tm
    w = ATT_WIDTH
    ns = _ffn_chunks(ffn)
    tl = lambda i: jnp.maximum(i - ns, 0)
    tile = pl.BlockSpec((tm, w), lambda i: (tl(i), 0))
    tile_t = pl.BlockSpec((1, w, tm), lambda i: (tl(i) // per_b, 0, tl(i) % per_b))
    row_tile = pl.BlockSpec((tm, d), lambda i: (tl(i), 0))
    tok = jax.ShapeDtypeStruct((m, w), BF16)
    chan = jax.ShapeDtypeStruct((bsz, w, seq), BF16)
    return pl.pallas_call(
        functools.partial(_ffn_inproj_odd_kernel, tiles_per_seq=per_b),
        out_shape=(jax.ShapeDtypeStruct((m, d), F32), chan, tok, chan, tok),
        grid=(ns + m // tm,),
        in_specs=[row_tile] + _ffn_specs(*ffn, layer) + [
            pl.BlockSpec((None, 1, d), lambda i: (layer, 0, 0)),
            _resident(wqt.shape), _resident(wk.shape), _resident(wvt.shape), _resident(wup.shape),
            _resident(qg_col.shape), _resident(kg.shape),
            _resident(w_pool.shape), _resident(pool_scale.shape),
        ],
        out_specs=(row_tile, tile_t, tile, tile_t, tile),
        scratch_shapes=_ffn_scratch(ffn, tm) + [pltpu.VMEM((POOL_MAX, w), F32)],
        compiler_params=_params("arbitrary"),
        name="ffn_inproj_odd",
    )(x2d, *ffn, gain, wqt, wk, wvt, wup, qg_col, kg, w_pool, pool_scale)


def _attn_kernel(qt_ref, k0_ref, k1_ref, k2_ref, vt0_ref, vt1_ref, vt2_ref, brow_ref, o_ref, bias_ref):
    t = pl.program_id(1)
    tq = o_ref.shape[0]
    nk = 3 * tq

    @pl.when((pl.program_id(0) == 0) & (t == 0))
    def _():
        keyi = lax.broadcasted_iota(jnp.int32, (nk, tq), 0)
        qryi = lax.broadcasted_iota(jnp.int32, (nk, tq), 1)
        rel_chunk = keyi // ATT_CHUNK - (2 * tq // ATT_CHUNK - LEFT_CHUNKS) - qryi // ATT_CHUNK
        rel_chunk = jnp.where(rel_chunk >= 0, rel_chunk, LEFT_CHUNKS + 1)
        for hh in range(ATT_HEADS):
            base = jnp.broadcast_to(brow_ref[hh:hh + 1, :], (nk, 4 * tq))
            shifted = pltpu.roll(base, 0, 1, stride=1, stride_axis=0)
            bias = shifted[:, nk:] * LOG2E
            for var in range(3):
                first_key = (2 - var) * tq
                keep = jnp.where(keyi >= first_key, rel_chunk, LEFT_CHUNKS + 1) <= LEFT_CHUNKS
                bias_ref[var, hh] = jnp.where(keep, bias, -jnp.inf)

    var = jnp.minimum(t, 2)
    qt = qt_ref[0]
    kcat = jnp.concatenate([k0_ref[...], k1_ref[...], k2_ref[...]], axis=0)
    vtcat = jnp.concatenate([vt0_ref[0], vt1_ref[0], vt2_ref[0]], axis=1)
    hd = ATT_HEAD_DIM
    heads_per_slab = LANES // hd
    no_q = jnp.zeros((LANES - hd, tq), qt.dtype)
    ones_rows = jnp.ones((16, nk), qt.dtype)
    n_tiles = nk // tq

    def scores(head, j):
        p, hh = divmod(head, heads_per_slab)
        rows = slice(head * hd, (head + 1) * hd)
        qth = jnp.concatenate([qt[rows]] + [no_q] if hh == 0 else [no_q] + [qt[rows]], axis=0)
        ks = slice(j * tq, (j + 1) * tq)
        return _dot(kcat[ks, p * LANES:(p + 1) * LANES], qth) + bias_ref[var, head, ks, :]

    items = [(head, j) for head in range(ATT_HEADS) for j in range(n_tiles)]
    ahead = [scores(*it) for it in items[:ATT_LOOKAHEAD]]
    outs, parts, tops = [], [], []
    for idx, (head, j) in enumerate(items):
        st = ahead.pop(0)
        if idx + ATT_LOOKAHEAD < len(items):
            ahead.append(scores(*items[idx + ATT_LOOKAHEAD]))
        rows = slice(head * hd, (head + 1) * hd)
        ks = slice(j * tq, (j + 1) * tq)
        top = jnp.maximum(jnp.max(st, axis=0, keepdims=True), MASKED_FLOOR)
        e = jnp.exp2(st - top).astype(BF16)
        vth = jnp.concatenate([vtcat[rows, ks], ones_rows[:, ks]], axis=0)
        parts.append(_dot(vth, e))
        tops.append(top)
        if j == n_tiles - 1:
            top_all = functools.reduce(jnp.maximum, tops)
            ot = sum(part * jnp.exp2(tp - top_all) for part, tp in zip(parts, tops))
            outs.append(ot[:hd] / ot[hd:hd + 1])
            parts, tops = [], []
            if head % heads_per_slab == heads_per_slab - 1:
                p = head // heads_per_slab
                o_ref[:, p * LANES:(p + 1) * LANES] = jnp.concatenate(outs, axis=0).T.astype(o_ref.dtype)
                outs = []


def _attention(qt, k, vt, brow, bsz, seq):
    m, w = k.shape
    tq = ATT_T
    assert 2 * tq >= LEFT_CHUNKS * ATT_CHUNK and tq > MAX_REL and seq % tq == 0
    nt = seq // tq
    back = lambda n: (lambda b, t: (b * nt + jnp.maximum(t - n, 0), 0))
    back_t = lambda n: (lambda b, t: (b, 0, jnp.maximum(t - n, 0)))
    tok = lambda f: pl.BlockSpec((tq, w), f)
    chan = lambda f: pl.BlockSpec((1, w, tq), f)
    return pl.pallas_call(
        _attn_kernel,
        out_shape=jax.ShapeDtypeStruct((m, w), BF16),
        grid=(bsz, nt),
        in_specs=[chan(back_t(0)), tok(back(2)), tok(back(1)), tok(back(0)),
                  chan(back_t(2)), chan(back_t(1)), chan(back_t(0)),
                  pl.BlockSpec(brow.shape, lambda b, t: (0, 0))],
        out_specs=tok(back(0)),
        scratch_shapes=[pltpu.VMEM((3, ATT_HEADS, 3 * tq, tq), F32)],
        compiler_params=_params("arbitrary", "arbitrary"),
        name="band_attention",
    )(qt, k, k, k, vt, vt, vt, brow)


def _bias_rows(rel_bias, tq):
    rb = rel_bias.astype(F32)
    nh = rb.shape[0]
    near = jnp.broadcast_to(rb[:, :1], (nh, tq - MAX_REL))
    far = jnp.broadcast_to(rb[:, -1:], (nh, 3 * tq - MAX_REL - 1))
    return jnp.concatenate([near, rb, far], axis=1)


def _outproj_odd_ffn_kernel(x_ref, yc_ref, yd_ref, wa_ref, wb_ref, fg_ref, wg_ref, wu_ref, wd_ref,
                            o_ref, wg_s, wu_s, wd_s, act_scr):
    i = pl.program_id(0)
    _ffn_stage(i, wg_ref, wu_ref, wd_ref, wg_s, wu_s, wd_s)

    @pl.when(i >= wg_s.shape[0])
    def _():
        x2 = x_ref[...] + _dot(yc_ref[...], wa_ref[...]) + _dot(yd_ref[...], wb_ref[...])
        o_ref[...] = _ffn_apply(x2, fg_ref, wg_s, wu_s, wd_s, act_scr)


def _outproj_odd_ffn(x2d, yc, yd, wa, wb, ffn, layer):
    m, d = x2d.shape
    tm = min(OUT_TM, m)
    ns = _ffn_chunks(ffn)
    tile = lambda i: jnp.maximum(i - ns, 0)
    return pl.pallas_call(
        _outproj_odd_ffn_kernel,
        out_shape=jax.ShapeDtypeStruct((m, d), F32),
        grid=(ns + m // tm,),
        in_specs=[
            pl.BlockSpec((tm, d), lambda i: (tile(i), 0)),
            pl.BlockSpec((tm, ATT_WIDTH), lambda i: (tile(i), 0)),
            pl.BlockSpec((tm, ATT_WIDTH), lambda i: (tile(i), 0)),
            _resident(wa.shape), _resident(wb.shape),
        ] + _ffn_specs(*ffn, layer),
        out_specs=pl.BlockSpec((tm, d), lambda i: (tile(i), 0)),
        scratch_shapes=_ffn_scratch(ffn, tm),
        compiler_params=_params("arbitrary"),
        name="outproj_odd_ffn",
    )(x2d, yc, yd, wa, wb, *ffn)


def _even_layer(x2d, ffn1, ffn2, gain, layer, bsz, seq, even_w_in, even_w_out, ssd_conv_w, ssd_conv_b,
                ssd_dt_bias, ssd_a_log, ssd_d, ssd_norm, s5_a_re, s5_a_im, s5_log_dt, s5_b_re, s5_b_im,
                s5_c_re, s5_c_im, s5_d, s5_w_glu, s5_b_glu):
    i = layer // 2
    w_in = even_w_in[i]
    o1 = SSD_INNER
    o2 = o1 + SSD_XBC
    o3 = o2 + SSD_HEADS
    wz = w_in[:, :o1].astype(BF16)
    wx = w_in[:, o1:o2].astype(BF16)
    wut = jnp.concatenate([w_in[:, o3:].T, w_in[:, o2:o3].T], axis=0)
    wut = jnp.pad(wut, ((0, -wut.shape[0] % 16), (0, 0))).astype(BF16)
    x1, z, xbc, dtt, ut = _ffn_inproj_even(x2d, ffn1, gain, wz, wx, wut, layer, bsz, seq)

    col = lambda v: v.astype(F32)[:, None]
    expand = (jnp.arange(LANES)[:, None] == (jnp.arange(SSD_INNER)[None, :] // SSD_HEAD_DIM)).astype(BF16)
    ya = _ssd(z, xbc, dtt, ssd_conv_w[i], ssd_conv_b[i][None, :], col(ssd_dt_bias[i]),
              col(ssd_a_log[i]), jnp.repeat(ssd_d[i], SSD_HEAD_DIM)[None, :], ssd_norm[i][None, :],
              expand, bsz, seq)

    q = min(S5_Q, seq)
    are, aim = s5_a_re[i], s5_a_im[i]
    dup = lambda v: jnp.concatenate([v, v], axis=-1)
    arow = jnp.stack([dup(are), dup(aim)], axis=1)
    ldt = s5_log_dt[i][:, None, None]
    bd = jnp.concatenate([jnp.swapaxes(s5_b_re[i], 1, 2), jnp.swapaxes(s5_b_im[i], 1, 2)], axis=-1)
    cd_re, cd_im = dup(s5_c_re[i]), dup(s5_c_im[i])
    d_rows = jnp.broadcast_to(s5_d[i][:, :, None], (S5_GROUPS, S5_GROUP, q))
    yb5 = _s5(ut, arow, ldt, bd, cd_re, cd_im, d_rows)

    w_out = even_w_out[i]
    return _outproj_even_ffn(x1, ya, yb5, s5_w_glu[i].T.astype(BF16), s5_b_glu[i][:, None],
                             w_out[:SSD_INNER].astype(BF16), w_out[SSD_INNER:].astype(BF16),
                             ffn2, layer, bsz, seq)


def _odd_layer(x2d, ffn1, ffn2, gain, layer, bsz, seq, odd_w_in, odd_w_out, attn_q_norm, attn_k_norm,
               attn_rel_bias, pool_w, pool_scale):
    i = layer // 2
    w_in = odd_w_in[i].astype(BF16)
    w = ATT_WIDTH
    tile_gain = lambda v: jnp.tile(v.astype(F32), ATT_HEADS)[None, :]
    x1, qt, k, vt, yd = _ffn_inproj_odd(
        x2d, ffn1, gain, w_in[:, :w].T, w_in[:, w:2 * w], w_in[:, 2 * w:3 * w].T, w_in[:, 3 * w:],
        tile_gain(attn_q_norm[i]).T, tile_gain(attn_k_norm[i]),
        pool_w[i].astype(BF16), pool_scale[i][None, :], layer, bsz, seq)
    yc = _attention(qt, k, vt, _bias_rows(attn_rel_bias[i], ATT_T), bsz, seq)
    w_out = odd_w_out[i]
    return _outproj_odd_ffn(x1, yc, yd, w_out[:w].astype(BF16), w_out[w:].astype(BF16), ffn2, layer)


def kernel(x, ffn1_norm, ffn1_w_gate, ffn1_w_up, ffn1_w_down, mix_norm, even_w_in, even_w_out, ssd_conv_w, ssd_conv_b, ssd_dt_bias, ssd_a_log, ssd_d, ssd_norm, s5_a_re, s5_a_im, s5_log_dt, s5_b_re, s5_b_im, s5_c_re, s5_c_im, s5_d, s5_w_glu, s5_b_glu, odd_w_in, odd_w_out, attn_q_norm, attn_k_norm, attn_rel_bias, pool_w, pool_scale, ffn2_norm, ffn2_w_gate, ffn2_w_up, ffn2_w_down):
    bsz, seq, d = x.shape
    depth = ffn1_norm.shape[0]
    x2d = x.reshape(bsz * seq, d)
    g3 = lambda v: v.astype(F32)[:, None, :]
    f1n, mxn, f2n = g3(ffn1_norm), g3(mix_norm), g3(ffn2_norm)
    as32 = lambda v: v.astype(F32)
    ffn1 = (f1n, as32(ffn1_w_gate), as32(ffn1_w_up), as32(ffn1_w_down))
    ffn2 = (f2n, as32(ffn2_w_gate), as32(ffn2_w_up), as32(ffn2_w_down))
    for layer in range(depth):
        if layer % 2 == 0:
            x2d = _even_layer(x2d, ffn1, ffn2, mxn, layer, bsz, seq, even_w_in, even_w_out, ssd_conv_w,
                              ssd_conv_b, ssd_dt_bias, ssd_a_log, ssd_d, ssd_norm, s5_a_re, s5_a_im,
                              s5_log_dt, s5_b_re, s5_b_im, s5_c_re, s5_c_im, s5_d, s5_w_glu, s5_b_glu)
        else:
            x2d = _odd_layer(x2d, ffn1, ffn2, mxn, layer, bsz, seq, odd_w_in, odd_w_out, attn_q_norm,
                             attn_k_norm, attn_rel_bias, pool_w, pool_scale)
    return x2d.reshape(bsz, seq, d)
```

```python
import functools
import math

import jax
import jax.numpy as jnp
from jax import lax
from jax.experimental import pallas as pl
from jax.experimental.pallas import tpu as pltpu

F32 = jnp.float32
BF16 = jnp.bfloat16
HIGHEST = lax.Precision.HIGHEST

EPS = 1e-6
LOG2E = 1.4426950408889634
MASKED_FLOOR = -1e30
LANES = 128
VMEM_LIMIT = 56 * 1024 * 1024

SSD_HEADS = 8
SSD_HEAD_DIM = 64
SSD_INNER = 512
SSD_GROUPS = 2
SSD_STATE = 128
SSD_CONV = 4
SSD_XBC = 1024
S5_WIDTH = 512
S5_GROUP = 16
S5_GROUPS = 32
S5_STATE = 64
ATT_HEADS = 8
ATT_HEAD_DIM = 64
ATT_WIDTH = 512
ATT_CHUNK = 64
LEFT_CHUNKS = 8
MAX_REL = 128
POOL_WINDOWS = (2, 4, 8, 16)
POOL_GROUP = 128
POOL_MAX = 16

FFN_TM = 512
OUT_TM = 1024
FFN_TF = 256
SSD_Q = 128
SSD_PER_STEP = 4
S5_Q = 128
S5_PER_STEP = 2
ATT_T = 256
ATT_LOOKAHEAD = 5


def _params(*sem):
    return pltpu.CompilerParams(dimension_semantics=sem, vmem_limit_bytes=VMEM_LIMIT)


def _dot(a, b):
    return jnp.dot(a, b, preferred_element_type=F32)


def _dot_exact(a, b):
    return jnp.dot(a, b, preferred_element_type=F32, precision=HIGHEST)


def _split3(x):
    hi = x.astype(BF16)
    rest = x - hi.astype(F32)
    mid = rest.astype(BF16)
    lo = (rest - mid.astype(F32)).astype(BF16)
    return hi, mid, lo


def _select_right(x, sel):
    hi, mid, lo = _split3(x)
    return _dot(hi, sel) + _dot(mid, sel) + _dot(lo, sel)


def _rmsnorm_bf16(x, gain):
    ms = jnp.mean(x * x, axis=-1, keepdims=True)
    return (x * lax.rsqrt(ms + EPS) * gain).astype(BF16)


def _silu(x):
    return x * jax.nn.sigmoid(x)


def _ffn_stage(i, wg_ref, wu_ref, wd_ref, wg_s, wu_s, wd_s):
    @pl.when(i < wg_s.shape[0])
    def _():
        wg_s[i] = wg_ref[...].astype(BF16)
        wu_s[i] = wu_ref[...].astype(BF16)
        wd_s[pl.ds(pl.multiple_of(i * FFN_TF, FFN_TF), FFN_TF), :] = wd_ref[...].astype(BF16)


def _ffn_apply(x, g_ref, wg_s, wu_s, wd_s, act_scr):
    h = _rmsnorm_bf16(x, g_ref[...])
    for c in range(wg_s.shape[0]):
        gate = _dot(h, wg_s[c])
        up = _dot(h, wu_s[c])
        act_scr[:, c * FFN_TF:(c + 1) * FFN_TF] = (_silu(gate) * up).astype(BF16)
    return x + 0.5 * _dot(act_scr[...], wd_s[...])


def _resident(shape):
    return pl.BlockSpec(shape, lambda i: (0,) * len(shape), pipeline_mode=pl.Buffered(1))


def _ffn_chunks(ffn):
    dff = ffn[1].shape[-1]
    assert dff % FFN_TF == 0
    return dff // FFN_TF


def _ffn_specs(gain, wg, wu, wd, layer):
    d, dff = wg.shape[1:]
    last = dff // FFN_TF - 1
    return [
        pl.BlockSpec((None, 1, d), lambda i: (layer, 0, 0)),
        pl.BlockSpec((None, d, FFN_TF), lambda i: (layer, 0, jnp.minimum(i, last))),
        pl.BlockSpec((None, d, FFN_TF), lambda i: (layer, 0, jnp.minimum(i, last))),
        pl.BlockSpec((None, FFN_TF, d), lambda i: (layer, jnp.minimum(i, last), 0)),
    ]


def _ffn_scratch(ffn, tm):
    d, dff = ffn[1].shape[1:]
    n = dff // FFN_TF
    return [pltpu.VMEM((n, d, FFN_TF), BF16), pltpu.VMEM((n, d, FFN_TF), BF16),
            pltpu.VMEM((dff, d), BF16), pltpu.VMEM((tm, dff), BF16)]


def _ffn_inproj_even_kernel(x_ref, fg_ref, wg_ref, wu_ref, wd_ref, g_ref, wz_ref, wx_ref,
                            wut_ref, x1_ref, z_ref, xbc_ref, dtt_ref, ut_ref, wg_s, wu_s, wd_s, act_scr):
    i = pl.program_id(0)
    _ffn_stage(i, wg_ref, wu_ref, wd_ref, wg_s, wu_s, wd_s)

    @pl.when(i >= wg_s.shape[0])
    def _():
        x1 = _ffn_apply(x_ref[...], fg_ref, wg_s, wu_s, wd_s, act_scr)
        x1_ref[...] = x1
        h = _rmsnorm_bf16(x1, g_ref[...])
        z_ref[...] = _dot(h, wz_ref[...])
        xbc_ref[...] = _dot(h, wx_ref[...])
        ut = lax.dot_general(wut_ref[...], h, (((1,), (1,)), ((), ())), preferred_element_type=F32)
        dtt_ref[0] = ut[S5_WIDTH:S5_WIDTH + SSD_HEADS, :]
        q = ut_ref.shape[-1]
        for cc in range(ut_ref.shape[2]):
            ut_ref[0, :, cc] = ut[:S5_WIDTH, cc * q:(cc + 1) * q].reshape(S5_GROUPS, S5_GROUP, q)


def _ffn_inproj_even(x2d, ffn, gain, wz, wx, wut, layer, bsz, seq):
    m, d = x2d.shape
    tm = min(FFN_TM, seq)
    per_b = seq // tm
    q = min(S5_Q, seq)
    ns = _ffn_chunks(ffn)
    tile = lambda i: jnp.maximum(i - ns, 0)
    return pl.pallas_call(
        _ffn_inproj_even_kernel,
        out_shape=(
            jax.ShapeDtypeStruct((m, d), F32),
            jax.ShapeDtypeStruct((m, SSD_INNER), F32),
            jax.ShapeDtypeStruct((m, SSD_XBC), F32),
            jax.ShapeDtypeStruct((bsz, SSD_HEADS, seq), F32),
            jax.ShapeDtypeStruct((bsz, S5_GROUPS, seq // q, S5_GROUP, q), F32),
        ),
        grid=(ns + m // tm,),
        in_specs=[pl.BlockSpec((tm, d), lambda i: (tile(i), 0))] + _ffn_specs(*ffn, layer) + [
            pl.BlockSpec((None, 1, d), lambda i: (layer, 0, 0)),
            _resident(wz.shape), _resident(wx.shape), _resident(wut.shape),
        ],
        out_specs=(
            pl.BlockSpec((tm, d), lambda i: (tile(i), 0)),
            pl.BlockSpec((tm, SSD_INNER), lambda i: (tile(i), 0)),
            pl.BlockSpec((tm, SSD_XBC), lambda i: (tile(i), 0)),
            pl.BlockSpec((1, SSD_HEADS, tm), lambda i: (tile(i) // per_b, 0, tile(i) % per_b)),
            pl.BlockSpec((1, S5_GROUPS, tm // q, S5_GROUP, q),
                         lambda i: (tile(i) // per_b, 0, tile(i) % per_b, 0, 0)),
        ),
        scratch_shapes=_ffn_scratch(ffn, tm),
        compiler_params=_params("arbitrary"),
        name="ffn_inproj_even",
    )(x2d, *ffn, gain, wz, wx, wut)


def _shift_rows(cur, prev8, j):
    rolled = pltpu.roll(cur, j, 0)
    prev_rolled = pltpu.roll(prev8, j, 0)
    row = lax.broadcasted_iota(jnp.int32, prev8.shape, 0)
    top = jnp.where(row < j, prev_rolled, rolled[:8])
    return jnp.concatenate([top, rolled[8:]], axis=0)


def _ssd_kernel(z_ref, xbc_ref, xprev_ref, dt_ref, *rest):
    consts, o_ref, h_scr = rest[:-2], rest[-2], rest[-1]

    @pl.when(pl.program_id(1) == 0)
    def _():
        h_scr[...] = jnp.zeros_like(h_scr)

    for bb in range(z_ref.shape[0]):
        _ssd_chunk(z_ref.at[bb], xbc_ref.at[bb], xprev_ref.at[bb], dt_ref.at[bb], *consts,
                   o_ref.at[bb], h_scr.at[bb])


def _ssd_chunk(z_ref, xbc_ref, xprev_ref, dt_ref, cw_ref, cb_ref, dtb_ref, alog_ref,
               dx_ref, ng_ref, e_ref, o_ref, h_scr):
    t = pl.program_id(1)
    q = z_ref.shape[0]
    gw = SSD_INNER // SSD_GROUPS
    hpg = SSD_HEADS // SSD_GROUPS

    xbc = xbc_ref[...]
    prev = jnp.where(t > 0, xprev_ref[...], 0.0)
    conv = xbc * cw_ref[SSD_CONV - 1:SSD_CONV, :] + cb_ref[...]
    for j in range(1, SSD_CONV):
        conv = conv + _shift_rows(xbc, prev, j) * cw_ref[SSD_CONV - 1 - j:SSD_CONV - j, :]
    xc = _silu(conv)
    xs = xc[:, :SSD_INNER]

    dtr = dt_ref[...] + dtb_ref[...]
    dt_t = jnp.maximum(dtr, 0.0) + jnp.log1p(jnp.exp(-jnp.abs(dtr)))
    a = -jnp.exp(alog_ref[...])
    row = lax.broadcasted_iota(jnp.int32, (q, q), 0)
    col = lax.broadcasted_iota(jnp.int32, (q, q), 1)
    causal = row >= col
    triu = jnp.where(row <= col, 1.0, 0.0).astype(BF16)
    acs_t = _select_right(dt_t * a, triu)
    pad = jnp.zeros((LANES - SSD_HEADS, q), F32)
    acs = jnp.concatenate([acs_t, pad], axis=0).T
    dt = jnp.concatenate([dt_t, pad], axis=0).T
    expanded = _select_right(jnp.concatenate([acs, dt], axis=0), e_ref[...])
    acs_x = expanded[:q]
    dt_x = expanded[q:]
    xdt = xs * dt_x
    decay_in = jnp.exp(acs_x)
    acs_end = acs_x[q - 1:q, :]
    to_end = jnp.exp(acs_end - acs_x)
    chunk_decay = jnp.exp(acs_end)
    xdt_end = xdt * to_end

    lane_head = lax.broadcasted_iota(jnp.int32, (1, gw), 1) // SSD_HEAD_DIM
    ys = []
    for g in range(SSD_GROUPS):
        bm = xc[:, SSD_INNER + g * SSD_STATE:SSD_INNER + (g + 1) * SSD_STATE].astype(BF16)
        cm = xc[:, SSD_INNER + (SSD_GROUPS + g) * SSD_STATE:
                SSD_INNER + (SSD_GROUPS + g + 1) * SSD_STATE].astype(BF16)
        cb = lax.dot_general(cm, bm, (((1,), (1,)), ((), ())), preferred_element_type=F32)
        xdt_g = xdt[:, g * gw:(g + 1) * gw]
        gmats, rhs = [], []
        for j in range(hpg):
            hh = g * hpg + j
            seg = acs[:, hh:hh + 1] - acs_t[hh:hh + 1, :]
            decay = jnp.exp(jnp.where(causal, seg, -jnp.inf))
            gmats.append((cb * decay).astype(BF16))
            rhs.append(jnp.where(lane_head == j, xdt_g, 0.0).astype(BF16))
        y_diag = _dot(jnp.concatenate(gmats, axis=1), jnp.concatenate(rhs, axis=0))
        h_prev = h_scr[g]
        y_off = _dot(cm, h_prev.astype(BF16)) * decay_in[:, g * gw:(g + 1) * gw]
        upd = lax.dot_general(bm, xdt_end[:, g * gw:(g + 1) * gw].astype(BF16),
                              (((0,), (0,)), ((), ())), preferred_element_type=F32)
        h_scr[g] = chunk_decay[:, g * gw:(g + 1) * gw] * h_prev + upd
        ys.append(y_diag + y_off)
    y = jnp.concatenate(ys, axis=1) + dx_ref[...] * xs
    y = y * _silu(z_ref[...])
    outs = []
    for g in range(SSD_GROUPS):
        yg = y[:, g * gw:(g + 1) * gw]
        outs.append(yg * lax.rsqrt(jnp.mean(yg * yg, axis=-1, keepdims=True) + EPS))
    o_ref[...] = (jnp.concatenate(outs, axis=1) * ng_ref[...]).astype(o_ref.dtype)


def _ssd(z, xbc, dt, conv_w, conv_b, dt_bias, a_log, d_x, norm_gain, expand, bsz, seq):
    m = z.shape[0]
    q = min(SSD_Q, seq)
    nq = seq // q
    nb = math.gcd(bsz, SSD_PER_STEP)
    const = lambda b, t: (0, 0)
    per_seq = lambda v: v.reshape(bsz, seq, v.shape[-1])
    out = pl.pallas_call(
        _ssd_kernel,
        out_shape=jax.ShapeDtypeStruct((bsz, seq, SSD_INNER), BF16),
        grid=(bsz // nb, nq),
        in_specs=[
            pl.BlockSpec((nb, q, SSD_INNER), lambda b, t: (b, t, 0)),
            pl.BlockSpec((nb, q, SSD_XBC), lambda b, t: (b, t, 0)),
            pl.BlockSpec((nb, 8, SSD_XBC), lambda b, t: (b, jnp.maximum(t * (q // 8) - 1, 0), 0)),
            pl.BlockSpec((nb, SSD_HEADS, q), lambda b, t: (b, 0, t)),
            pl.BlockSpec(conv_w.shape, const),
            pl.BlockSpec(conv_b.shape, const),
            pl.BlockSpec(dt_bias.shape, const),
            pl.BlockSpec(a_log.shape, const),
            pl.BlockSpec(d_x.shape, const),
            pl.BlockSpec(norm_gain.shape, const),
            pl.BlockSpec(expand.shape, const),
        ],
        out_specs=pl.BlockSpec((nb, q, SSD_INNER), lambda b, t: (b, t, 0)),
        scratch_shapes=[pltpu.VMEM((nb, SSD_GROUPS, SSD_STATE, SSD_INNER // SSD_GROUPS), F32)],
        compiler_params=_params("parallel", "arbitrary"),
        name="ssd",
    )(per_seq(z), per_seq(xbc), per_seq(xbc), dt, conv_w, conv_b, dt_bias, a_log, d_x,
      norm_gain, expand)
    return out.reshape(m, SSD_INNER)


def _cpow(base_re, base_im, exponent, nbits):
    shape = jnp.broadcast_shapes(base_re.shape, exponent.shape)
    pr = jnp.ones(shape, F32)
    pi = jnp.zeros(shape, F32)
    sr, si = base_re, base_im
    for k in range(nbits):
        bit = ((exponent >> k) & 1) == 1
        nr = pr * sr - pi * si
        ni = pr * si + pi * sr
        pr = jnp.where(bit, nr, pr)
        pi = jnp.where(bit, ni, pi)
        if k + 1 < nbits:
            sr, si = sr * sr - si * si, 2.0 * sr * si
    return pr, pi


def _s5_discretize(ar, ai, log_dt):
    dt = jnp.exp(log_dt)
    mag = jnp.exp(dt * ar)
    lam_re = mag * jnp.cos(dt * ai)
    lam_im = mag * jnp.sin(dt * ai)
    den = ar * ar + ai * ai
    f_re = ((lam_re - 1.0) * ar + lam_im * ai) / den
    f_im = (lam_im * ar - (lam_re - 1.0) * ai) / den
    return lam_re, lam_im, f_re, f_im


def _s5_kernel(u_ref, arow_ref, ldt_ref, bd_ref, cd_re_ref, cd_im_ref, d_ref, y_ref, *scratch):
    n = u_ref.shape[1]
    scr = [scratch[3 * gg:3 * gg + 3] for gg in range(n)]
    lams = [_s5_prepare(arow_ref.at[gg], ldt_ref.at[gg], bd_ref.at[gg], cd_re_ref.at[gg],
                        cd_im_ref.at[gg], *scr[gg], u_ref.shape[-1]) for gg in range(n)]
    for gg in range(n):
        _s5_apply(lams[gg], u_ref.at[:, gg], d_ref.at[gg], y_ref.at[:, gg], *scr[gg][1:])


def _s5_prepare(arow_ref, ldt_ref, bd_ref, cd_re_ref, cd_im_ref, kmat_scr, t_scr, f_scr, q):
    gsz = S5_GROUP
    p2 = 2 * S5_STATE
    nbits = (q - 1).bit_length()
    log_dt = ldt_ref[...]

    lane = lax.broadcasted_iota(jnp.int32, (1, p2), 1)
    first_half_l = lane < S5_STATE
    rowp = lax.broadcasted_iota(jnp.int32, (p2, 1), 0)
    first_half_r = rowp < S5_STATE

    ar_r = arow_ref[0:1, :]
    ai_r = arow_ref[1:2, :]
    lr, li, fr, fi = _s5_discretize(ar_r, ai_r, log_dt)
    sign_b = jnp.where(first_half_l, -1.0, 1.0)
    bd1 = bd_ref[...]
    bd2 = pltpu.roll(bd1, S5_STATE, 1) * sign_b
    bb_a = fr * bd1 + fi * bd2
    bb_b = fr * bd2 - fi * bd1

    srow = lax.broadcasted_iota(jnp.int32, (q, 1), 0)
    pr_re, pr_im = _cpow(lr, li, (q - 1) - srow, nbits)
    for i in range(gsz):
        es = pr_re * bb_a[i:i + 1, :] + pr_im * bb_b[i:i + 1, :]
        t_scr[i * q:(i + 1) * q, gsz * q:gsz * q + p2] = es.astype(BF16)

    pt_re, pt_im = _cpow(lr, li, srow, nbits)
    p1_re = pt_re * lr - pt_im * li
    p1_im = pt_re * li + pt_im * lr
    cre = cd_re_ref[...]
    cim = cd_im_ref[...]
    for o in range(gsz):
        c_r = cre[o:o + 1, :]
        c_i = cim[o:o + 1, :]
        f_o = jnp.where(first_half_l, c_r * p1_re - c_i * p1_im, -(c_r * p1_im + c_i * p1_re))
        f_scr[o * q:(o + 1) * q, :] = f_o.astype(BF16)

    eye = (lax.broadcasted_iota(jnp.int32, (p2, p2), 0)
           == lax.broadcasted_iota(jnp.int32, (p2, p2), 1))
    lcr = jnp.sum(jnp.where(eye, lr, 0.0), axis=1, keepdims=True)
    lci = jnp.sum(jnp.where(eye, li, 0.0), axis=1, keepdims=True)
    dlane = lax.broadcasted_iota(jnp.int32, (1, q), 1)
    pw_re, pw_im = _cpow(lcr, lci, dlane, nbits)
    pow_stack = jnp.where(first_half_r, pw_re, pw_im)

    v1 = bb_a * jnp.where(first_half_l, 1.0, -1.0)
    v2 = -pltpu.roll(bb_a, S5_STATE, 1)
    coef =cre[:, None, :] * v1[None, :, :] + cim[:, None, :] * v2[None, :, :]
    kmat_scr[...] = _dot_exact(coef.reshape(gsz * gsz, p2), pow_stack)

    trow = lax.broadcasted_iota(jnp.int32, (q, q), 0)
    tcol = lax.broadcasted_iota(jnp.int32, (q, q), 1)
    lower = tcol >= trow
    for i in range(gsz):
        for o in range(gsz):
            k_row = kmat_scr[o * gsz + i:o * gsz + i + 1, :]
            blk = pltpu.roll(jnp.broadcast_to(k_row, (q, q)), 0, 1, stride=1, stride_axis=0)
            t_scr[i * q:(i + 1) * q, o * q:(o + 1) * q] = jnp.where(lower, blk, 0.0).astype(BF16)
    return lr, li


def _s5_apply(lam, u_ref, d_ref, y_ref, t_scr, f_scr):
    lr, li = lam
    bsz, rows, q = u_ref.shape
    gsz = S5_GROUP
    nc = rows // gsz
    m = bsz * nc
    sign_b = jnp.where(lax.broadcasted_iota(jnp.int32, (1, 2 * S5_STATE), 1) < S5_STATE, -1.0, 1.0)
    u_f32 = [jnp.concatenate([u_ref[b, pl.ds(i, nc, stride=gsz), :] for b in range(bsz)], axis=0)
             for i in range(gsz)]
    u_cat = jnp.concatenate([v.astype(BF16) for v in u_f32], axis=1)
    acc = _dot(u_cat, t_scr[...])
    x = acc[:, gsz * q:]

    sq_re, sq_im = lr, li
    for _ in range(q.bit_length() - 1):
        sq_re, sq_im = sq_re * sq_re - sq_im * sq_im, 2.0 * sq_re * sq_im
    crow = lax.broadcasted_iota(jnp.int32, (m, 1), 0) % nc
    k = 1
    while k < nc:
        sh = jnp.where(crow >= k, pltpu.roll(x, k, 0), 0.0)
        x = x + sq_re * sh + (sq_im * sign_b) * pltpu.roll(sh, S5_STATE, 1)
        sq_re, sq_im = sq_re * sq_re - sq_im * sq_im, 2.0 * sq_re * sq_im
        k *= 2
    h_prev = jnp.where(crow >= 1, pltpu.roll(x, 1, 0), 0.0)
    y = acc[:, :gsz * q] + lax.dot_general(h_prev.astype(BF16), f_scr[...], (((1,), (1,)), ((), ())),
                                           preferred_element_type=F32)
    for o in range(gsz):
        y_o = y[:, o * q:(o + 1) * q] + d_ref[o:o + 1, :] * u_f32[o]
        for b in range(bsz):
            y_ref[b, pl.ds(o, nc, stride=gsz), :] = y_o[b * nc:(b + 1) * nc]


def _s5(u5, arow, ldt, bd, cd_re, cd_im, d_rows):
    bsz, _, nc, _, q = u5.shape
    rows = nc * S5_GROUP
    p2 = 2 * S5_STATE
    per_g = lambda g: (g, 0, 0)
    n = S5_PER_STEP
    assert S5_GROUPS % n == 0
    y4 = pl.pallas_call(
        _s5_kernel,
        out_shape=jax.ShapeDtypeStruct((bsz, S5_GROUPS, rows, q), F32),
        grid=(S5_GROUPS // n,),
        in_specs=[
            pl.BlockSpec((bsz, n, rows, q), lambda g: (0, g, 0, 0)),
            pl.BlockSpec((n, 2, p2), per_g),
            pl.BlockSpec((n, 1, 1), per_g),
            pl.BlockSpec((n, S5_GROUP, p2), per_g),
            pl.BlockSpec((n, S5_GROUP, p2), per_g),
            pl.BlockSpec((n, S5_GROUP, p2), per_g),
            pl.BlockSpec((n, S5_GROUP, q), per_g),
        ],
        out_specs=pl.BlockSpec((bsz, n, rows, q), lambda g: (0, g, 0, 0)),
        scratch_shapes=n * [
            pltpu.VMEM((S5_GROUP * S5_GROUP, q), F32),
            pltpu.VMEM((S5_GROUP * q, S5_GROUP * q + p2), BF16),
            pltpu.VMEM((S5_GROUP * q, p2), BF16),
        ],
        compiler_params=_params("parallel"),
        name="s5",
    )(u5.reshape(bsz, S5_GROUPS, rows, q), arow, ldt, bd, cd_re, cd_im, d_rows)
    return y4.reshape(u5.shape)


def _gelu_tanh(x):
    return 0.5 * x * (1.0 + jnp.tanh(0.7978845608028654 * (x + 0.044715 * (x * x * x))))


def _outproj_even_ffn_kernel(x_ref, ya_ref, ybt_ref, wglut_ref, bglu_ref, wa_ref, wb_ref,
                             fg_ref, wg_ref, wu_ref, wd_ref, o_ref, wg_s, wu_s, wd_s, act_scr):
    i = pl.program_id(0)
    _ffn_stage(i, wg_ref, wu_ref, wd_ref, wg_s, wu_s, wd_s)

    @pl.when(i >= wg_s.shape[0])
    def _():
        q = ybt_ref.shape[-1]
        ybt = jnp.concatenate([ybt_ref[0, :, cc].reshape(S5_WIDTH, q) for cc in range(ybt_ref.shape[2])],
                              axis=1)
        gate = _dot(wglut_ref[...], _gelu_tanh(ybt).astype(BF16)) + bglu_ref[...]
        yb = (ybt * jax.nn.sigmoid(gate)).T.astype(BF16)
        x2 = x_ref[...] + _dot(ya_ref[...], wa_ref[...]) + _dot(yb, wb_ref[...])
        o_ref[...] = _ffn_apply(x2, fg_ref, wg_s, wu_s, wd_s, act_scr)


def _outproj_even_ffn(x2d, ya, yb5, wglut, bglu, wa, wb, ffn, layer, bsz, seq):
    m, d = x2d.shape
    tm = min(FFN_TM, seq)
    per_b = seq // tm
    q = yb5.shape[-1]
    ns = _ffn_chunks(ffn)
    tile = lambda i: jnp.maximum(i - ns, 0)
    return pl.pallas_call(
        _outproj_even_ffn_kernel,
        out_shape=jax.ShapeDtypeStruct((m, d), F32),
        grid=(ns + m // tm,),
        in_specs=[
            pl.BlockSpec((tm, d), lambda i: (tile(i), 0)),
            pl.BlockSpec((tm, SSD_INNER), lambda i: (tile(i), 0)),
            pl.BlockSpec((1, S5_GROUPS, tm // q, S5_GROUP, q),
                         lambda i: (tile(i) // per_b, 0, tile(i) % per_b, 0, 0)),
            _resident(wglut.shape), _resident(bglu.shape), _resident(wa.shape), _resident(wb.shape),
        ] + _ffn_specs(*ffn, layer),
        out_specs=pl.BlockSpec((tm, d), lambda i: (tile(i), 0)),
        scratch_shapes=_ffn_scratch(ffn, tm),
        compiler_params=_params("arbitrary"),
        name="outproj_even_ffn",
    )(x2d, ya, yb5, wglut, bglu, wa, wb, *ffn)


def _head_rmsnorm(x, gain):
    assert LANES == 2 * ATT_HEAD_DIM
    first = lax.broadcasted_iota(jnp.int32, (1, LANES), 1) < ATT_HEAD_DIM
    tiles = []
    for p in range(x.shape[1] // LANES):
        xp = x[:, p * LANES:(p + 1) * LANES]
        sq = xp * xp
        s0 = jnp.sum(jnp.where(first, sq, 0.0), axis=-1, keepdims=True)
        s1 = jnp.sum(jnp.where(first, 0.0, sq), axis=-1, keepdims=True)
        ss = jnp.where(first, s0, s1)
        tiles.append(xp * lax.rsqrt(ss * (1.0 / ATT_HEAD_DIM) + EPS))
    return jnp.concatenate(tiles, axis=1) * gain


def _head_rmsnorm_t(xt, gain_col):
    c, n = xt.shape
    x3 = xt.reshape(c // ATT_HEAD_DIM, ATT_HEAD_DIM, n)
    ms = jnp.mean(x3 * x3, axis=1, keepdims=True)
    return (x3 * lax.rsqrt(ms + EPS)).reshape(c, n) * gain_col


def _pool_apply(u, prev, pos, w_ref, sc_ref):
    ext = jnp.concatenate([prev, u], axis=0)
    sums = {}
    s = ext
    w = 1
    while w < POOL_MAX:
        s = s + pltpu.roll(s, w, 0)
        w *= 2
        sums[w] = s[POOL_MAX:]
    outs = []
    for g, win in enumerate(POOL_WINDOWS):
        sl = slice(g * POOL_GROUP, (g + 1) * POOL_GROUP)
        count = jnp.minimum(pos + 1, win).astype(F32)
        pooled = sums[win][:, sl] / count - u[:, sl]
        outs.append(_dot(pooled.astype(BF16), w_ref[g]))
    return jnp.concatenate(outs, axis=1) * sc_ref[...]


def _ffn_inproj_odd_kernel(x_ref, fg_ref, wg_ref, wu_ref, wd_ref, g_ref, wqt_ref, wk_ref, wvt_ref,
                           wup_ref, qg_ref, kg_ref, pw_ref, ps_ref,
                           x1_ref, qt_ref, k_ref, vt_ref, yd_ref, wg_s, wu_s, wd_s, act_scr, carry_scr,
                           *, tiles_per_seq):
    i = pl.program_id(0)
    tm = x_ref.shape[0]
    n_stage = wg_s.shape[0]
    _ffn_stage(i, wg_ref, wu_ref, wd_ref, wg_s, wu_s, wd_s)

    @pl.when(i == 0)
    def _():
        carry_scr[...] = jnp.zeros_like(carry_scr)

    @pl.when(i >= n_stage)
    def _():
        x1 = _ffn_apply(x_ref[...], fg_ref, wg_s, wu_s, wd_s, act_scr)
        x1_ref[...] = x1
        h = _rmsnorm_bf16(x1, g_ref[...])
        nt = (((1,), (1,)), ((), ()))
        qt = lax.dot_general(wqt_ref[...], h, nt, preferred_element_type=F32)
        qt = _head_rmsnorm_t(qt, qg_ref[...]) * (ATT_HEAD_DIM ** -0.5 * LOG2E)
        qt_ref[0] = qt.astype(BF16)
        k_ref[...] = _head_rmsnorm(_dot(h, wk_ref[...]), kg_ref[...]).astype(BF16)
        vt_ref[0] = lax.dot_general(wvt_ref[...], h, nt, preferred_element_type=F32).astype(BF16)
        u = _dot(h, wup_ref[...])
        tile_in_seq = (i - n_stage) % tiles_per_seq
        prev = jnp.where(tile_in_seq == 0, 0.0, carry_scr[...])
        pos = tile_in_seq * tm + lax.broadcasted_iota(jnp.int32, (tm, 1), 0)
        yd_ref[...] = _pool_apply(u, prev, pos, pw_ref, ps_ref).astype(yd_ref.dtype)
        carry_scr[...] = u[tm - POOL_MAX:]


def _ffn_inproj_odd(x2d, ffn, gain, wqt, wk, wvt, wup, qg_col, kg, w_pool, pool_scale, layer, bsz, seq):
    m, d = x2d.shape
    tm = min(FFN_TM, seq)
    per_b = seq // tm
    w = ATT_WIDTH
    ns = _ffn_chunks(ffn)
    tl = lambda i: jnp.maximum(i - ns, 0)
    tile = pl.BlockSpec((tm, w), lambda i: (tl(i), 0))
    tile_t = pl.BlockSpec((1, w, tm), lambda i: (tl(i) // per_b, 0, tl(i) % per_b))
    row_tile = pl.BlockSpec((tm, d), lambda i: (tl(i), 0))
    tok = jax.ShapeDtypeStruct((m, w), BF16)
    chan = jax.ShapeDtypeStruct((bsz, w, seq), BF16)
    return pl.pallas_call(
        functools.partial(_ffn_inproj_odd_kernel, tiles_per_seq=per_b),
        out_shape=(jax.ShapeDtypeStruct((m, d), F32), chan, tok, chan, tok),
        grid=(ns + m // tm,),
        in_specs=[row_tile] + _ffn_specs(*ffn, layer) + [
            pl.BlockSpec((None, 1, d), lambda i: (layer, 0, 0)),
            _resident(wqt.shape), _resident(wk.shape), _resident(wvt.shape), _resident(wup.shape),
            _resident(qg_col.shape), _resident(kg.shape),
            _resident(w_pool.shape), _resident(pool_scale.shape),
        ],
        out_specs=(row_tile, tile_t, tile, tile_t, tile),
        scratch_shapes=_ffn_scratch(ffn, tm) + [pltpu.VMEM((POOL_MAX, w), F32)],
        compiler_params=_params("arbitrary"),
        name="ffn_inproj_odd",
    )(x2d, *ffn, gain, wqt, wk, wvt, wup, qg_col, kg, w_pool, pool_scale)


def _attn_kernel(qt_ref, k0_ref, k1_ref, k2_ref, vt0_ref, vt1_ref, vt2_ref, brow_ref, o_ref, bias_ref):
    t = pl.program_id(1)
    tq = o_ref.shape[0]
    nk = 3 * tq

    @pl.when((pl.program_id(0) == 0) & (t == 0))
    def _():
        keyi = lax.broadcasted_iota(jnp.int32, (nk, tq), 0)
        qryi = lax.broadcasted_iota(jnp.int32, (nk, tq), 1)
        rel_chunk = keyi // ATT_CHUNK - (2 * tq // ATT_CHUNK - LEFT_CHUNKS) - qryi // ATT_CHUNK
        rel_chunk = jnp.where(rel_chunk >= 0, rel_chunk, LEFT_CHUNKS + 1)
        for hh in range(ATT_HEADS):
            base = jnp.broadcast_to(brow_ref[hh:hh + 1, :], (nk, 4 * tq))
            shifted = pltpu.roll(base, 0, 1, stride=1, stride_axis=0)
            bias = shifted[:, nk:] * LOG2E
            for var in range(3):
                first_key = (2 - var) * tq
                keep = jnp.where(keyi >= first_key, rel_chunk, LEFT_CHUNKS + 1) <= LEFT_CHUNKS
                bias_ref[var, hh] = jnp.where(keep, bias, -jnp.inf)

    var = jnp.minimum(t, 2)
    qt = qt_ref[0]
    kcat = jnp.concatenate([k0_ref[...], k1_ref[...], k2_ref[...]], axis=0)
    vtcat = jnp.concatenate([vt0_ref[0], vt1_ref[0], vt2_ref[0]], axis=1)
    hd = ATT_HEAD_DIM
    heads_per_slab = LANES // hd
    no_q = jnp.zeros((LANES - hd, tq), qt.dtype)
    ones_rows = jnp.ones((16, nk), qt.dtype)
    n_tiles = nk // tq

    def scores(head, j):
        p, hh = divmod(head, heads_per_slab)
        rows = slice(head * hd, (head + 1) * hd)
        qth = jnp.concatenate([qt[rows]] + [no_q] if hh == 0 else [no_q] + [qt[rows]], axis=0)
        ks = slice(j * tq, (j + 1) * tq)
        return _dot(kcat[ks, p * LANES:(p + 1) * LANES], qth) + bias_ref[var, head, ks, :]

    items = [(head, j) for head in range(ATT_HEADS) for j in range(n_tiles)]
    ahead = [scores(*it) for it in items[:ATT_LOOKAHEAD]]
    outs, parts, tops = [], [], []
    for idx, (head, j) in enumerate(items):
        st = ahead.pop(0)
        if idx + ATT_LOOKAHEAD < len(items):
            ahead.append(scores(*items[idx + ATT_LOOKAHEAD]))
        rows = slice(head * hd, (head + 1) * hd)
        ks = slice(j * tq, (j + 1) * tq)
        top = jnp.maximum(jnp.max(st, axis=0, keepdims=True), MASKED_FLOOR)
        e = jnp.exp2(st - top).astype(BF16)
        vth = jnp.concatenate([vtcat[rows, ks], ones_rows[:, ks]], axis=0)
        parts.append(_dot(vth, e))
        tops.append(top)
        if j == n_tiles - 1:
            top_all = functools.reduce(jnp.maximum, tops)
            ot = sum(part * jnp.exp2(tp - top_all) for part, tp in zip(parts, tops))
            outs.append(ot[:hd] / ot[hd:hd + 1])
            parts, tops = [], []
            if head % heads_per_slab == heads_per_slab - 1:
                p = head // heads_per_slab
                o_ref[:, p * LANES:(p + 1) * LANES] = jnp.concatenate(outs, axis=0).T.astype(o_ref.dtype)
                outs = []


def _attention(qt, k, vt, brow, bsz, seq):
    m, w = k.shape
    tq = ATT_T
    assert 2 * tq >= LEFT_CHUNKS * ATT_CHUNK and tq > MAX_REL and seq % tq == 0
    nt = seq // tq
    back = lambda n: (lambda b, t: (b * nt + jnp.maximum(t - n, 0), 0))
    back_t = lambda n: (lambda b, t: (b, 0, jnp.maximum(t - n, 0)))
    tok = lambda f: pl.BlockSpec((tq, w), f)
    chan = lambda f: pl.BlockSpec((1, w, tq), f)
    return pl.pallas_call(
        _attn_kernel,
        out_shape=jax.ShapeDtypeStruct((m, w), BF16),
        grid=(bsz, nt),
        in_specs=[chan(back_t(0)), tok(back(2)), tok(back(1)), tok(back(0)),
                  chan(back_t(2)), chan(back_t(1)), chan(back_t(0)),
                  pl.BlockSpec(brow.shape, lambda b, t: (0, 0))],
        out_specs=tok(back(0)),
        scratch_shapes=[pltpu.VMEM((3, ATT_HEADS, 3 * tq, tq), F32)],
        compiler_params=_params("arbitrary", "arbitrary"),
        name="band_attention",
    )(qt, k, k, k, vt, vt, vt, brow)


def _bias_rows(rel_bias, tq):
    rb = rel_bias.astype(F32)
    nh = rb.shape[0]
    near = jnp.broadcast_to(rb[:, :1], (nh, tq - MAX_REL))
    far = jnp.broadcast_to(rb[:, -1:], (nh, 3 * tq - MAX_REL - 1))
    return jnp.concatenate([near, rb, far], axis=1)


def _outproj_odd_ffn_kernel(x_ref, yc_ref, yd_ref, wa_ref, wb_ref, fg_ref, wg_ref, wu_ref, wd_ref,
                            o_ref, wg_s, wu_s, wd_s, act_scr):
    i = pl.program_id(0)
    _ffn_stage(i, wg_ref, wu_ref, wd_ref, wg_s, wu_s, wd_s)

    @pl.when(i >= wg_s.shape[0])
    def _():
        x2 = x_ref[...] + _dot(yc_ref[...], wa_ref[...]) + _dot(yd_ref[...], wb_ref[...])
        o_ref[...] = _ffn_apply(x2, fg_ref, wg_s, wu_s, wd_s, act_scr)


def _outproj_odd_ffn(x2d, yc, yd, wa, wb, ffn, layer):
    m, d = x2d.shape
    tm = min(OUT_TM, m)
    ns = _ffn_chunks(ffn)
    tile = lambda i: jnp.maximum(i - ns, 0)
    return pl.pallas_call(
        _outproj_odd_ffn_kernel,
        out_shape=jax.ShapeDtypeStruct((m, d), F32),
        grid=(ns + m // tm,),
        in_specs=[
            pl.BlockSpec((tm, d), lambda i: (tile(i), 0)),
            pl.BlockSpec((tm, ATT_WIDTH), lambda i: (tile(i), 0)),
            pl.BlockSpec((tm, ATT_WIDTH), lambda i: (tile(i), 0)),
            _resident(wa.shape), _resident(wb.shape),
        ] + _ffn_specs(*ffn, layer),
        out_specs=pl.BlockSpec((tm, d), lambda i: (tile(i), 0)),
        scratch_shapes=_ffn_scratch(ffn, tm),
        compiler_params=_params("arbitrary"),
        name="outproj_odd_ffn",
    )(x2d, yc, yd, wa, wb, *ffn)


def _even_layer(x2d, ffn1, ffn2, gain, layer, bsz, seq, even_w_in, even_w_out, ssd_conv_w, ssd_conv_b,
                ssd_dt_bias, ssd_a_log, ssd_d, ssd_norm, s5_a_re, s5_a_im, s5_log_dt, s5_b_re, s5_b_im,
                s5_c_re, s5_c_im, s5_d, s5_w_glu, s5_b_glu):
    i = layer // 2
    w_in = even_w_in[i]
    o1 = SSD_INNER
    o2 = o1 + SSD_XBC
    o3 = o2 + SSD_HEADS
    wz = w_in[:, :o1].astype(BF16)
    wx = w_in[:, o1:o2].astype(BF16)
    wut = jnp.concatenate([w_in[:, o3:].T, w_in[:, o2:o3].T], axis=0)
    wut = jnp.pad(wut, ((0, -wut.shape[0] % 16), (0, 0))).astype(BF16)
    x1, z, xbc, dtt, ut = _ffn_inproj_even(x2d, ffn1, gain, wz, wx, wut, layer, bsz, seq)

    col = lambda v: v.astype(F32)[:, None]
    expand = (jnp.arange(LANES)[:, None] == (jnp.arange(SSD_INNER)[None, :] // SSD_HEAD_DIM)).astype(BF16)
    ya = _ssd(z, xbc, dtt, ssd_conv_w[i], ssd_conv_b[i][None, :], col(ssd_dt_bias[i]),
              col(ssd_a_log[i]), jnp.repeat(ssd_d[i], SSD_HEAD_DIM)[None, :], ssd_norm[i][None, :],
              expand, bsz, seq)

    q = min(S5_Q, seq)
    are, aim = s5_a_re[i], s5_a_im[i]
    dup = lambda v: jnp.concatenate([v, v], axis=-1)
    arow = jnp.stack([dup(are), dup(aim)], axis=1)
    ldt = s5_log_dt[i][:, None, None]
    bd = jnp.concatenate([jnp.swapaxes(s5_b_re[i], 1, 2), jnp.swapaxes(s5_b_im[i], 1, 2)], axis=-1)
    cd_re, cd_im = dup(s5_c_re[i]), dup(s5_c_im[i])
    d_rows = jnp.broadcast_to(s5_d[i][:, :, None], (S5_GROUPS, S5_GROUP, q))
    yb5 = _s5(ut, arow, ldt, bd, cd_re, cd_im, d_rows)

    w_out = even_w_out[i]
    return _outproj_even_ffn(x1, ya, yb5, s5_w_glu[i].T.astype(BF16), s5_b_glu[i][:, None],
                             w_out[:SSD_INNER].astype(BF16), w_out[SSD_INNER:].astype(BF16),
                             ffn2, layer, bsz, seq)


def _odd_layer(x2d, ffn1, ffn2, gain, layer, bsz, seq, odd_w_in, odd_w_out, attn_q_norm, attn_k_norm,
               attn_rel_bias, pool_w, pool_scale):
    i = layer // 2
    w_in = odd_w_in[i].astype(BF16)
    w = ATT_WIDTH
    tile_gain = lambda v: jnp.tile(v.astype(F32), ATT_HEADS)[None, :]
    x1, qt, k, vt, yd = _ffn_inproj_odd(
        x2d, ffn1, gain, w_in[:, :w].T, w_in[:, w:2 * w], w_in[:, 2 * w:3 * w].T, w_in[:, 3 * w:],
        tile_gain(attn_q_norm[i]).T, tile_gain(attn_k_norm[i]),
        pool_w[i].astype(BF16), pool_scale[i][None, :], layer, bsz, seq)
    yc = _attention(qt, k, vt, _bias_rows(attn_rel_bias[i], ATT_T), bsz, seq)
    w_out = odd_w_out[i]
    return _outproj_odd_ffn(x1, yc, yd, w_out[:w].astype(BF16), w_out[w:].astype(BF16), ffn2, layer)


def kernel(x, ffn1_norm, ffn1_w_gate, ffn1_w_up, ffn1_w_down, mix_norm, even_w_in, even_w_out, ssd_conv_w, ssd_conv_b, ssd_dt_bias, ssd_a_log, ssd_d, ssd_norm, s5_a_re, s5_a_im, s5_log_dt, s5_b_re, s5_b_im, s5_c_re, s5_c_im, s5_d, s5_w_glu, s5_b_glu, odd_w_in, odd_w_out, attn_q_norm, attn_k_norm, attn_rel_bias, pool_w, pool_scale, ffn2_norm, ffn2_w_gate, ffn2_w_up, ffn2_w_down):
    bsz, seq, d = x.shape
    depth = ffn1_norm.shape[0]
    x2d = x.reshape(bsz * seq, d)
    g3 = lambda v: v.astype(F32)[:, None, :]
    f1n, mxn, f2n = g3(ffn1_norm), g3(mix_norm), g3(ffn2_norm)
    as32 = lambda v: v.astype(F32)
    ffn1 = (f1n, as32(ffn1_w_gate), as32(ffn1_w_up), as32(ffn1_w_down))
    ffn2 = (f2n, as32(ffn2_w_gate), as32(ffn2_w_up), as32(ffn2_w_down))
    for layer in range(depth):
        if layer % 2 == 0:
            x2d = _even_layer(x2d, ffn1, ffn2, mxn, layer, bsz, seq, even_w_in, even_w_out, ssd_conv_w,
                              ssd_conv_b, ssd_dt_bias, ssd_a_log, ssd_d, ssd_norm, s5_a_re, s5_a_im,
                              s5_log_dt, s5_b_re, s5_b_im, s5_c_re, s5_c_im, s5_d, s5_w_glu, s5_b_glu)
        else:
            x2d = _odd_layer(x2d, ffn1, ffn2, mxn, layer, bsz, seq, odd_w_in, odd_w_out, attn_q_norm,
                             attn_k_norm, attn_rel_bias, pool_w, pool_scale)
    return x2d.reshape(bsz, seq, d)
```

```python
import functools
import math

import jax
import jax.numpy as jnp
from jax import lax
from jax.experimental import pallas as pl
from jax.experimental.pallas import tpu as pltpu

F32 = jnp.float32
BF16 = jnp.bfloat16
HIGHEST = lax.Precision.HIGHEST

EPS = 1e-6
LOG2E = 1.4426950408889634
MASKED_FLOOR = -1e30
LANES = 128
VMEM_LIMIT = 56 * 1024 * 1024

SSD_HEADS = 8
SSD_HEAD_DIM = 64
SSD_INNER = 512
SSD_GROUPS = 2
SSD_STATE = 128
SSD_CONV = 4
SSD_XBC = 1024
S5_WIDTH = 512
S5_GROUP = 16
S5_GROUPS = 32
S5_STATE = 64
ATT_HEADS = 8
ATT_HEAD_DIM = 64
ATT_WIDTH = 512
ATT_CHUNK = 64
LEFT_CHUNKS = 8
MAX_REL = 128
POOL_WINDOWS = (2, 4, 8, 16)
POOL_GROUP = 128
POOL_MAX = 16

FFN_TM = 512
OUT_TM = 1024
FFN_TF = 256
SSD_Q = 128
SSD_PER_STEP = 4
S5_Q = 128
S5_PER_STEP = 2
ATT_T = 256
ATT_PER_STEP = 2
ATT_LOOKAHEAD = 5


def _params(*sem):
    return pltpu.CompilerParams(dimension_semantics=sem, vmem_limit_bytes=VMEM_LIMIT)


def _dot(a, b):
    return jnp.dot(a, b, preferred_element_type=F32)


def _dot_exact(a, b):
    return jnp.dot(a, b, preferred_element_type=F32, precision=HIGHEST)


def _split3(x):
    hi = x.astype(BF16)
    rest = x - hi.astype(F32)
    mid = rest.astype(BF16)
    lo = (rest - mid.astype(F32)).astype(BF16)
    return hi, mid, lo


def _select_right(x, sel):
    hi, mid, lo = _split3(x)
    return _dot(hi, sel) + _dot(mid, sel) + _dot(lo, sel)


def _rmsnorm_bf16(x, gain):
    ms = jnp.mean(x * x, axis=-1, keepdims=True)
    return (x * lax.rsqrt(ms + EPS) * gain).astype(BF16)


def _silu(x):
    return x * jax.nn.sigmoid(x)


def _ffn_stage(i, wg_ref, wu_ref, wd_ref, wg_s, wu_s, wd_s):
    @pl.when(i < wg_s.shape[0])
    def _():
        wg_s[i] = wg_ref[...].astype(BF16)
        wu_s[i] = wu_ref[...].astype(BF16)
        wd_s[pl.ds(pl.multiple_of(i * FFN_TF, FFN_TF), FFN_TF), :] = wd_ref[...].astype(BF16)


def _ffn_apply(x, g_ref, wg_s, wu_s, wd_s, act_scr):
    h = _rmsnorm_bf16(x, g_ref[...])
    for c in range(wg_s.shape[0]):
        gate = _dot(h, wg_s[c])
        up = _dot(h, wu_s[c])
        act_scr[:, c * FFN_TF:(c + 1) * FFN_TF] = (_silu(gate) * up).astype(BF16)
    return x + 0.5 * _dot(act_scr[...], wd_s[...])


def _resident(shape):
    return pl.BlockSpec(shape, lambda i: (0,) * len(shape), pipeline_mode=pl.Buffered(1))


def _ffn_chunks(ffn):
    dff = ffn[1].shape[-1]
    assert dff % FFN_TF == 0
    return dff // FFN_TF


def _ffn_specs(gain, wg, wu, wd, layer):
    d, dff = wg.shape[1:]
    last = dff // FFN_TF - 1
    return [
        pl.BlockSpec((None, 1, d), lambda i: (layer, 0, 0)),
        pl.BlockSpec((None, d, FFN_TF), lambda i: (layer, 0, jnp.minimum(i, last))),
        pl.BlockSpec((None, d, FFN_TF), lambda i: (layer, 0, jnp.minimum(i, last))),
        pl.BlockSpec((None, FFN_TF, d), lambda i: (layer, jnp.minimum(i, last), 0)),
    ]


def _ffn_scratch(ffn, tm):
    d, dff = ffn[1].shape[1:]
    n = dff // FFN_TF
    return [pltpu.VMEM((n, d, FFN_TF), BF16), pltpu.VMEM((n, d, FFN_TF), BF16),
            pltpu.VMEM((dff, d), BF16), pltpu.VMEM((tm, dff), BF16)]


def _ffn_inproj_even_kernel(x_ref, fg_ref, wg_ref, wu_ref, wd_ref, g_ref, wz_ref, wx_ref,
                            wut_ref, x1_ref, z_ref, xbc_ref, dtt_ref, ut_ref, wg_s, wu_s, wd_s, act_scr):
    i = pl.program_id(0)
    _ffn_stage(i, wg_ref, wu_ref, wd_ref, wg_s, wu_s, wd_s)

    @pl.when(i >= wg_s.shape[0])
    def _():
        x1 = _ffn_apply(x_ref[...], fg_ref, wg_s, wu_s, wd_s, act_scr)
        x1_ref[...] = x1
        h = _rmsnorm_bf16(x1, g_ref[...])
        z_ref[...] = _dot(h, wz_ref[...])
        xbc_ref[...] = _dot(h, wx_ref[...])
        ut = lax.dot_general(wut_ref[...], h, (((1,), (1,)), ((), ())), preferred_element_type=F32)
        dtt_ref[0] = ut[S5_WIDTH:S5_WIDTH + SSD_HEADS, :]
        q = ut_ref.shape[-1]
        for cc in range(ut_ref.shape[2]):
            ut_ref[0, :, cc] = ut[:S5_WIDTH, cc * q:(cc + 1) * q].reshape(S5_GROUPS, S5_GROUP, q)


def _ffn_inproj_even(x2d, ffn, gain, wz, wx, wut, layer, bsz, seq):
    m, d = x2d.shape
    tm = min(FFN_TM, seq)
    per_b = seq // tm
    q = min(S5_Q, seq)
    ns = _ffn_chunks(ffn)
    tile = lambda i: jnp.maximum(i - ns, 0)
    return pl.pallas_call(
        _ffn_inproj_even_kernel,
        out_shape=(
            jax.ShapeDtypeStruct((m, d), F32),
            jax.ShapeDtypeStruct((m, SSD_INNER), F32),
            jax.ShapeDtypeStruct((m, SSD_XBC), F32),
            jax.ShapeDtypeStruct((bsz, SSD_HEADS, seq), F32),
            jax.ShapeDtypeStruct((bsz, S5_GROUPS, seq // q, S5_GROUP, q), F32),
        ),
        grid=(ns + m // tm,),
        in_specs=[pl.BlockSpec((tm, d), lambda i: (tile(i), 0))] + _ffn_specs(*ffn, layer) + [
            pl.BlockSpec((None, 1, d), lambda i: (layer, 0, 0)),
            _resident(wz.shape), _resident(wx.shape), _resident(wut.shape),
        ],
        out_specs=(
            pl.BlockSpec((tm, d), lambda i: (tile(i), 0)),
            pl.BlockSpec((tm, SSD_INNER), lambda i: (tile(i), 0)),
            pl.BlockSpec((tm, SSD_XBC), lambda i: (tile(i), 0)),
            pl.BlockSpec((1, SSD_HEADS, tm), lambda i: (tile(i) // per_b, 0, tile(i) % per_b)),
            pl.BlockSpec((1, S5_GROUPS, tm // q, S5_GROUP, q),
                         lambda i: (tile(i) // per_b, 0, tile(i) % per_b, 0, 0)),
        ),
        scratch_shapes=_ffn_scratch(ffn, tm),
        compiler_params=_params("arbitrary"),
        name="ffn_inproj_even",
    )(x2d, *ffn, gain, wz, wx, wut)


def _shift_rows(cur, prev8, j):
    rolled = pltpu.roll(cur, j, 0)
    prev_rolled = pltpu.roll(prev8, j, 0)
    row = lax.broadcasted_iota(jnp.int32, prev8.shape, 0)
    top = jnp.where(row < j, prev_rolled, rolled[:8])
    return jnp.concatenate([top, rolled[8:]], axis=0)


def _ssd_kernel(z_ref, xbc_ref, xprev_ref, dt_ref, *rest):
    consts, o_ref, h_scr = rest[:-2], rest[-2], rest[-1]

    @pl.when(pl.program_id(1) == 0)
    def _():
        h_scr[...] = jnp.zeros_like(h_scr)

    for bb in range(z_ref.shape[0]):
        _ssd_chunk(z_ref.at[bb], xbc_ref.at[bb], xprev_ref.at[bb], dt_ref.at[bb], *consts,
                   o_ref.at[bb], h_scr.at[bb])


def _ssd_chunk(z_ref, xbc_ref, xprev_ref, dt_ref, cw_ref, cb_ref, dtb_ref, alog_ref,
               dx_ref, ng_ref, e_ref, o_ref, h_scr):
    t = pl.program_id(1)
    q = z_ref.shape[0]
    gw = SSD_INNER // SSD_GROUPS
    hpg = SSD_HEADS // SSD_GROUPS

    xbc = xbc_ref[...]
    prev = jnp.where(t > 0, xprev_ref[...], 0.0)
    conv = xbc * cw_ref[SSD_CONV - 1:SSD_CONV, :] + cb_ref[...]
    for j in range(1, SSD_CONV):
        conv = conv + _shift_rows(xbc, prev, j) * cw_ref[SSD_CONV - 1 - j:SSD_CONV - j, :]
    xc = _silu(conv)
    xs = xc[:, :SSD_INNER]

    dtr = dt_ref[...] + dtb_ref[...]
    dt_t = jnp.maximum(dtr, 0.0) + jnp.log1p(jnp.exp(-jnp.abs(dtr)))
    a = -jnp.exp(alog_ref[...])
    row = lax.broadcasted_iota(jnp.int32, (q, q), 0)
    col = lax.broadcasted_iota(jnp.int32, (q, q), 1)
    causal = row >= col
    triu = jnp.where(row <= col, 1.0, 0.0).astype(BF16)
    acs_t = _select_right(dt_t * a, triu)
    pad = jnp.zeros((LANES - SSD_HEADS, q), F32)
    acs = jnp.concatenate([acs_t, pad], axis=0).T
    dt = jnp.concatenate([dt_t, pad], axis=0).T
    expanded = _select_right(jnp.concatenate([acs, dt], axis=0), e_ref[...])
    acs_x = expanded[:q]
    dt_x = expanded[q:]
    xdt = xs * dt_x
    decay_in = jnp.exp(acs_x)
    acs_end = acs_x[q - 1:q, :]
    to_end = jnp.exp(acs_end - acs_x)
    chunk_decay = jnp.exp(acs_end)
    xdt_end = xdt * to_end

    lane_head = lax.broadcasted_iota(jnp.int32, (1, gw), 1) // SSD_HEAD_DIM
    ys = []
    for g in range(SSD_GROUPS):
        bm = xc[:, SSD_INNER + g * SSD_STATE:SSD_INNER + (g + 1) * SSD_STATE].astype(BF16)
        cm = xc[:, SSD_INNER + (SSD_GROUPS + g) * SSD_STATE:
                SSD_INNER + (SSD_GROUPS + g + 1) * SSD_STATE].astype(BF16)
        cb = lax.dot_general(cm, bm, (((1,), (1,)), ((), ())), preferred_element_type=F32)
        xdt_g = xdt[:, g * gw:(g + 1) * gw]
        gmats, rhs = [], []
        for j in range(hpg):
            hh = g * hpg + j
            seg = acs[:, hh:hh + 1] - acs_t[hh:hh + 1, :]
            decay = jnp.exp(jnp.where(causal, seg, -jnp.inf))
            gmats.append((cb * decay).astype(BF16))
            rhs.append(jnp.where(lane_head == j, xdt_g, 0.0).astype(BF16))
        y_diag = _dot(jnp.concatenate(gmats, axis=1), jnp.concatenate(rhs, axis=0))
        h_prev = h_scr[g]
        y_off = _dot(cm, h_prev.astype(BF16)) * decay_in[:, g * gw:(g + 1) * gw]
        upd = lax.dot_general(bm, xdt_end[:, g * gw:(g + 1) * gw].astype(BF16),
                              (((0,), (0,)), ((), ())), preferred_element_type=F32)
        h_scr[g] = chunk_decay[:, g * gw:(g + 1) * gw] * h_prev + upd
        ys.append(y_diag + y_off)
    y = jnp.concatenate(ys, axis=1) + dx_ref[...] * xs
    y = y * _silu(z_ref[...])
    outs = []
    for g in range(SSD_GROUPS):
        yg = y[:, g * gw:(g + 1) * gw]
        outs.append(yg * lax.rsqrt(jnp.mean(yg * yg, axis=-1, keepdims=True) + EPS))
    o_ref[...] = (jnp.concatenate(outs, axis=1) * ng_ref[...]).astype(o_ref.dtype)


def _ssd(z, xbc, dt, conv_w, conv_b, dt_bias, a_log, d_x, norm_gain, expand, bsz, seq):
    m = z.shape[0]
    q = min(SSD_Q, seq)
    nq = seq // q
    nb = math.gcd(bsz, SSD_PER_STEP)
    const = lambda b, t: (0, 0)
    per_seq = lambda v: v.reshape(bsz, seq, v.shape[-1])
    out = pl.pallas_call(
        _ssd_kernel,
        out_shape=jax.ShapeDtypeStruct((bsz, seq, SSD_INNER), BF16),
        grid=(bsz // nb, nq),
        in_specs=[
            pl.BlockSpec((nb, q, SSD_INNER), lambda b, t: (b, t, 0)),
            pl.BlockSpec((nb, q, SSD_XBC), lambda b, t: (b, t, 0)),
            pl.BlockSpec((nb, 8, SSD_XBC), lambda b, t: (b, jnp.maximum(t * (q // 8) - 1, 0), 0)),
            pl.BlockSpec((nb, SSD_HEADS, q), lambda b, t: (b, 0, t)),
            pl.BlockSpec(conv_w.shape, const),
            pl.BlockSpec(conv_b.shape, const),
            pl.BlockSpec(dt_bias.shape, const),
            pl.BlockSpec(a_log.shape, const),
            pl.BlockSpec(d_x.shape, const),
            pl.BlockSpec(norm_gain.shape, const),
            pl.BlockSpec(expand.shape, const),
        ],
        out_specs=pl.BlockSpec((nb, q, SSD_INNER), lambda b, t: (b, t, 0)),
        scratch_shapes=[pltpu.VMEM((nb, SSD_GROUPS, SSD_STATE, SSD_INNER // SSD_GROUPS), F32)],
        compiler_params=_params("parallel", "arbitrary"),
        name="ssd",
    )(per_seq(z), per_seq(xbc), per_seq(xbc), dt, conv_w, conv_b, dt_bias, a_log, d_x,
      norm_gain, expand)
    return out.reshape(m, SSD_INNER)


def _cpow(base_re, base_im, exponent, nbits):
    shape = jnp.broadcast_shapes(base_re.shape, exponent.shape)
    pr = jnp.ones(shape, F32)
    pi = jnp.zeros(shape, F32)
    sr, si = base_re, base_im
    for k in range(nbits):
        bit = ((exponent >> k) & 1) == 1
        nr = pr * sr - pi * si
        ni = pr * si + pi * sr
        pr = jnp.where(bit, nr, pr)
        pi = jnp.where(bit, ni, pi)
        if k + 1 < nbits:
            sr, si = sr * sr - si * si, 2.0 * sr * si
    return pr, pi


def _s5_discretize(ar, ai, log_dt):
    dt = jnp.exp(log_dt)
    mag = jnp.exp(dt * ar)
    lam_re = mag * jnp.cos(dt * ai)
    lam_im = mag * jnp.sin(dt * ai)
    den = ar * ar + ai * ai
    f_re = ((lam_re - 1.0) * ar + lam_im * ai) / den
    f_im = (lam_im * ar - (lam_re - 1.0) * ai) / den
    return lam_re, lam_im, f_re, f_im


def _s5_kernel(u_ref, arow_ref, ldt_ref, bd_ref, cd_re_ref, cd_im_ref, d_ref, y_ref, *scratch):
    n = u_ref.shape[1]
    scr = [scratch[3 * gg:3 * gg + 3] for gg in range(n)]
    lams = [_s5_prepare(arow_ref.at[gg], ldt_ref.at[gg], bd_ref.at[gg], cd_re_ref.at[gg],
                        cd_im_ref.at[gg], *scr[gg], u_ref.shape[-1]) for gg in range(n)]
    for gg in range(n):
        _s5_apply(lams[gg], u_ref.at[:, gg], d_ref.at[gg], y_ref.at[:, gg], *scr[gg][1:])


def _s5_prepare(arow_ref, ldt_ref, bd_ref, cd_re_ref, cd_im_ref, kmat_scr, t_scr, f_scr, q):
    gsz = S5_GROUP
    p2 = 2 * S5_STATE
    nbits = (q - 1).bit_length()
    log_dt = ldt_ref[...]

    lane = lax.broadcasted_iota(jnp.int32, (1, p2), 1)
    first_half_l = lane < S5_STATE
    rowp = lax.broadcasted_iota(jnp.int32, (p2, 1), 0)
    first_half_r = rowp < S5_STATE

    ar_r = arow_ref[0:1, :]
    ai_r = arow_ref[1:2, :]
    lr, li, fr, fi = _s5_discretize(ar_r, ai_r, log_dt)
    sign_b = jnp.where(first_half_l, -1.0, 1.0)
    bd1 = bd_ref[...]
    bd2 = pltpu.roll(bd1, S5_STATE, 1) * sign_b
    bb_a = fr * bd1 + fi * bd2
    bb_b = fr * bd2 - fi * bd1

    srow = lax.broadcasted_iota(jnp.int32, (q, 1), 0)
    pr_re, pr_im = _cpow(lr, li, (q - 1) - srow, nbits)
    for i in range(gsz):
        es = pr_re * bb_a[i:i + 1, :] + pr_im * bb_b[i:i + 1, :]
        t_scr[i * q:(i + 1) * q, gsz * q:gsz * q + p2] = es.astype(BF16)

    pt_re, pt_im = _cpow(lr, li, srow, nbits)
    p1_re = pt_re * lr - pt_im * li
    p1_im = pt_re * li + pt_im * lr
    cre = cd_re_ref[...]
    cim = cd_im_ref[...]
    for o in range(gsz):
        c_r = cre[o:o + 1, :]
        c_i = cim[o:o + 1, :]
        f_o = jnp.where(first_half_l, c_r * p1_re - c_i * p1_im, -(c_r * p1_im + c_i * p1_re))
        f_scr[o * q:(o + 1) * q, :] = f_o.astype(BF16)

    eye = (lax.broadcasted_iota(jnp.int32, (p2, p2), 0)
           == lax.broadcasted_iota(jnp.int32, (p2, p2), 1))
    lcr = jnp.sum(jnp.where(eye, lr, 0.0), axis=1, keepdims=True)
    lci = jnp.sum(jnp.where(eye, li, 0.0), axis=1, keepdims=True)
    dlane = lax.broadcasted_iota(jnp.int32, (1, q), 1)
    pw_re, pw_im = _cpow(lcr, lci, dlane, nbits)
    pow_stack = jnp.where(first_half_r, pw_re, pw_im)

    v1 = bb_a * jnp.where(first_half_l, 1.0, -1.0)
    v2 = -pltpu.roll(bb_a, S5_STATE, 1)
    coef =cre[:, None, :] * v1[None, :, :] + cim[:, None, :] * v2[None, :, :]
    kmat_scr[...] = _dot_exact(coef.reshape(gsz * gsz, p2), pow_stack)

    trow = lax.broadcasted_iota(jnp.int32, (q, q), 0)
    tcol = lax.broadcasted_iota(jnp.int32, (q, q), 1)
    lower = tcol >= trow
    for i in range(gsz):
        for o in range(gsz):
            k_row = kmat_scr[o * gsz + i:o * gsz + i + 1, :]
            blk = pltpu.roll(jnp.broadcast_to(k_row, (q, q)), 0, 1, stride=1, stride_axis=0)
            t_scr[i * q:(i + 1) * q, o * q:(o + 1) * q] = jnp.where(lower, blk, 0.0).astype(BF16)
    return lr, li


def _s5_apply(lam, u_ref, d_ref, y_ref, t_scr, f_scr):
    lr, li = lam
    bsz, rows, q = u_ref.shape
    gsz = S5_GROUP
    nc = rows // gsz
    m = bsz * nc
    sign_b = jnp.where(lax.broadcasted_iota(jnp.int32, (1, 2 * S5_STATE), 1) < S5_STATE, -1.0, 1.0)
    u_f32 = [jnp.concatenate([u_ref[b, pl.ds(i, nc, stride=gsz), :] for b in range(bsz)], axis=0)
             for i in range(gsz)]
    u_cat = jnp.concatenate([v.astype(BF16) for v in u_f32], axis=1)
    acc = _dot(u_cat, t_scr[...])
    x = acc[:, gsz * q:]

    sq_re, sq_im = lr, li
    for _ in range(q.bit_length() - 1):
        sq_re, sq_im = sq_re * sq_re - sq_im * sq_im, 2.0 * sq_re * sq_im
    crow = lax.broadcasted_iota(jnp.int32, (m, 1), 0) % nc
    k = 1
    while k < nc:
        sh = jnp.where(crow >= k, pltpu.roll(x, k, 0), 0.0)
        x = x + sq_re * sh + (sq_im * sign_b) * pltpu.roll(sh, S5_STATE, 1)
        sq_re, sq_im = sq_re * sq_re - sq_im * sq_im, 2.0 * sq_re * sq_im
        k *= 2
    h_prev = jnp.where(crow >= 1, pltpu.roll(x, 1, 0), 0.0)
    y = acc[:, :gsz * q] + lax.dot_general(h_prev.astype(BF16), f_scr[...], (((1,), (1,)), ((), ())),
                                           preferred_element_type=F32)
    for o in range(gsz):
        y_o = y[:, o * q:(o + 1) * q] + d_ref[o:o + 1, :] * u_f32[o]
        for b in range(bsz):
            y_ref[b, pl.ds(o, nc, stride=gsz), :] = y_o[b * nc:(b + 1) * nc]


def _s5(u5, arow, ldt, bd, cd_re, cd_im, d_rows):
    bsz, _, nc, _, q = u5.shape
    rows = nc * S5_GROUP
    p2 = 2 * S5_STATE
    per_g = lambda g: (g, 0, 0)
    n = S5_PER_STEP
    assert S5_GROUPS % n == 0
    y4 = pl.pallas_call(
        _s5_kernel,
        out_shape=jax.ShapeDtypeStruct((bsz, S5_GROUPS, rows, q), F32),
        grid=(S5_GROUPS // n,),
        in_specs=[
            pl.BlockSpec((bsz, n, rows, q), lambda g: (0, g, 0, 0)),
            pl.BlockSpec((n, 2, p2), per_g),
            pl.BlockSpec((n, 1, 1), per_g),
            pl.BlockSpec((n, S5_GROUP, p2), per_g),
            pl.BlockSpec((n, S5_GROUP, p2), per_g),
            pl.BlockSpec((n, S5_GROUP, p2), per_g),
            pl.BlockSpec((n, S5_GROUP, q), per_g),
        ],
        out_specs=pl.BlockSpec((bsz, n, rows, q), lambda g: (0, g, 0, 0)),
        scratch_shapes=n * [
            pltpu.VMEM((S5_GROUP * S5_GROUP, q), F32),
            pltpu.VMEM((S5_GROUP * q, S5_GROUP * q + p2), BF16),
            pltpu.VMEM((S5_GROUP * q, p2), BF16),
        ],
        compiler_params=_params("parallel"),
        name="s5",
    )(u5.reshape(bsz, S5_GROUPS, rows, q), arow, ldt, bd, cd_re, cd_im, d_rows)
    return y4.reshape(u5.shape)


def _gelu_tanh(x):
    return 0.5 * x * (1.0 + jnp.tanh(0.7978845608028654 * (x + 0.044715 * (x * x * x))))


def _outproj_even_ffn_kernel(x_ref, ya_ref, ybt_ref, wglut_ref, bglu_ref, wa_ref, wb_ref,
                             fg_ref, wg_ref, wu_ref, wd_ref, o_ref, wg_s, wu_s, wd_s, act_scr):
    i = pl.program_id(0)
    _ffn_stage(i, wg_ref, wu_ref, wd_ref, wg_s, wu_s, wd_s)

    @pl.when(i >= wg_s.shape[0])
    def _():
        q = ybt_ref.shape[-1]
        ybt = jnp.concatenate([ybt_ref[0, :, cc].reshape(S5_WIDTH, q) for cc in range(ybt_ref.shape[2])],
                              axis=1)
        gate = _dot(wglut_ref[...], _gelu_tanh(ybt).astype(BF16)) + bglu_ref[...]
        yb = (ybt * jax.nn.sigmoid(gate)).T.astype(BF16)
        x2 = x_ref[...] + _dot(ya_ref[...], wa_ref[...]) + _dot(yb, wb_ref[...])
        o_ref[...] = _ffn_apply(x2, fg_ref, wg_s, wu_s, wd_s, act_scr)


def _outproj_even_ffn(x2d, ya, yb5, wglut, bglu, wa, wb, ffn, layer, bsz, seq):
    m, d = x2d.shape
    tm = min(FFN_TM, seq)
    per_b = seq // tm
    q = yb5.shape[-1]
    ns = _ffn_chunks(ffn)
    tile = lambda i: jnp.maximum(i - ns, 0)
    return pl.pallas_call(
        _outproj_even_ffn_kernel,
        out_shape=jax.ShapeDtypeStruct((m, d), F32),
        grid=(ns + m // tm,),
        in_specs=[
            pl.BlockSpec((tm, d), lambda i: (tile(i), 0)),
            pl.BlockSpec((tm, SSD_INNER), lambda i: (tile(i), 0)),
            pl.BlockSpec((1, S5_GROUPS, tm // q, S5_GROUP, q),
                         lambda i: (tile(i) // per_b, 0, tile(i) % per_b, 0, 0)),
            _resident(wglut.shape), _resident(bglu.shape), _resident(wa.shape), _resident(wb.shape),
        ] + _ffn_specs(*ffn, layer),
        out_specs=pl.BlockSpec((tm, d), lambda i: (tile(i), 0)),
        scratch_shapes=_ffn_scratch(ffn, tm),
        compiler_params=_params("arbitrary"),
        name="outproj_even_ffn",
    )(x2d, ya, yb5, wglut, bglu, wa, wb, *ffn)


def _head_rmsnorm(x, gain):
    assert LANES == 2 * ATT_HEAD_DIM
    first = lax.broadcasted_iota(jnp.int32, (1, LANES), 1) < ATT_HEAD_DIM
    tiles = []
    for p in range(x.shape[1] // LANES):
        xp = x[:, p * LANES:(p + 1) * LANES]
        sq = xp * xp
        s0 = jnp.sum(jnp.where(first, sq, 0.0), axis=-1, keepdims=True)
        s1 = jnp.sum(jnp.where(first, 0.0, sq), axis=-1, keepdims=True)
        ss = jnp.where(first, s0, s1)
        tiles.append(xp * lax.rsqrt(ss * (1.0 / ATT_HEAD_DIM) + EPS))
    return jnp.concatenate(tiles, axis=1) * gain


def _head_rmsnorm_t(xt, gain_col):
    c, n = xt.shape
    x3 = xt.reshape(c // ATT_HEAD_DIM, ATT_HEAD_DIM, n)
    ms = jnp.mean(x3 * x3, axis=1, keepdims=True)
    return (x3 * lax.rsqrt(ms + EPS)).reshape(c, n) * gain_col


def _pool_apply(u, prev, pos, w_ref, sc_ref):
    ext = jnp.concatenate([prev, u], axis=0)
    sums = {}
    s = ext
    w = 1
    while w < POOL_MAX:
        s = s + pltpu.roll(s, w, 0)
        w *= 2
        sums[w] = s[POOL_MAX:]
    outs = []
    for g, win in enumerate(POOL_WINDOWS):
        sl = slice(g * POOL_GROUP, (g + 1) * POOL_GROUP)
        count = jnp.minimum(pos + 1, win).astype(F32)
        pooled = sums[win][:, sl] / count - u[:, sl]
        outs.append(_dot(pooled.astype(BF16), w_ref[g]))
    return jnp.concatenate(outs, axis=1) * sc_ref[...]


def _ffn_inproj_odd_kernel(x_ref, fg_ref, wg_ref, wu_ref, wd_ref, g_ref, wqt_ref, wk_ref, wvt_ref,
                           wup_ref, qg_ref, kg_ref, pw_ref, ps_ref,
                           x1_ref, qt_ref, k_ref, vt_ref, yd_ref, wg_s, wu_s, wd_s, act_scr, carry_scr,
                           *, tiles_per_seq):
    i = pl.program_id(0)
    tm = x_ref.shape[0]
    n_stage = wg_s.shape[0]
    _ffn_stage(i, wg_ref, wu_ref, wd_ref, wg_s, wu_s, wd_s)

    @pl.when(i == 0)
    def _():
        carry_scr[...] = jnp.zeros_like(carry_scr)

    @pl.when(i >= n_stage)
    def _():
        x1 = _ffn_apply(x_ref[...], fg_ref, wg_s, wu_s, wd_s, act_scr)
        x1_ref[...] = x1
        h = _rmsnorm_bf16(x1, g_ref[...])
        nt = (((1,), (1,)), ((), ()))
        qt = lax.dot_general(wqt_ref[...], h, nt, preferred_element_type=F32)
        qt = _head_rmsnorm_t(qt, qg_ref[...]) * (ATT_HEAD_DIM ** -0.5 * LOG2E)
        qt_ref[0] = qt.astype(BF16)
        k_ref[...] = _head_rmsnorm(_dot(h, wk_ref[...]), kg_ref[...]).astype(BF16)
        vt_ref[0] = lax.dot_general(wvt_ref[...], h, nt, preferred_element_type=F32).astype(BF16)
        u = _dot(h, wup_ref[...])
        tile_in_seq = (i - n_stage) % tiles_per_seq
        prev = jnp.where(tile_in_seq == 0, 0.0, carry_scr[...])
        pos = tile_in_seq * tm + lax.broadcasted_iota(jnp.int32, (tm, 1), 0)
        yd_ref[...] = _pool_apply(u, prev, pos, pw_ref, ps_ref).astype(yd_ref.dtype)
        carry_scr[...] = u[tm - POOL_MAX:]


def _ffn_inproj_odd(x2d, ffn, gain, wqt, wk, wvt, wup, qg_col, kg, w_pool, pool_scale, layer, bsz, seq):
    m, d = x2d.shape
    tm = min(FFN_TM, seq)
    per_b = seq // tm
    w = ATT_WIDTH
    ns = _ffn_chunks(ffn)
    tl = lambda i: jnp.maximum(i - ns, 0)
    tile = pl.BlockSpec((tm, w), lambda i: (tl(i), 0))
    tile_t = pl.BlockSpec((1, w, tm), lambda i: (tl(i) // per_b, 0, tl(i) % per_b))
    row_tile = pl.BlockSpec((tm, d), lambda i: (tl(i), 0))
    tok = jax.ShapeDtypeStruct((m, w), BF16)
    chan = jax.ShapeDtypeStruct((bsz, w, seq), BF16)
    return pl.pallas_call(
        functools.partial(_ffn_inproj_odd_kernel, tiles_per_seq=per_b),
        out_shape=(jax.ShapeDtypeStruct((m, d), F32), chan, tok, chan, tok),
        grid=(ns + m // tm,),
        in_specs=[row_tile] + _ffn_specs(*ffn, layer) + [
            pl.BlockSpec((None, 1, d), lambda i: (layer, 0, 0)),
            _resident(wqt.shape), _resident(wk.shape), _resident(wvt.shape), _resident(wup.shape),
            _resident(qg_col.shape), _resident(kg.shape),
            _resident(w_pool.shape), _resident(pool_scale.shape),
        ],
        out_specs=(row_tile, tile_t, tile, tile_t, tile),
        scratch_shapes=_ffn_scratch(ffn, tm) + [pltpu.VMEM((POOL_MAX, w), F32)],
        compiler_params=_params("arbitrary"),
        name="ffn_inproj_odd",
    )(x2d, *ffn, gain, wqt, wk, wvt, wup, qg_col, kg, w_pool, pool_scale)


def _attn_kernel(qt_ref, k0_ref, k1_ref, k2_ref, vt0_ref, vt1_ref, vt2_ref, brow_ref, o_ref, bias_ref):
    t = pl.program_id(1)
    nseq, tq = o_ref.shape[:2]
    nk = 3 * tq

    @pl.when((pl.program_id(0) == 0) & (t == 0))
    def _():
        keyi = lax.broadcasted_iota(jnp.int32, (nk, tq), 0)
        qryi = lax.broadcasted_iota(jnp.int32, (nk, tq), 1)
        rel_chunk = keyi // ATT_CHUNK - (2 * tq // ATT_CHUNK - LEFT_CHUNKS) - qryi // ATT_CHUNK
        rel_chunk = jnp.where(rel_chunk >= 0, rel_chunk, LEFT_CHUNKS + 1)
        for hh in range(ATT_HEADS):
            base = jnp.broadcast_to(brow_ref[hh:hh + 1, :], (nk, 4 * tq))
            shifted = pltpu.roll(base, 0, 1, stride=1, stride_axis=0)
            bias = shifted[:, nk:] * LOG2E
            for var in range(3):
                first_key = (2 - var) * tq
                keep = jnp.where(keyi >= first_key, rel_chunk, LEFT_CHUNKS + 1) <= LEFT_CHUNKS
                bias_ref[var, hh] = jnp.where(keep, bias, -jnp.inf)

    var = jnp.minimum(t, 2)
    qt = [qt_ref[s] for s in range(nseq)]
    kcat = [jnp.concatenate([k0_ref[s], k1_ref[s], k2_ref[s]], axis=0) for s in range(nseq)]
    vtcat = [jnp.concatenate([vt0_ref[s], vt1_ref[s], vt2_ref[s]], axis=1) for s in range(nseq)]
    hd = ATT_HEAD_DIM
    heads_per_slab = LANES // hd
    no_q = jnp.zeros((LANES - hd, tq), qt[0].dtype)
    ones_rows = jnp.ones((16, nk), qt[0].dtype)
    n_tiles = nk // tq

    def scores(s, head, j):
        p, hh = divmod(head, heads_per_slab)
        rows = slice(head * hd, (head + 1) * hd)
        qth = jnp.concatenate([qt[s][rows]] + [no_q] if hh == 0 else [no_q] + [qt[s][rows]], axis=0)
        ks = slice(j * tq, (j + 1) * tq)
        return _dot(kcat[s][ks, p * LANES:(p + 1) * LANES], qth) + bias_ref[var, head, ks, :]

    items = [(s, head, j) for s in range(nseq) for head in range(ATT_HEADS) for j in range(n_tiles)]
    ahead = [scores(*it) for it in items[:ATT_LOOKAHEAD]]
    outs, parts, tops = [], [], []
    for idx, (s, head, j) in enumerate(items):
        st = ahead.pop(0)
        if idx + ATT_LOOKAHEAD < len(items):
            ahead.append(scores(*items[idx + ATT_LOOKAHEAD]))
        rows = slice(head * hd, (head + 1) * hd)
        ks = slice(j * tq, (j + 1) * tq)
        top = jnp.maximum(jnp.max(st, axis=0, keepdims=True), MASKED_FLOOR)
        e = jnp.exp2(st - top).astype(BF16)
        vth = jnp.concatenate([vtcat[s][rows, ks], ones_rows[:, ks]], axis=0)
        parts.append(_dot(vth, e))
        tops.append(top)
        if j == n_tiles - 1:
            top_all = functools.reduce(jnp.maximum, tops)
            ot = sum(part * jnp.exp2(tp - top_all) for part, tp in zip(parts, tops))
            outs.append(ot[:hd] / ot[hd:hd + 1])
            parts, tops = [], []
            if head % heads_per_slab == heads_per_slab - 1:
                p = head // heads_per_slab
                o_ref[s, :, p * LANES:(p + 1) * LANES] = (
                    jnp.concatenate(outs, axis=0).T.astype(o_ref.dtype))
                outs = []


def _attention(qt, k, vt, brow, bsz, seq):
    m, w = k.shape
    tq = ATT_T
    assert 2 * tq >= LEFT_CHUNKS * ATT_CHUNK and tq > MAX_REL and seq % tq == 0
    nt = seq // tq
    nb = math.gcd(bsz, ATT_PER_STEP)
    back = lambda n: (lambda b, t: (b, jnp.maximum(t - n, 0), 0))
    back_t = lambda n: (lambda b, t: (b, 0, jnp.maximum(t - n, 0)))
    tok = lambda f: pl.BlockSpec((nb, tq, w), f)
    chan = lambda f: pl.BlockSpec((nb, w, tq), f)
    k3 = k.reshape(bsz, seq, w)
    out = pl.pallas_call(
        _attn_kernel,
        out_shape=jax.ShapeDtypeStruct((bsz, seq, w), BF16),
        grid=(bsz // nb, nt),
        in_specs=[chan(back_t(0)), tok(back(2)), tok(back(1)), tok(back(0)),
                  chan(back_t(2)), chan(back_t(1)), chan(back_t(0)),
                  pl.BlockSpec(brow.shape, lambda b, t: (0, 0))],
        out_specs=tok(back(0)),
        scratch_shapes=[pltpu.VMEM((3, ATT_HEADS, 3 * tq, tq), F32)],
        compiler_params=_params("arbitrary", "arbitrary"),
        name="band_attention",
    )(qt, k3, k3, k3, vt, vt, vt, brow)
    return out.reshape(m, w)


def _bias_rows(rel_bias, tq):
    rb = rel_bias.astype(F32)
    nh = rb.shape[0]
    near = jnp.broadcast_to(rb[:, :1], (nh, tq - MAX_REL))
    far = jnp.broadcast_to(rb[:, -1:], (nh, 3 * tq - MAX_REL - 1))
    return jnp.concatenate([near, rb, far], axis=1)


def _outproj_odd_ffn_kernel(x_ref, yc_ref, yd_ref, wa_ref, wb_ref, fg_ref, wg_ref, wu_ref, wd_ref,
                            o_ref, wg_s, wu_s, wd_s, act_scr):
    i = pl.program_id(0)
    _ffn_stage(i, wg_ref, wu_ref, wd_ref, wg_s, wu_s, wd_s)

    @pl.when(i >= wg_s.shape[0])
    def _():
        x2 = x_ref[...] + _dot(yc_ref[...], wa_ref[...]) + _dot(yd_ref[...], wb_ref[...])
        o_ref[...] = _ffn_apply(x2, fg_ref, wg_s, wu_s, wd_s, act_scr)


def _outproj_odd_ffn(x2d, yc, yd, wa, wb, ffn, layer):
    m, d = x2d.shape
    tm = min(OUT_TM, m)
    ns = _ffn_chunks(ffn)
    tile = lambda i: jnp.maximum(i - ns, 0)
    return pl.pallas_call(
        _outproj_odd_ffn_kernel,
        out_shape=jax.ShapeDtypeStruct((m, d), F32),
        grid=(ns + m // tm,),
        in_specs=[
            pl.BlockSpec((tm, d), lambda i: (tile(i), 0)),
            pl.BlockSpec((tm, ATT_WIDTH), lambda i: (tile(i), 0)),
            pl.BlockSpec((tm, ATT_WIDTH), lambda i: (tile(i), 0)),
            _resident(wa.shape), _resident(wb.shape),
        ] + _ffn_specs(*ffn, layer),
        out_specs=pl.BlockSpec((tm, d), lambda i: (tile(i), 0)),
        scratch_shapes=_ffn_scratch(ffn, tm),
        compiler_params=_params("arbitrary"),
        name="outproj_odd_ffn",
    )(x2d, yc, yd, wa, wb, *ffn)


def _even_layer(x2d, ffn1, ffn2, gain, layer, bsz, seq, even_w_in, even_w_out, ssd_conv_w, ssd_conv_b,
                ssd_dt_bias, ssd_a_log, ssd_d, ssd_norm, s5_a_re, s5_a_im, s5_log_dt, s5_b_re, s5_b_im,
                s5_c_re, s5_c_im, s5_d, s5_w_glu, s5_b_glu):
    i = layer // 2
    w_in = even_w_in[i]
    o1 = SSD_INNER
    o2 = o1 + SSD_XBC
    o3 = o2 + SSD_HEADS
    wz = w_in[:, :o1].astype(BF16)
    wx = w_in[:, o1:o2].astype(BF16)
    wut = jnp.concatenate([w_in[:, o3:].T, w_in[:, o2:o3].T], axis=0)
    wut = jnp.pad(wut, ((0, -wut.shape[0] % 16), (0, 0))).astype(BF16)
    x1, z, xbc, dtt, ut = _ffn_inproj_even(x2d, ffn1, gain, wz, wx, wut, layer, bsz, seq)

    col = lambda v: v.astype(F32)[:, None]
    expand = (jnp.arange(LANES)[:, None] == (jnp.arange(SSD_INNER)[None, :] // SSD_HEAD_DIM)).astype(BF16)
    ya = _ssd(z, xbc, dtt, ssd_conv_w[i], ssd_conv_b[i][None, :], col(ssd_dt_bias[i]),
              col(ssd_a_log[i]), jnp.repeat(ssd_d[i], SSD_HEAD_DIM)[None, :], ssd_norm[i][None, :],
              expand, bsz, seq)

    q = min(S5_Q, seq)
    are, aim = s5_a_re[i], s5_a_im[i]
    dup = lambda v: jnp.concatenate([v, v], axis=-1)
    arow = jnp.stack([dup(are), dup(aim)], axis=1)
    ldt = s5_log_dt[i][:, None, None]
    bd = jnp.concatenate([jnp.swapaxes(s5_b_re[i], 1, 2), jnp.swapaxes(s5_b_im[i], 1, 2)], axis=-1)
    cd_re, cd_im = dup(s5_c_re[i]), dup(s5_c_im[i])
    d_rows = jnp.broadcast_to(s5_d[i][:, :, None], (S5_GROUPS, S5_GROUP, q))
    yb5 = _s5(ut, arow, ldt, bd, cd_re, cd_im, d_rows)

    w_out = even_w_out[i]
    return _outproj_even_ffn(x1, ya, yb5, s5_w_glu[i].T.astype(BF16), s5_b_glu[i][:, None],
                             w_out[:SSD_INNER].astype(BF16), w_out[SSD_INNER:].astype(BF16),
                             ffn2, layer, bsz, seq)


def _odd_layer(x2d, ffn1, ffn2, gain, layer, bsz, seq, odd_w_in, odd_w_out, attn_q_norm, attn_k_norm,
               attn_rel_bias, pool_w, pool_scale):
    i = layer // 2
    w_in = odd_w_in[i].astype(BF16)
    w = ATT_WIDTH
    tile_gain = lambda v: jnp.tile(v.astype(F32), ATT_HEADS)[None, :]
    x1, qt, k, vt, yd = _ffn_inproj_odd(
        x2d, ffn1, gain, w_in[:, :w].T, w_in[:, w:2 * w], w_in[:, 2 * w:3 * w].T, w_in[:, 3 * w:],
        tile_gain(attn_q_norm[i]).T, tile_gain(attn_k_norm[i]),
        pool_w[i].astype(BF16), pool_scale[i][None, :], layer, bsz, seq)
    yc = _attention(qt, k, vt, _bias_rows(attn_rel_bias[i], ATT_T), bsz, seq)
    w_out = odd_w_out[i]
    return _outproj_odd_ffn(x1, yc, yd, w_out[:w].astype(BF16), w_out[w:].astype(BF16), ffn2, layer)


def kernel(x, ffn1_norm, ffn1_w_gate, ffn1_w_up, ffn1_w_down, mix_norm, even_w_in, even_w_out, ssd_conv_w, ssd_conv_b, ssd_dt_bias, ssd_a_log, ssd_d, ssd_norm, s5_a_re, s5_a_im, s5_log_dt, s5_b_re, s5_b_im, s5_c_re, s5_c_im, s5_d, s5_w_glu, s5_b_glu, odd_w_in, odd_w_out, attn_q_norm, attn_k_norm, attn_rel_bias, pool_w, pool_scale, ffn2_norm, ffn2_w_gate, ffn2_w_up, ffn2_w_down):
    bsz, seq, d = x.shape
    depth = ffn1_norm.shape[0]
    x2d = x.reshape(bsz * seq, d)
    g3 = lambda v: v.astype(F32)[:, None, :]
    f1n, mxn, f2n = g3(ffn1_norm), g3(mix_norm), g3(ffn2_norm)
    as32 = lambda v: v.astype(F32)
    ffn1 = (f1n, as32(ffn1_w_gate), as32(ffn1_w_up), as32(ffn1_w_down))
    ffn2 = (f2n, as32(ffn2_w_gate), as32(ffn2_w_up), as32(ffn2_w_down))
    for layer in range(depth):
        if layer % 2 == 0:
            x2d = _even_layer(x2d, ffn1, ffn2, mxn, layer, bsz, seq, even_w_in, even_w_out, ssd_conv_w,
                              ssd_conv_b, ssd_dt_bias, ssd_a_log, ssd_d, ssd_norm, s5_a_re, s5_a_im,
                              s5_log_dt, s5_b_re, s5_b_im, s5_c_re, s5_c_im, s5_d, s5_w_glu, s5_b_glu)
        else:
            x2d = _odd_layer(x2d, ffn1, ffn2, mxn, layer, bsz, seq, odd_w_in, odd_w_out, attn_q_norm,
                             attn_k_norm, attn_rel_bias, pool_w, pool_scale)
    return x2d.reshape(bsz, seq, d)
```

```python
import functools
import math

import jax
import jax.numpy as jnp
from jax import lax
from jax.experimental import pallas as pl
from jax.experimental.pallas import tpu as pltpu

F32 = jnp.float32
BF16 = jnp.bfloat16
HIGHEST = lax.Precision.HIGHEST

EPS = 1e-6
LOG2E = 1.4426950408889634
MASKED_FLOOR = -1e30
LANES = 128
VMEM_LIMIT = 56 * 1024 * 1024

SSD_HEADS = 8
SSD_HEAD_DIM = 64
SSD_INNER = 512
SSD_GROUPS = 2
SSD_STATE = 128
SSD_CONV = 4
SSD_XBC = 1024
S5_WIDTH = 512
S5_GROUP = 16
S5_GROUPS = 32
S5_STATE = 64
ATT_HEADS = 8
ATT_HEAD_DIM = 64
ATT_WIDTH = 512
ATT_CHUNK = 64
LEFT_CHUNKS = 8
MAX_REL = 128
POOL_WINDOWS = (2, 4, 8, 16)
POOL_GROUP = 128
POOL_MAX = 16

FFN_TM = 512
OUT_TM = 1024
FFN_TF = 256
SSD_Q = 128
SSD_PER_STEP = 4
S5_Q = 128
S5_PER_STEP = 2
ATT_T = 256
ATT_PER_STEP = 4
ATT_LOOKAHEAD = 5


def _params(*sem):
    return pltpu.CompilerParams(dimension_semantics=sem, vmem_limit_bytes=VMEM_LIMIT)


def _dot(a, b):
    return jnp.dot(a, b, preferred_element_type=F32)


def _dot_exact(a, b):
    return jnp.dot(a, b, preferred_element_type=F32, precision=HIGHEST)


def _split3(x):
    hi = x.astype(BF16)
    rest = x - hi.astype(F32)
    mid = rest.astype(BF16)
    lo = (rest - mid.astype(F32)).astype(BF16)
    return hi, mid, lo


def _select_right(x, sel):
    hi, mid, lo = _split3(x)
    return _dot(hi, sel) + _dot(mid, sel) + _dot(lo, sel)


def _rmsnorm_bf16(x, gain):
    ms = jnp.mean(x * x, axis=-1, keepdims=True)
    return (x * lax.rsqrt(ms + EPS) * gain).astype(BF16)


def _silu(x):
    return x * jax.nn.sigmoid(x)


def _ffn_stage(i, wg_ref, wu_ref, wd_ref, wg_s, wu_s, wd_s):
    @pl.when(i < wg_s.shape[0])
    def _():
        wg_s[i] = wg_ref[...].astype(BF16)
        wu_s[i] = wu_ref[...].astype(BF16)
        wd_s[pl.ds(pl.multiple_of(i * FFN_TF, FFN_TF), FFN_TF), :] = wd_ref[...].astype(BF16)


def _ffn_apply(x, g_ref, wg_s, wu_s, wd_s, act_scr):
    h = _rmsnorm_bf16(x, g_ref[...])
    for c in range(wg_s.shape[0]):
        gate = _dot(h, wg_s[c])
        up = _dot(h, wu_s[c])
        act_scr[:, c * FFN_TF:(c + 1) * FFN_TF] = (_silu(gate) * up).astype(BF16)
    return x + 0.5 * _dot(act_scr[...], wd_s[...])


def _resident(shape):
    return pl.BlockSpec(shape, lambda i: (0,) * len(shape), pipeline_mode=pl.Buffered(1))


def _ffn_chunks(ffn):
    dff = ffn[1].shape[-1]
    assert dff % FFN_TF == 0
    return dff // FFN_TF


def _ffn_specs(gain, wg, wu, wd, layer):
    d, dff = wg.shape[1:]
    last = dff // FFN_TF - 1
    return [
        pl.BlockSpec((None, 1, d), lambda i: (layer, 0, 0)),
        pl.BlockSpec((None, d, FFN_TF), lambda i: (layer, 0, jnp.minimum(i, last))),
        pl.BlockSpec((None, d, FFN_TF), lambda i: (layer, 0, jnp.minimum(i, last))),
        pl.BlockSpec((None, FFN_TF, d), lambda i: (layer, jnp.minimum(i, last), 0)),
    ]


def _ffn_scratch(ffn, tm):
    d, dff = ffn[1].shape[1:]
    n = dff // FFN_TF
    return [pltpu.VMEM((n, d, FFN_TF), BF16), pltpu.VMEM((n, d, FFN_TF), BF16),
            pltpu.VMEM((dff, d), BF16), pltpu.VMEM((tm, dff), BF16)]


def _ffn_inproj_even_kernel(x_ref, fg_ref, wg_ref, wu_ref, wd_ref, g_ref, wz_ref, wx_ref,
                            wut_ref, x1_ref, z_ref, xbc_ref, dtt_ref, ut_ref, wg_s, wu_s, wd_s, act_scr):
    i = pl.program_id(0)
    _ffn_stage(i, wg_ref, wu_ref, wd_ref, wg_s, wu_s, wd_s)

    @pl.when(i >= wg_s.shape[0])
    def _():
        x1 = _ffn_apply(x_ref[...], fg_ref, wg_s, wu_s, wd_s, act_scr)
        x1_ref[...] = x1
        h = _rmsnorm_bf16(x1, g_ref[...])
        z_ref[...] = _dot(h, wz_ref[...])
        xbc_ref[...] = _dot(h, wx_ref[...])
        ut = lax.dot_general(wut_ref[...], h, (((1,), (1,)), ((), ())), preferred_element_type=F32)
        dtt_ref[0] = ut[S5_WIDTH:S5_WIDTH + SSD_HEADS, :]
        q = ut_ref.shape[-1]
        for cc in range(ut_ref.shape[2]):
            ut_ref[0, :, cc] = ut[:S5_WIDTH, cc * q:(cc + 1) * q].reshape(S5_GROUPS, S5_GROUP, q)


def _ffn_inproj_even(x2d, ffn, gain, wz, wx, wut, layer, bsz, seq):
    m, d = x2d.shape
    tm = min(FFN_TM, seq)
    per_b = seq // tm
    q = min(S5_Q, seq)
    ns = _ffn_chunks(ffn)
    tile = lambda i: jnp.maximum(i - ns, 0)
    return pl.pallas_call(
        _ffn_inproj_even_kernel,
        out_shape=(
            jax.ShapeDtypeStruct((m, d), F32),
            jax.ShapeDtypeStruct((m, SSD_INNER), F32),
            jax.ShapeDtypeStruct((m, SSD_XBC), F32),
            jax.ShapeDtypeStruct((bsz, SSD_HEADS, seq), F32),
            jax.ShapeDtypeStruct((bsz, S5_GROUPS, seq // q, S5_GROUP, q), F32),
        ),
        grid=(ns + m // tm,),
        in_specs=[pl.BlockSpec((tm, d), lambda i: (tile(i), 0))] + _ffn_specs(*ffn, layer) + [
            pl.BlockSpec((None, 1, d), lambda i: (layer, 0, 0)),
            _resident(wz.shape), _resident(wx.shape), _resident(wut.shape),
        ],
        out_specs=(
            pl.BlockSpec((tm, d), lambda i: (tile(i), 0)),
            pl.BlockSpec((tm, SSD_INNER), lambda i: (tile(i), 0)),
            pl.BlockSpec((tm, SSD_XBC), lambda i: (tile(i), 0)),
            pl.BlockSpec((1, SSD_HEADS, tm), lambda i: (tile(i) // per_b, 0, tile(i) % per_b)),
            pl.BlockSpec((1, S5_GROUPS, tm // q, S5_GROUP, q),
                         lambda i: (tile(i) // per_b, 0, tile(i) % per_b, 0, 0)),
        ),
        scratch_shapes=_ffn_scratch(ffn, tm),
        compiler_params=_params("arbitrary"),
        name="ffn_inproj_even",
    )(x2d, *ffn, gain, wz, wx, wut)


def _shift_rows(cur, prev8, j):
    rolled = pltpu.roll(cur, j, 0)
    prev_rolled = pltpu.roll(prev8, j, 0)
    row = lax.broadcasted_iota(jnp.int32, prev8.shape, 0)
    top = jnp.where(row < j, prev_rolled, rolled[:8])
    return jnp.concatenate([top, rolled[8:]], axis=0)


def _ssd_kernel(z_ref, xbc_ref, xprev_ref, dt_ref, *rest):
    consts, o_ref, h_scr = rest[:-2], rest[-2], rest[-1]

    @pl.when(pl.program_id(1) == 0)
    def _():
        h_scr[...] = jnp.zeros_like(h_scr)

    for bb in range(z_ref.shape[0]):
        _ssd_chunk(z_ref.at[bb], xbc_ref.at[bb], xprev_ref.at[bb], dt_ref.at[bb], *consts,
                   o_ref.at[bb], h_scr.at[bb])


def _ssd_chunk(z_ref, xbc_ref, xprev_ref, dt_ref, cw_ref, cb_ref, dtb_ref, alog_ref,
               dx_ref, ng_ref, e_ref, o_ref, h_scr):
    t = pl.program_id(1)
    q = z_ref.shape[0]
    gw = SSD_INNER // SSD_GROUPS
    hpg = SSD_HEADS // SSD_GROUPS

    xbc = xbc_ref[...]
    prev = jnp.where(t > 0, xprev_ref[...], 0.0)
    conv = xbc * cw_ref[SSD_CONV - 1:SSD_CONV, :] + cb_ref[...]
    for j in range(1, SSD_CONV):
        conv = conv + _shift_rows(xbc, prev, j) * cw_ref[SSD_CONV - 1 - j:SSD_CONV - j, :]
    xc = _silu(conv)
    xs = xc[:, :SSD_INNER]

    dtr = dt_ref[...] + dtb_ref[...]
    dt_t = jnp.maximum(dtr, 0.0) + jnp.log1p(jnp.exp(-jnp.abs(dtr)))
    a = -jnp.exp(alog_ref[...])
    row = lax.broadcasted_iota(jnp.int32, (q, q), 0)
    col = lax.broadcasted_iota(jnp.int32, (q, q), 1)
    causal = row >= col
    triu = jnp.where(row <= col, 1.0, 0.0).astype(BF16)
    acs_t = _select_right(dt_t * a, triu)
    pad = jnp.zeros((LANES - SSD_HEADS, q), F32)
    acs = jnp.concatenate([acs_t, pad], axis=0).T
    dt = jnp.concatenate([dt_t, pad], axis=0).T
    expanded = _select_right(jnp.concatenate([acs, dt], axis=0), e_ref[...])
    acs_x = expanded[:q]
    dt_x = expanded[q:]
    xdt = xs * dt_x
    decay_in = jnp.exp(acs_x)
    acs_end = acs_x[q - 1:q, :]
    to_end = jnp.exp(acs_end - acs_x)
    chunk_decay = jnp.exp(acs_end)
    xdt_end = xdt * to_end

    lane_head = lax.broadcasted_iota(jnp.int32, (1, gw), 1) // SSD_HEAD_DIM
    ys = []
    for g in range(SSD_GROUPS):
        bm = xc[:, SSD_INNER + g * SSD_STATE:SSD_INNER + (g + 1) * SSD_STATE].astype(BF16)
        cm = xc[:, SSD_INNER + (SSD_GROUPS + g) * SSD_STATE:
                SSD_INNER + (SSD_GROUPS + g + 1) * SSD_STATE].astype(BF16)
        cb = lax.dot_general(cm, bm, (((1,), (1,)), ((), ())), preferred_element_type=F32)
        xdt_g = xdt[:, g * gw:(g + 1) * gw]
        gmats, rhs = [], []
        for j in range(hpg):
            hh = g * hpg + j
            seg = acs[:, hh:hh + 1] - acs_t[hh:hh + 1, :]
            decay = jnp.exp(jnp.where(causal, seg, -jnp.inf))
            gmats.append((cb * decay).astype(BF16))
            rhs.append(jnp.where(lane_head == j, xdt_g, 0.0).astype(BF16))
        y_diag = _dot(jnp.concatenate(gmats, axis=1), jnp.concatenate(rhs, axis=0))
        h_prev = h_scr[g]
        y_off = _dot(cm, h_prev.astype(BF16)) * decay_in[:, g * gw:(g + 1) * gw]
        upd = lax.dot_general(bm, xdt_end[:, g * gw:(g + 1) * gw].astype(BF16),
                              (((0,), (0,)), ((), ())), preferred_element_type=F32)
        h_scr[g] = chunk_decay[:, g * gw:(g + 1) * gw] * h_prev + upd
        ys.append(y_diag + y_off)
    y = jnp.concatenate(ys, axis=1) + dx_ref[...] * xs
    y = y * _silu(z_ref[...])
    outs = []
    for g in range(SSD_GROUPS):
        yg = y[:, g * gw:(g + 1) * gw]
        outs.append(yg * lax.rsqrt(jnp.mean(yg * yg, axis=-1, keepdims=True) + EPS))
    o_ref[...] = (jnp.concatenate(outs, axis=1) * ng_ref[...]).astype(o_ref.dtype)


def _ssd(z, xbc, dt, conv_w, conv_b, dt_bias, a_log, d_x, norm_gain, expand, bsz, seq):
    m = z.shape[0]
    q = min(SSD_Q, seq)
    nq = seq // q
    nb = math.gcd(bsz, SSD_PER_STEP)
    const = lambda b, t: (0, 0)
    per_seq = lambda v: v.reshape(bsz, seq, v.shape[-1])
    out = pl.pallas_call(
        _ssd_kernel,
        out_shape=jax.ShapeDtypeStruct((bsz, seq, SSD_INNER), BF16),
        grid=(bsz // nb, nq),
        in_specs=[
            pl.BlockSpec((nb, q, SSD_INNER), lambda b, t: (b, t, 0)),
            pl.BlockSpec((nb, q, SSD_XBC), lambda b, t: (b, t, 0)),
            pl.BlockSpec((nb, 8, SSD_XBC), lambda b, t: (b, jnp.maximum(t * (q // 8) - 1, 0), 0)),
            pl.BlockSpec((nb, SSD_HEADS, q), lambda b, t: (b, 0, t)),
            pl.BlockSpec(conv_w.shape, const),
            pl.BlockSpec(conv_b.shape, const),
            pl.BlockSpec(dt_bias.shape, const),
            pl.BlockSpec(a_log.shape, const),
            pl.BlockSpec(d_x.shape, const),
            pl.BlockSpec(norm_gain.shape, const),
            pl.BlockSpec(expand.shape, const),
        ],
        out_specs=pl.BlockSpec((nb, q, SSD_INNER), lambda b, t: (b, t, 0)),
        scratch_shapes=[pltpu.VMEM((nb, SSD_GROUPS, SSD_STATE, SSD_INNER // SSD_GROUPS), F32)],
        compiler_params=_params("parallel", "arbitrary"),
        name="ssd",
    )(per_seq(z), per_seq(xbc), per_seq(xbc), dt, conv_w, conv_b, dt_bias, a_log, d_x,
      norm_gain, expand)
    return out.reshape(m, SSD_INNER)


def _cpow(base_re, base_im, exponent, nbits):
    shape = jnp.broadcast_shapes(base_re.shape, exponent.shape)
    pr = jnp.ones(shape, F32)
    pi = jnp.zeros(shape, F32)
    sr, si = base_re, base_im
    for k in range(nbits):
        bit = ((exponent >> k) & 1) == 1
        nr = pr * sr - pi * si
        ni = pr * si + pi * sr
        pr = jnp.where(bit, nr, pr)
        pi = jnp.where(bit, ni, pi)
        if k + 1 < nbits:
            sr, si = sr * sr - si * si, 2.0 * sr * si
    return pr, pi


def _s5_discretize(ar, ai, log_dt):
    dt = jnp.exp(log_dt)
    mag = jnp.exp(dt * ar)
    lam_re = mag * jnp.cos(dt * ai)
    lam_im = mag * jnp.sin(dt * ai)
    den = ar * ar + ai * ai
    f_re = ((lam_re - 1.0) * ar + lam_im * ai) / den
    f_im = (lam_im * ar - (lam_re - 1.0) * ai) / den
    return lam_re, lam_im, f_re, f_im


def _s5_kernel(u_ref, arow_ref, ldt_ref, bd_ref, cd_re_ref, cd_im_ref, d_ref, y_ref, *scratch):
    n = u_ref.shape[1]
    scr = [scratch[3 * gg:3 * gg + 3] for gg in range(n)]
    lams = [_s5_prepare(arow_ref.at[gg], ldt_ref.at[gg], bd_ref.at[gg], cd_re_ref.at[gg],
                        cd_im_ref.at[gg], *scr[gg], u_ref.shape[-1]) for gg in range(n)]
    for gg in range(n):
        _s5_apply(lams[gg], u_ref.at[:, gg], d_ref.at[gg], y_ref.at[:, gg], *scr[gg][1:])


def _s5_prepare(arow_ref, ldt_ref, bd_ref, cd_re_ref, cd_im_ref, kmat_scr, t_scr, f_scr, q):
    gsz = S5_GROUP
    p2 = 2 * S5_STATE
    nbits = (q - 1).bit_length()
    log_dt = ldt_ref[...]

    lane = lax.broadcasted_iota(jnp.int32, (1, p2), 1)
    first_half_l = lane < S5_STATE
    rowp = lax.broadcasted_iota(jnp.int32, (p2, 1), 0)
    first_half_r = rowp < S5_STATE

    ar_r = arow_ref[0:1, :]
    ai_r = arow_ref[1:2, :]
    lr, li, fr, fi = _s5_discretize(ar_r, ai_r, log_dt)
    sign_b = jnp.where(first_half_l, -1.0, 1.0)
    bd1 = bd_ref[...]
    bd2 = pltpu.roll(bd1, S5_STATE, 1) * sign_b
    bb_a = fr * bd1 + fi * bd2
    bb_b = fr * bd2 - fi * bd1

    srow = lax.broadcasted_iota(jnp.int32, (q, 1), 0)
    pr_re, pr_im = _cpow(lr, li, (q - 1) - srow, nbits)
    for i in range(gsz):
        es = pr_re * bb_a[i:i + 1, :] + pr_im * bb_b[i:i + 1, :]
        t_scr[i * q:(i + 1) * q, gsz * q:gsz * q + p2] = es.astype(BF16)

    pt_re, pt_im = _cpow(lr, li, srow, nbits)
    p1_re = pt_re * lr - pt_im * li
    p1_im = pt_re * li + pt_im * lr
    cre = cd_re_ref[...]
    cim = cd_im_ref[...]
    for o in range(gsz):
        c_r = cre[o:o + 1, :]
        c_i = cim[o:o + 1, :]
        f_o = jnp.where(first_half_l, c_r * p1_re - c_i * p1_im, -(c_r * p1_im + c_i * p1_re))
        f_scr[o * q:(o + 1) * q, :] = f_o.astype(BF16)

    eye = (lax.broadcasted_iota(jnp.int32, (p2, p2), 0)
           == lax.broadcasted_iota(jnp.int32, (p2, p2), 1))
    lcr = jnp.sum(jnp.where(eye, lr, 0.0), axis=1, keepdims=True)
    lci = jnp.sum(jnp.where(eye, li, 0.0), axis=1, keepdims=True)
    dlane = lax.broadcasted_iota(jnp.int32, (1, q), 1)
    pw_re, pw_im = _cpow(lcr, lci, dlane, nbits)
    pow_stack = jnp.where(first_half_r, pw_re, pw_im)

    v1 = bb_a * jnp.where(first_half_l, 1.0, -1.0)
    v2 = -pltpu.roll(bb_a, S5_STATE, 1)
    coef =cre[:, None, :] * v1[None, :, :] + cim[:, None, :] * v2[None, :, :]
    kmat_scr[...] = _dot_exact(coef.reshape(gsz * gsz, p2), pow_stack)

    trow = lax.broadcasted_iota(jnp.int32, (q, q), 0)
    tcol = lax.broadcasted_iota(jnp.int32, (q, q), 1)
    lower = tcol >= trow
    for i in range(gsz):
        for o in range(gsz):
            k_row = kmat_scr[o * gsz + i:o * gsz + i + 1, :]
            blk = pltpu.roll(jnp.broadcast_to(k_row, (q, q)), 0, 1, stride=1, stride_axis=0)
            t_scr[i * q:(i + 1) * q, o * q:(o + 1) * q] = jnp.where(lower, blk, 0.0).astype(BF16)
    return lr, li


def _s5_apply(lam, u_ref, d_ref, y_ref, t_scr, f_scr):
    lr, li = lam
    bsz, rows, q = u_ref.shape
    gsz = S5_GROUP
    nc = rows // gsz
    m = bsz * nc
    sign_b = jnp.where(lax.broadcasted_iota(jnp.int32, (1, 2 * S5_STATE), 1) < S5_STATE, -1.0, 1.0)
    u_f32 = [jnp.concatenate([u_ref[b, pl.ds(i, nc, stride=gsz), :] for b in range(bsz)], axis=0)
             for i in range(gsz)]
    u_cat = jnp.concatenate([v.astype(BF16) for v in u_f32], axis=1)
    acc = _dot(u_cat, t_scr[...])
    x = acc[:, gsz * q:]

    sq_re, sq_im = lr, li
    for _ in range(q.bit_length() - 1):
        sq_re, sq_im = sq_re * sq_re - sq_im * sq_im, 2.0 * sq_re * sq_im
    crow = lax.broadcasted_iota(jnp.int32, (m, 1), 0) % nc
    k = 1
    while k < nc:
        sh = jnp.where(crow >= k, pltpu.roll(x, k, 0), 0.0)
        x = x + sq_re * sh + (sq_im * sign_b) * pltpu.roll(sh, S5_STATE, 1)
        sq_re, sq_im = sq_re * sq_re - sq_im * sq_im, 2.0 * sq_re * sq_im
        k *= 2
    h_prev = jnp.where(crow >= 1, pltpu.roll(x, 1, 0), 0.0)
    y = acc[:, :gsz * q] + lax.dot_general(h_prev.astype(BF16), f_scr[...], (((1,), (1,)), ((), ())),
                                           preferred_element_type=F32)
    for o in range(gsz):
        y_o = y[:, o * q:(o + 1) * q] + d_ref[o:o + 1, :] * u_f32[o]
        for b in range(bsz):
            y_ref[b, pl.ds(o, nc, stride=gsz), :] = y_o[b * nc:(b + 1) * nc]


def _s5(u5, arow, ldt, bd, cd_re, cd_im, d_rows):
    bsz, _, nc, _, q = u5.shape
    rows = nc * S5_GROUP
    p2 = 2 * S5_STATE
    per_g = lambda g: (g, 0, 0)
    n = S5_PER_STEP
    assert S5_GROUPS % n == 0
    y4 = pl.pallas_call(
        _s5_kernel,
        out_shape=jax.ShapeDtypeStruct((bsz, S5_GROUPS, rows, q), F32),
        grid=(S5_GROUPS // n,),
        in_specs=[
            pl.BlockSpec((bsz, n, rows, q), lambda g: (0, g, 0, 0)),
            pl.BlockSpec((n, 2, p2), per_g),
            pl.BlockSpec((n, 1, 1), per_g),
            pl.BlockSpec((n, S5_GROUP, p2), per_g),
            pl.BlockSpec((n, S5_GROUP, p2), per_g),
            pl.BlockSpec((n, S5_GROUP, p2), per_g),
            pl.BlockSpec((n, S5_GROUP, q), per_g),
        ],
        out_specs=pl.BlockSpec((bsz, n, rows, q), lambda g: (0, g, 0, 0)),
        scratch_shapes=n * [
            pltpu.VMEM((S5_GROUP * S5_GROUP, q), F32),
            pltpu.VMEM((S5_GROUP * q, S5_GROUP * q + p2), BF16),
            pltpu.VMEM((S5_GROUP * q, p2), BF16),
        ],
        compiler_params=_params("parallel"),
        name="s5",
    )(u5.reshape(bsz, S5_GROUPS, rows, q), arow, ldt, bd, cd_re, cd_im, d_rows)
    return y4.reshape(u5.shape)


def _gelu_tanh(x):
    return 0.5 * x * (1.0 + jnp.tanh(0.7978845608028654 * (x + 0.044715 * (x * x * x))))


def _outproj_even_ffn_kernel(x_ref, ya_ref, ybt_ref, wglut_ref, bglu_ref, wa_ref, wb_ref,
                             fg_ref, wg_ref, wu_ref, wd_ref, o_ref, wg_s, wu_s, wd_s, act_scr):
    i = pl.program_id(0)
    _ffn_stage(i, wg_ref, wu_ref, wd_ref, wg_s, wu_s, wd_s)

    @pl.when(i >= wg_s.shape[0])
    def _():
        q = ybt_ref.shape[-1]
        ybt = jnp.concatenate([ybt_ref[0, :, cc].reshape(S5_WIDTH, q) for cc in range(ybt_ref.shape[2])],
                              axis=1)
        gate = _dot(wglut_ref[...], _gelu_tanh(ybt).astype(BF16)) + bglu_ref[...]
        yb = (ybt * jax.nn.sigmoid(gate)).T.astype(BF16)
        x2 = x_ref[...] + _dot(ya_ref[...], wa_ref[...]) + _dot(yb, wb_ref[...])
        o_ref[...] = _ffn_apply(x2, fg_ref, wg_s, wu_s, wd_s, act_scr)


def _outproj_even_ffn(x2d, ya, yb5, wglut, bglu, wa, wb, ffn, layer, bsz, seq):
    m, d = x2d.shape
    tm = min(FFN_TM, seq)
    per_b = seq // tm
    q = yb5.shape[-1]
    ns = _ffn_chunks(ffn)
    tile = lambda i: jnp.maximum(i - ns, 0)
    return pl.pallas_call(
        _outproj_even_ffn_kernel,
        out_shape=jax.ShapeDtypeStruct((m, d), F32),
        grid=(ns + m // tm,),
        in_specs=[
            pl.BlockSpec((tm, d), lambda i: (tile(i), 0)),
            pl.BlockSpec((tm, SSD_INNER), lambda i: (tile(i), 0)),
            pl.BlockSpec((1, S5_GROUPS, tm // q, S5_GROUP, q),
                         lambda i: (tile(i) // per_b, 0, tile(i) % per_b, 0, 0)),
            _resident(wglut.shape), _resident(bglu.shape), _resident(wa.shape), _resident(wb.shape),
        ] + _ffn_specs(*ffn, layer),
        out_specs=pl.BlockSpec((tm, d), lambda i: (tile(i), 0)),
        scratch_shapes=_ffn_scratch(ffn, tm),
        compiler_params=_params("arbitrary"),
        name="outproj_even_ffn",
    )(x2d, ya, yb5, wglut, bglu, wa, wb, *ffn)


def _head_rmsnorm(x, gain):
    assert LANES == 2 * ATT_HEAD_DIM
    first = lax.broadcasted_iota(jnp.int32, (1, LANES), 1) < ATT_HEAD_DIM
    tiles = []
    for p in range(x.shape[1] // LANES):
        xp = x[:, p * LANES:(p + 1) * LANES]
        sq = xp * xp
        s0 = jnp.sum(jnp.where(first, sq, 0.0), axis=-1, keepdims=True)
        s1 = jnp.sum(jnp.where(first, 0.0, sq), axis=-1, keepdims=True)
        ss = jnp.where(first, s0, s1)
        tiles.append(xp * lax.rsqrt(ss * (1.0 / ATT_HEAD_DIM) + EPS))
    return jnp.concatenate(tiles, axis=1) * gain


def _head_rmsnorm_t(xt, gain_col):
    c, n = xt.shape
    x3 = xt.reshape(c // ATT_HEAD_DIM, ATT_HEAD_DIM, n)
    ms = jnp.mean(x3 * x3, axis=1, keepdims=True)
    return (x3 * lax.rsqrt(ms + EPS)).reshape(c, n) * gain_col


def _pool_apply(u, prev, pos, w_ref, sc_ref):
    ext = jnp.concatenate([prev, u], axis=0)
    sums = {}
    s = ext
    w = 1
    while w < POOL_MAX:
        s = s + pltpu.roll(s, w, 0)
        w *= 2
        sums[w] = s[POOL_MAX:]
    outs = []
    for g, win in enumerate(POOL_WINDOWS):
        sl = slice(g * POOL_GROUP, (g + 1) * POOL_GROUP)
        count = jnp.minimum(pos + 1, win).astype(F32)
        pooled = sums[win][:, sl] / count - u[:, sl]
        outs.append(_dot(pooled.astype(BF16), w_ref[g]))
    return jnp.concatenate(outs, axis=1) * sc_ref[...]


def _ffn_inproj_odd_kernel(x_ref, fg_ref, wg_ref, wu_ref, wd_ref, g_ref, wqt_ref, wk_ref, wvt_ref,
                           wup_ref, qg_ref, kg_ref, pw_ref, ps_ref,
                           x1_ref, qt_ref, k_ref, vt_ref, yd_ref, wg_s, wu_s, wd_s, act_scr, carry_scr,
                           *, tiles_per_seq):
    i = pl.program_id(0)
    tm = x_ref.shape[0]
    n_stage = wg_s.shape[0]
    _ffn_stage(i, wg_ref, wu_ref, wd_ref, wg_s, wu_s, wd_s)

    @pl.when(i == 0)
    def _():
        carry_scr[...] = jnp.zeros_like(carry_scr)

    @pl.when(i >= n_stage)
    def _():
        x1 = _ffn_apply(x_ref[...], fg_ref, wg_s, wu_s, wd_s, act_scr)
        x1_ref[...] = x1
        h = _rmsnorm_bf16(x1, g_ref[...])
        nt = (((1,), (1,)), ((), ()))
        qt = lax.dot_general(wqt_ref[...], h, nt, preferred_element_type=F32)
        qt = _head_rmsnorm_t(qt, qg_ref[...]) * (ATT_HEAD_DIM ** -0.5 * LOG2E)
        qt_ref[0] = qt.astype(BF16)
        k_ref[...] = _head_rmsnorm(_dot(h, wk_ref[...]), kg_ref[...]).astype(BF16)
        vt_ref[0] = lax.dot_general(wvt_ref[...], h, nt, preferred_element_type=F32).astype(BF16)
        u = _dot(h, wup_ref[...])
        tile_in_seq = (i - n_stage) % tiles_per_seq
        prev = jnp.where(tile_in_seq == 0, 0.0, carry_scr[...])
        pos = tile_in_seq * tm + lax.broadcasted_iota(jnp.int32, (tm, 1), 0)
        yd_ref[...] = _pool_apply(u, prev, pos, pw_ref, ps_ref).astype(yd_ref.dtype)
        carry_scr[...] = u[tm - POOL_MAX:]


def _ffn_inproj_odd(x2d, ffn, gain, wqt, wk, wvt, wup, qg_col, kg, w_pool, pool_scale, layer, bsz, seq):
    m, d = x2d.shape
    tm = min(FFN_TM, seq)
    per_b = seq // tm
    w = ATT_WIDTH
    ns = _ffn_chunks(ffn)
    tl = lambda i: jnp.maximum(i - ns, 0)
    tile = pl.BlockSpec((tm, w), lambda i: (tl(i), 0))
    tile_t = pl.BlockSpec((1, w, tm), lambda i: (tl(i) // per_b, 0, tl(i) % per_b))
    row_tile = pl.BlockSpec((tm, d), lambda i: (tl(i), 0))
    tok = jax.ShapeDtypeStruct((m, w), BF16)
    chan = jax.ShapeDtypeStruct((bsz, w, seq), BF16)
    return pl.pallas_call(
        functools.partial(_ffn_inproj_odd_kernel, tiles_per_seq=per_b),
        out_shape=(jax.ShapeDtypeStruct((m, d), F32), chan, tok, chan, tok),
        grid=(ns + m // tm,),
        in_specs=[row_tile] + _ffn_specs(*ffn, layer) + [
            pl.BlockSpec((None, 1, d), lambda i: (layer, 0, 0)),
            _resident(wqt.shape), _resident(wk.shape), _resident(wvt.shape), _resident(wup.shape),
            _resident(qg_col.shape), _resident(kg.shape),
            _resident(w_pool.shape), _resident(pool_scale.shape),
        ],
        out_specs=(row_tile, tile_t, tile, tile_t, tile),
        scratch_shapes=_ffn_scratch(ffn, tm) + [pltpu.VMEM((POOL_MAX, w), F32)],
        compiler_params=_params("arbitrary"),
        name="ffn_inproj_odd",
    )(x2d, *ffn, gain, wqt, wk, wvt, wup, qg_col, kg, w_pool, pool_scale)


def _attn_kernel(qt_ref, k0_ref, k1_ref, k2_ref, vt0_ref, vt1_ref, vt2_ref, brow_ref, o_ref, bias_ref):
    t = pl.program_id(1)
    nseq, tq = o_ref.shape[:2]
    nk = 3 * tq

    @pl.when((pl.program_id(0) == 0) & (t == 0))
    def _():
        keyi = lax.broadcasted_iota(jnp.int32, (nk, tq), 0)
        qryi = lax.broadcasted_iota(jnp.int32, (nk, tq), 1)
        rel_chunk = keyi // ATT_CHUNK - (2 * tq // ATT_CHUNK - LEFT_CHUNKS) - qryi // ATT_CHUNK
        rel_chunk = jnp.where(rel_chunk >= 0, rel_chunk, LEFT_CHUNKS + 1)
        for hh in range(ATT_HEADS):
            base = jnp.broadcast_to(brow_ref[hh:hh + 1, :], (nk, 4 * tq))
            shifted = pltpu.roll(base, 0, 1, stride=1, stride_axis=0)
            bias = shifted[:, nk:] * LOG2E
            for var in range(3):
                first_key = (2 - var) * tq
                keep = jnp.where(keyi >= first_key, rel_chunk, LEFT_CHUNKS + 1) <= LEFT_CHUNKS
                bias_ref[var, hh] = jnp.where(keep, bias, -jnp.inf)

    var = jnp.minimum(t, 2)
    qt = [qt_ref[s] for s in range(nseq)]
    kcat = [jnp.concatenate([k0_ref[s], k1_ref[s], k2_ref[s]], axis=0) for s in range(nseq)]
    vtcat = [jnp.concatenate([vt0_ref[s], vt1_ref[s], vt2_ref[s]], axis=1) for s in range(nseq)]
    hd = ATT_HEAD_DIM
    heads_per_slab = LANES // hd
    no_q = jnp.zeros((LANES - hd, tq), qt[0].dtype)
    ones_rows = jnp.ones((16, nk), qt[0].dtype)
    n_tiles = nk // tq

    def scores(s, head, j):
        p, hh = divmod(head, heads_per_slab)
        rows = slice(head * hd, (head + 1) * hd)
        qth = jnp.concatenate([qt[s][rows]] + [no_q] if hh == 0 else [no_q] + [qt[s][rows]], axis=0)
        ks = slice(j * tq, (j + 1) * tq)
        return _dot(kcat[s][ks, p * LANES:(p + 1) * LANES], qth) + bias_ref[var, head, ks, :]

    items = [(s, head, j) for s in range(nseq) for head in range(ATT_HEADS) for j in range(n_tiles)]
    ahead = [scores(*it) for it in items[:ATT_LOOKAHEAD]]
    outs, parts, tops = [], [], []
    for idx, (s, head, j) in enumerate(items):
        st = ahead.pop(0)
        if idx + ATT_LOOKAHEAD < len(items):
            ahead.append(scores(*items[idx + ATT_LOOKAHEAD]))
        rows = slice(head * hd, (head + 1) * hd)
        ks = slice(j * tq, (j + 1) * tq)
        top = jnp.maximum(jnp.max(st, axis=0, keepdims=True), MASKED_FLOOR)
        e = jnp.exp2(st - top).astype(BF16)
        vth = jnp.concatenate([vtcat[s][rows, ks], ones_rows[:, ks]], axis=0)
        parts.append(_dot(vth, e))
        tops.append(top)
        if j == n_tiles - 1:
            top_all = functools.reduce(jnp.maximum, tops)
            ot = sum(part * jnp.exp2(tp - top_all) for part, tp in zip(parts, tops))
            outs.append(ot[:hd] / ot[hd:hd + 1])
            parts, tops = [], []
            if head % heads_per_slab == heads_per_slab - 1:
                p = head // heads_per_slab
                o_ref[s, :, p * LANES:(p + 1) * LANES] = (
                    jnp.concatenate(outs, axis=0).T.astype(o_ref.dtype))
                outs = []


def _attention(qt, k, vt, brow, bsz, seq):
    m, w = k.shape
    tq = ATT_T
    assert 2 * tq >= LEFT_CHUNKS * ATT_CHUNK and tq > MAX_REL and seq % tq == 0
    nt = seq // tq
    nb = math.gcd(bsz, ATT_PER_STEP)
    back = lambda n: (lambda b, t: (b, jnp.maximum(t - n, 0), 0))
    back_t = lambda n: (lambda b, t: (b, 0, jnp.maximum(t - n, 0)))
    tok = lambda f: pl.BlockSpec((nb, tq, w), f)
    chan = lambda f: pl.BlockSpec((nb, w, tq), f)
    k3 = k.reshape(bsz, seq, w)
    out = pl.pallas_call(
        _attn_kernel,
        out_shape=jax.ShapeDtypeStruct((bsz, seq, w), BF16),
        grid=(bsz // nb, nt),
        in_specs=[chan(back_t(0)), tok(back(2)), tok(back(1)), tok(back(0)),
                  chan(back_t(2)), chan(back_t(1)), chan(back_t(0)),
                  pl.BlockSpec(brow.shape, lambda b, t: (0, 0))],
        out_specs=tok(back(0)),
        scratch_shapes=[pltpu.VMEM((3, ATT_HEADS, 3 * tq, tq), F32)],
        compiler_params=_params("arbitrary", "arbitrary"),
        name="band_attention",
    )(qt, k3, k3, k3, vt, vt, vt, brow)
    return out.reshape(m, w)


def _bias_rows(rel_bias, tq):
    rb = rel_bias.astype(F32)
    nh = rb.shape[0]
    near = jnp.broadcast_to(rb[:, :1], (nh, tq - MAX_REL))
    far = jnp.broadcast_to(rb[:, -1:], (nh, 3 * tq - MAX_REL - 1))
    return jnp.concatenate([near, rb, far], axis=1)


def _outproj_odd_ffn_kernel(x_ref, yc_ref, yd_ref, wa_ref, wb_ref, fg_ref, wg_ref, wu_ref, wd_ref,
                            o_ref, wg_s, wu_s, wd_s, act_scr):
    i = pl.program_id(0)
    _ffn_stage(i, wg_ref, wu_ref, wd_ref, wg_s, wu_s, wd_s)

    @pl.when(i >= wg_s.shape[0])
    def _():
        x2 = x_ref[...] + _dot(yc_ref[...], wa_ref[...]) + _dot(yd_ref[...], wb_ref[...])
        o_ref[...] = _ffn_apply(x2, fg_ref, wg_s, wu_s, wd_s, act_scr)


def _outproj_odd_ffn(x2d, yc, yd, wa, wb, ffn, layer):
    m, d = x2d.shape
    tm = min(OUT_TM, m)
    ns = _ffn_chunks(ffn)
    tile = lambda i: jnp.maximum(i - ns, 0)
    return pl.pallas_call(
        _outproj_odd_ffn_kernel,
        out_shape=jax.ShapeDtypeStruct((m, d), F32),
        grid=(ns + m // tm,),
        in_specs=[
            pl.BlockSpec((tm, d), lambda i: (tile(i), 0)),
            pl.BlockSpec((tm, ATT_WIDTH), lambda i: (tile(i), 0)),
            pl.BlockSpec((tm, ATT_WIDTH), lambda i: (tile(i), 0)),
            _resident(wa.shape), _resident(wb.shape),
        ] + _ffn_specs(*ffn, layer),
        out_specs=pl.BlockSpec((tm, d), lambda i: (tile(i), 0)),
        scratch_shapes=_ffn_scratch(ffn, tm),
        compiler_params=_params("arbitrary"),
        name="outproj_odd_ffn",
    )(x2d, yc, yd, wa, wb, *ffn)


def _even_layer(x2d, ffn1, ffn2, gain, layer, bsz, seq, even_w_in, even_w_out, ssd_conv_w, ssd_conv_b,
                ssd_dt_bias, ssd_a_log, ssd_d, ssd_norm, s5_a_re, s5_a_im, s5_log_dt, s5_b_re, s5_b_im,
                s5_c_re, s5_c_im, s5_d, s5_w_glu, s5_b_glu):
    i = layer // 2
    w_in = even_w_in[i]
    o1 = SSD_INNER
    o2 = o1 + SSD_XBC
    o3 = o2 + SSD_HEADS
    wz = w_in[:, :o1].astype(BF16)
    wx = w_in[:, o1:o2].astype(BF16)
    wut = jnp.concatenate([w_in[:, o3:].T, w_in[:, o2:o3].T], axis=0)
    wut = jnp.pad(wut, ((0, -wut.shape[0] % 16), (0, 0))).astype(BF16)
    x1, z, xbc, dtt, ut = _ffn_inproj_even(x2d, ffn1, gain, wz, wx, wut, layer, bsz, seq)

    col = lambda v: v.astype(F32)[:, None]
    expand = (jnp.arange(LANES)[:, None] == (jnp.arange(SSD_INNER)[None, :] // SSD_HEAD_DIM)).astype(BF16)
    ya = _ssd(z, xbc, dtt, ssd_conv_w[i], ssd_conv_b[i][None, :], col(ssd_dt_bias[i]),
              col(ssd_a_log[i]), jnp.repeat(ssd_d[i], SSD_HEAD_DIM)[None, :], ssd_norm[i][None, :],
              expand, bsz, seq)

    q = min(S5_Q, seq)
    are, aim = s5_a_re[i], s5_a_im[i]
    dup = lambda v: jnp.concatenate([v, v], axis=-1)
    arow = jnp.stack([dup(are), dup(aim)], axis=1)
    ldt = s5_log_dt[i][:, None, None]
    bd = jnp.concatenate([jnp.swapaxes(s5_b_re[i], 1, 2), jnp.swapaxes(s5_b_im[i], 1, 2)], axis=-1)
    cd_re, cd_im = dup(s5_c_re[i]), dup(s5_c_im[i])
    d_rows = jnp.broadcast_to(s5_d[i][:, :, None], (S5_GROUPS, S5_GROUP, q))
    yb5 = _s5(ut, arow, ldt, bd, cd_re, cd_im, d_rows)

    w_out = even_w_out[i]
    return _outproj_even_ffn(x1, ya, yb5, s5_w_glu[i].T.astype(BF16), s5_b_glu[i][:, None],
                             w_out[:SSD_INNER].astype(BF16), w_out[SSD_INNER:].astype(BF16),
                             ffn2, layer, bsz, seq)


def _odd_layer(x2d, ffn1, ffn2, gain, layer, bsz, seq, odd_w_in, odd_w_out, attn_q_norm, attn_k_norm,
               attn_rel_bias, pool_w, pool_scale):
    i = layer // 2
    w_in = odd_w_in[i].astype(BF16)
    w = ATT_WIDTH
    tile_gain = lambda v: jnp.tile(v.astype(F32), ATT_HEADS)[None, :]
    x1, qt, k, vt, yd = _ffn_inproj_odd(
        x2d, ffn1, gain, w_in[:, :w].T, w_in[:, w:2 * w], w_in[:, 2 * w:3 * w].T, w_in[:, 3 * w:],
        tile_gain(attn_q_norm[i]).T, tile_gain(attn_k_norm[i]),
        pool_w[i].astype(BF16), pool_scale[i][None, :], layer, bsz, seq)
    yc = _attention(qt, k, vt, _bias_rows(attn_rel_bias[i], ATT_T), bsz, seq)
    w_out = odd_w_out[i]
    return _outproj_odd_ffn(x1, yc, yd, w_out[:w].astype(BF16), w_out[w:].astype(BF16), ffn2, layer)


def kernel(x, ffn1_norm, ffn1_w_gate, ffn1_w_up, ffn1_w_down, mix_norm, even_w_in, even_w_out, ssd_conv_w, ssd_conv_b, ssd_dt_bias, ssd_a_log, ssd_d, ssd_norm, s5_a_re, s5_a_im, s5_log_dt, s5_b_re, s5_b_im, s5_c_re, s5_c_im, s5_d, s5_w_glu, s5_b_glu, odd_w_in, odd_w_out, attn_q_norm, attn_k_norm, attn_rel_bias, pool_w, pool_scale, ffn2_norm, ffn2_w_gate, ffn2_w_up, ffn2_w_down):
    bsz, seq, d = x.shape
    depth = ffn1_norm.shape[0]
    x2d = x.reshape(bsz * seq, d)
    g3 = lambda v: v.astype(F32)[:, None, :]
    f1n, mxn, f2n = g3(ffn1_norm), g3(mix_norm), g3(ffn2_norm)
    as32 = lambda v: v.astype(F32)
    ffn1 = (f1n, as32(ffn1_w_gate), as32(ffn1_w_up), as32(ffn1_w_down))
    ffn2 = (f2n, as32(ffn2_w_gate), as32(ffn2_w_up), as32(ffn2_w_down))
    for layer in range(depth):
        if layer % 2 == 0:
            x2d = _even_layer(x2d, ffn1, ffn2, mxn, layer, bsz, seq, even_w_in, even_w_out, ssd_conv_w,
                              ssd_conv_b, ssd_dt_bias, ssd_a_log, ssd_d, ssd_norm, s5_a_re, s5_a_im,
                              s5_log_dt, s5_b_re, s5_b_im, s5_c_re, s5_c_im, s5_d, s5_w_glu, s5_b_glu)
        else:
            x2d = _odd_layer(x2d, ffn1, ffn2, mxn, layer, bsz, seq, odd_w_in, odd_w_out, attn_q_norm,
                             attn_k_norm, attn_rel_bias, pool_w, pool_scale)
    return x2d.reshape(bsz, seq, d)
```
